```python
import jax, jax.numpy as jnp
from jax import lax
import numpy as np


D_MODEL = 1024
BATCH = 8
SEQ = 4096
DEPTH = 4

GRID_W = 64
CTX_LEN = 256
MIXER_KINDS = ('rwkv7', 'rglru', 'natten')
RMS_EPS = 1e-6

RW_HEAD_DIM = 64
RW_HEADS = D_MODEL // RW_HEAD_DIM
RW_LORA = 64
RW_GN_EPS = 64e-5

LRU_WIDTH = 1408
LRU_BLOCKS = 16
LRU_BLOCK_DIM = LRU_WIDTH // LRU_BLOCKS
LRU_CONV = 4
LRU_C = 8.0

NA_HEAD_DIM = 64
NA_HEADS = D_MODEL // NA_HEAD_DIM
NA_WIDTH = NA_HEADS * NA_HEAD_DIM
WIN_H = 8
WIN_W = 16
ROPE_THETA = 10000.0

kernel_name = 'hybrid_rwkv7_rglru_natten_dit'


def rms_norm(x, g, eps=RMS_EPS):
    xf = x.astype(jnp.float32)
    xf = xf * lax.rsqrt(jnp.mean(xf * xf, axis=-1, keepdims=True) + eps)
    return xf.astype(x.dtype) * g


def centred_shift(h):
    prev = jnp.pad(h[:, :-1], ((0, 0), (1, 0), (0, 0)))
    nxt = jnp.pad(h[:, 1:], ((0, 0), (0, 1), (0, 0)))
    return 0.5 * (prev + nxt)


def rwkv7_project(h, p):
    B, T, D = h.shape
    heads = lambda t: t.reshape(B, T, RW_HEADS, RW_HEAD_DIM)
    xx = centred_shift(h) - h
    mu = p['mu']
    xr, xw, xk, xv, xa, xg = [h + xx * mu[j] for j in range(6)]
    w_in = p['w_in']
    r = heads(xr @ w_in[0])
    k = heads(xk @ w_in[1])
    v = heads(xv @ w_in[2])
    gate = jax.nn.silu(xg @ w_in[3])
    k_k = p['k_ka'][0].reshape(RW_HEADS, RW_HEAD_DIM)
    k_a = p['k_ka'][1].reshape(RW_HEADS, RW_HEAD_DIM)
    kkf = (k * k_k).astype(jnp.float32)
    kk = (kkf / jnp.maximum(jnp.sqrt(jnp.sum(kkf * kkf, -1, keepdims=True)), 1e-12)).astype(k.dtype)
    b0, down, up = p['lora_b0'], p['lora_down'], p['lora_up']
    decay, a, k_dir = [], [], []
    for d in range(2):
        w_log = -jax.nn.softplus(-(b0[d, 0] + jnp.tanh(xw @ down[d, 0]) @ up[d, 0])) - 0.5
        a_d = heads(jax.nn.sigmoid(b0[d, 1] + (xa @ down[d, 1]) @ up[d, 1]))
        decay.append(heads(jnp.exp(-jnp.exp(w_log))))
        a.append(a_d)
        k_dir.append(k * (1.0 + (a_d - 1.0) * k_a))
    return dict(r=r, v=v, kk=kk, decay=decay, a=a, k=k_dir, gate=gate)


def rwkv7_scan(proj, d, s0, reverse):
    def step(S, inp):
        r_t, w_t, k_t, kk_t, a_t, v_t = inp
        sa = jnp.einsum('bhvk,bhk->bhv', S, -kk_t)
        S = (S * w_t[:, :, None, :] + sa[..., None] * (kk_t * a_t)[:, :, None, :]
             + v_t[..., None] * k_t[:, :, None, :])
        return S, jnp.einsum('bhvk,bhk->bhv', S, r_t)
    xs = tuple(jnp.moveaxis(t, 1, 0) for t in
               (proj['r'], proj['decay'][d], proj['k'][d], proj['kk'], proj['a'][d], proj['v']))
    s_final, ys = lax.scan(step, s0, xs, reverse=reverse)
    return jnp.moveaxis(ys, 0, 1), s_final


def rwkv7_output(proj, y, p):
    B, T, H, Dh = y.shape
    yf = y.astype(jnp.float32)
    mean = jnp.mean(yf, -1, keepdims=True)
    var = jnp.mean(jnp.square(yf - mean), -1, keepdims=True)
    yn = ((yf - mean) * lax.rsqrt(var + RW_GN_EPS)).astype(y.dtype).reshape(B, T, H * Dh)
    yn = yn * p['gn'][0] + p['gn'][1]
    r, v, r_k = proj['r'], proj['v'], p['r_k']
    bonus = (jnp.sum(r * proj['k'][0] * r_k, -1, keepdims=True)
             + jnp.sum(r * proj['k'][1] * r_k, -1, keepdims=True)) * v
    return ((yn + bonus.reshape(B, T, H * Dh)) * proj['gate']) @ p['w_out']


def rwkv7_mixer(h_l, h_c, p, ctx_out):
    proj_c = rwkv7_project(h_c, p)
    proj_l = rwkv7_project(h_l, p)
    s0 = jnp.zeros((h_l.shape[0], RW_HEADS, RW_HEAD_DIM, RW_HEAD_DIM), h_l.dtype)
    ys_c, ys_l = [], []
    for d in range(2):
        rev = d == 1
        y_c, s_c = rwkv7_scan(proj_c, d, s0, rev)
        y_l, _ = rwkv7_scan(proj_l, d, s_c, rev)
        ys_c.append(y_c)
        ys_l.append(y_l)
    out_l = rwkv7_output(proj_l, ys_l[0] + ys_l[1], p)
    out_c = rwkv7_output(proj_c, ys_c[0] + ys_c[1], p) if ctx_out else None
    return out_l, out_c


def depthwise_conv_centred(x, w, b):
    T = x.shape[1]
    xp = jnp.pad(x, ((0, 0), (LRU_CONV // 2, LRU_CONV - 1 - LRU_CONV // 2), (0, 0)))
    out = b + xp[:, 0:T] * w[0]
    for j in range(1, LRU_CONV):
        out = out + xp[:, j:j + T] * w[j]
    return out


def block_diag(x, w, b):
    B, T, _ = x.shape
    xb = x.reshape(B, T, LRU_BLOCKS, LRU_BLOCK_DIM)
    return jnp.einsum('btnc,ncd->btnd', xb, w).reshape(B, T, LRU_WIDTH) + b


def rglru_coeffs(x, p, d):
    r = jax.nn.sigmoid(block_diag(x, p['gate_w'][d, 0], p['gate_b'][d, 0]))
    i = jax.nn.sigmoid(block_diag(x, p['gate_w'][d, 1], p['gate_b'][d, 1]))
    log_a = -LRU_C * r * jax.nn.softplus(-p['lam'][d])
    a = jnp.exp(log_a)
    b = jnp.sqrt(-jnp.expm1(2.0 * log_a)) * (i * x)
    return a, b


def linear_scan(a, b, h0, reverse):
    def combine(e1, e2):
        a1, b1 = e1
        a2, b2 = e2
        return a1 * a2, a2 * b1 + b2
    a_cum, b_cum = lax.associative_scan(combine, (a, b), axis=1, reverse=reverse)
    h = a_cum * h0[:, None, :] + b_cum
    final = h[:, 0] if reverse else h[:, -1]
    return h, final


def rglru_mixer(h_l, h_c, p, ctx_out):
    def pre(h):
        z = h @ p['w_in']
        xr, g = z[..., :LRU_WIDTH], z[..., LRU_WIDTH:]
        return depthwise_conv_centred(xr, p['conv_w'], p['conv_b']), jax.nn.silu(g)
    x_c, g_c = pre(h_c)
    x_l, g_l = pre(h_l)
    h0 = jnp.zeros((h_l.shape[0], LRU_WIDTH), h_l.dtype)
    hs_c, hs_l = [], []
    for d in range(2):
        rev = d == 1
        a, b = rglru_coeffs(x_c, p, d)
        hc, hc_final = linear_scan(a, b, h0, rev)
        a, b = rglru_coeffs(x_l, p, d)
        hl, _ = linear_scan(a, b, hc_final, rev)
        hs_c.append(hc)
        hs_l.append(hl)
    out_l = ((hs_l[0] + hs_l[1]) * g_l) @ p['w_out']
    out_c = ((hs_c[0] + hs_c[1]) * g_c) @ p['w_out'] if ctx_out else None
    return out_l, out_c


def axial_rope(x, row, col):
    half = x.shape[-1] // 2
    nfreq = half // 2
    inv = ROPE_THETA ** (-jnp.arange(nfreq, dtype=jnp.float32) / nfreq)
    def rot(xp, pos):
        ang = pos.astype(jnp.float32)[:, None] * inv
        cos = jnp.cos(ang)[None, :, None, :]
        sin = jnp.sin(ang)[None, :, None, :]
        x1, x2 = xp[..., :nfreq], xp[..., nfreq:]
        return jnp.concatenate([x1 * cos - x2 * sin, x1 * sin + x2 * cos], -1).astype(x.dtype)
    return jnp.concatenate([rot(x[..., :half], row), rot(x[..., half:], col)], -1)


def natten_mixer(h_l, h_c, p, ctx_out):
    B, T, _ = h_l.shape
    rows = T // GRID_W
    kh = min(WIN_H, rows)
    H, Dh = NA_HEADS, NA_HEAD_DIM
    scale = Dh ** -0.5

    def project(h):
        n = h.shape[1]
        q, k, v, g = jnp.split(h @ p['w_in'], 4, axis=-1)
        q = rms_norm(q.reshape(B, n, H, Dh), p['qk_g'][0])
        k = rms_norm(k.reshape(B, n, H, Dh), p['qk_g'][1])
        return q, k, v.reshape(B, n, H, Dh), jax.nn.silu(g)

    q_c, k_c, v_c, g_c = project(h_c)
    q_l, k_l, v_l, g_l = project(h_l)
    pos = jnp.arange(T)
    row, col = pos // GRID_W, pos % GRID_W
    q_rot = axial_rope(q_l, row, col)
    k_rot = axial_rope(k_l, row, col)
    to_rows = lambda t: jnp.moveaxis(t.reshape(B, rows, GRID_W, H, Dh), 1, 0)
    k_grid = k_rot.reshape(B, rows, GRID_W, H, Dh)
    v_grid = v_l.reshape(B, rows, GRID_W, H, Dh)

    cols = np.arange(GRID_W)
    c_start = np.clip(cols - WIN_W // 2, 0, GRID_W - WIN_W)
    col_ok = (cols[None, :] >= c_start[:, None]) & (cols[None, :] < c_start[:, None] + WIN_W)
    col_ok = jnp.asarray(col_ok)[:, None, :]
    dc_idx = np.clip(cols[None, :] - cols[:, None] + WIN_W - 1, 0, 2 * WIN_W - 2)
    n_band = kh * GRID_W

    def row_block(args):
        r, q_r, q_p = args
        start = jnp.clip(r - kh // 2, 0, rows - kh)
        k_band = lax.dynamic_slice_in_dim(k_grid, start, kh, axis=1)
        v_band = lax.dynamic_slice_in_dim(v_grid, start, kh, axis=1)
        s_band = jnp.einsum('bqhd,bikhd->bhqik', q_r, k_band).astype(jnp.float32) * scale
        dr_idx = start + jnp.arange(kh) - r + WIN_H - 1
        bias = p['rpb'][:, dr_idx[None, :, None], dc_idx[:, None, :]]
        s_band = jnp.where(col_ok, s_band + bias.astype(jnp.float32), -jnp.inf)
        s_band = s_band.reshape(B, H, GRID_W, n_band)
        s_ctx = jnp.einsum('bqhd,bchd->bhqc', q_p, k_c).astype(jnp.float32) * scale
        prob = jax.nn.softmax(jnp.concatenate([s_band, s_ctx], -1), axis=-1).astype(v_l.dtype)
        o = jnp.einsum('bhqj,bjhd->bqhd', prob[..., :n_band], v_band.reshape(B, n_band, H, Dh))
        return o + jnp.einsum('bhqc,bchd->bqhd', prob[..., n_band:], v_c)

    o = lax.map(row_block, (jnp.arange(rows), to_rows(q_rot), to_rows(q_l)))
    o = jnp.moveaxis(o, 0, 1).reshape(B, T, NA_WIDTH)
    out_l = (o * g_l) @ p['w_out']
    out_c = None
    if ctx_out:
        s = jnp.einsum('bqhd,bkhd->bhqk', q_c, k_c).astype(jnp.float32) * scale
        prob = jax.nn.softmax(s, axis=-1).astype(v_c.dtype)
        o_c = jnp.einsum('bhqk,bkhd->bqhd', prob, v_c).reshape(B, h_c.shape[1], NA_WIDTH)
        out_c = (o_c * g_c) @ p['w_out']
    return out_l, out_c


MIXERS = {'rwkv7': rwkv7_mixer, 'rglru': rglru_mixer, 'natten': natten_mixer}


def layer_forward(x, ctx, c_act, cctx_act, p, kind, ctx_out):
    mod_l = c_act @ p['ada_w'] + p['ada_b']
    mod_c = cctx_act @ p['ada_w'] + p['ada_b']
    shift_l, scale_l, gate_l = jnp.split(mod_l[:, None, :], 3, axis=-1)
    shift_c, scale_c, gate_c = jnp.split(mod_c, 3, axis=-1)
    h_l = rms_norm(x, p['norm_g']) * (1.0 + scale_l) + shift_l
    h_c = rms_norm(ctx, p['norm_g']) * (1.0 + scale_c) + shift_c
    y_l, y_c = MIXERS[kind](h_l, h_c, p, ctx_out)
    x = x + gate_l * y_l
    if ctx_out:
        ctx = ctx + gate_c * y_c
    return x, ctx


def _normal(key, shape, scale):
    return scale * jax.random.normal(key, shape, jnp.float32)


def _ada_params(ks, pre):
    return {
        pre + 'norm_g': 1.0 + _normal(ks[0], (D_MODEL,), 0.02),
        pre + 'ada_w': _normal(ks[1], (D_MODEL, 3 * D_MODEL), 0.5 * D_MODEL ** -0.5),
        pre + 'ada_b': _normal(ks[2], (3 * D_MODEL,), 0.01),
    }


def _rwkv7_params(key, pre):
    ks = jax.random.split(key, 16)
    d = _ada_params(ks, pre)
    d[pre + 'w_in'] = _normal(ks[3], (4, D_MODEL, D_MODEL), D_MODEL ** -0.5)
    d[pre + 'mu'] = jax.random.uniform(ks[4], (6, D_MODEL), jnp.float32)
    w0 = jax.random.uniform(ks[5], (2, D_MODEL), jnp.float32, minval=-6.0, maxval=0.0)
    a0 = _normal(ks[6], (2, D_MODEL), 0.1)
    d[pre + 'lora_b0'] = jnp.stack([w0, a0], axis=1)
    d[pre + 'lora_down'] = _normal(ks[7], (2, 2, D_MODEL, RW_LORA), D_MODEL ** -0.5)
    d[pre + 'lora_up'] = _normal(ks[8], (2, 2, RW_LORA, D_MODEL), 0.1 * RW_LORA ** -0.5)
    d[pre + 'k_ka'] = jnp.stack([0.85 + _normal(ks[9], (D_MODEL,), 0.02),
                                 1.0 + _normal(ks[10], (D_MODEL,), 0.02)])
    d[pre + 'r_k'] = _normal(ks[11], (RW_HEADS, RW_HEAD_DIM), 0.1)
    d[pre + 'gn'] = jnp.stack([1.0 + _normal(ks[12], (D_MODEL,), 0.02),
                               _normal(ks[13], (D_MODEL,), 0.01)])
    d[pre + 'w_out'] = _normal(ks[14], (D_MODEL, D_MODEL), D_MODEL ** -0.5)
    return d


def _rglru_params(key, pre):
    ks = jax.random.split(key, 12)
    d = _ada_params(ks, pre)
    d[pre + 'w_in'] = _normal(ks[3], (D_MODEL, 2 * LRU_WIDTH), D_MODEL ** -0.5)
    d[pre + 'conv_w'] = _normal(ks[4], (LRU_CONV, LRU_WIDTH), LRU_CONV ** -0.5)
    d[pre + 'conv_b'] = _normal(ks[5], (LRU_WIDTH,), 0.01)
    d[pre + 'gate_w'] = _normal(ks[6], (2, 2, LRU_BLOCKS, LRU_BLOCK_DIM, LRU_BLOCK_DIM), LRU_BLOCK_DIM ** -0.5)
    d[pre + 'gate_b'] = _normal(ks[7], (2, 2, LRU_WIDTH), 0.01)
    u = jax.random.uniform(ks[8], (2, LRU_WIDTH), jnp.float32, minval=0.9, maxval=0.999)
    a_base = u ** (1.0 / LRU_C)
    d[pre + 'lam'] = jnp.log(a_base) - jnp.log1p(-a_base)
    d[pre + 'w_out'] = _normal(ks[9], (LRU_WIDTH, D_MODEL), LRU_WIDTH ** -0.5)
    return d


def _natten_params(key, pre):
    ks = jax.random.split(key, 8)
    d = _ada_params(ks, pre)
    d[pre + 'w_in'] = _normal(ks[3], (D_MODEL, 4 * NA_WIDTH), D_MODEL ** -0.5)
    d[pre + 'qk_g'] = 1.0 + _normal(ks[4], (2, NA_HEAD_DIM), 0.02)
    d[pre + 'rpb'] = _normal(ks[5], (NA_HEADS, 2 * WIN_H - 1, 2 * WIN_W - 1), 0.1)
    d[pre + 'w_out'] = _normal(ks[6], (NA_WIDTH, D_MODEL), NA_WIDTH ** -0.5)
    return d


def setup_inputs(seed: int = 0) -> dict:
    key = jax.random.key(seed)
    ks = jax.random.split(key, 4 + DEPTH)
    inputs = {
        'x': jax.random.normal(ks[0], (BATCH, SEQ, D_MODEL), jnp.float32),
        'c': jax.random.normal(ks[1], (BATCH, D_MODEL), jnp.float32),
        'ctx': jax.random.normal(ks[2], (BATCH, CTX_LEN, D_MODEL), jnp.float32),
        'c_ctx': jax.random.normal(ks[3], (D_MODEL,), jnp.float32),
    }
    makers = (_rwkv7_params, _rglru_params, _natten_params)
    for i in range(DEPTH):
        inputs.update(makers[i % len(makers)](ks[4 + i], 'l%d_' % i))
    return inputs


def reference(x, c, ctx, c_ctx,
              l0_norm_g, l0_ada_w, l0_ada_b, l0_w_in, l0_mu, l0_lora_b0, l0_lora_down, l0_lora_up,
              l0_k_ka, l0_r_k, l0_gn, l0_w_out,
              l1_norm_g, l1_ada_w, l1_ada_b, l1_w_in, l1_conv_w, l1_conv_b, l1_gate_w, l1_gate_b,
              l1_lam, l1_w_out,
              l2_norm_g, l2_ada_w, l2_ada_b, l2_w_in, l2_qk_g, l2_rpb, l2_w_out,
              l3_norm_g, l3_ada_w, l3_ada_b, l3_w_in, l3_mu, l3_lora_b0, l3_lora_down, l3_lora_up,
              l3_k_ka, l3_r_k, l3_gn, l3_w_out):
    layer_params = (
        dict(norm_g=l0_norm_g, ada_w=l0_ada_w, ada_b=l0_ada_b, w_in=l0_w_in, mu=l0_mu,
             lora_b0=l0_lora_b0, lora_down=l0_lora_down, lora_up=l0_lora_up, k_ka=l0_k_ka,
             r_k=l0_r_k, gn=l0_gn, w_out=l0_w_out),
        dict(norm_g=l1_norm_g, ada_w=l1_ada_w, ada_b=l1_ada_b, w_in=l1_w_in, conv_w=l1_conv_w,
             conv_b=l1_conv_b, gate_w=l1_gate_w, gate_b=l1_gate_b, lam=l1_lam, w_out=l1_w_out),
        dict(norm_g=l2_norm_g, ada_w=l2_ada_w, ada_b=l2_ada_b, w_in=l2_w_in, qk_g=l2_qk_g,
             rpb=l2_rpb, w_out=l2_w_out),
        dict(norm_g=l3_norm_g, ada_w=l3_ada_w, ada_b=l3_ada_b, w_in=l3_w_in, mu=l3_mu,
             lora_b0=l3_lora_b0, lora_down=l3_lora_down, lora_up=l3_lora_up, k_ka=l3_k_ka,
             r_k=l3_r_k, gn=l3_gn, w_out=l3_w_out),
    )
    c_act = jax.nn.silu(c)
    cctx_act = jax.nn.silu(c_ctx)
    for i in range(DEPTH):
        x, ctx = layer_forward(x, ctx, c_act, cctx_act, layer_params[i],
                               MIXER_KINDS[i % len(MIXER_KINDS)], ctx_out=i < DEPTH - 1)
    return x
```

```python
import functools

import jax
import jax.numpy as jnp
import numpy as np
from jax import lax
from jax.experimental import pallas as pl
from jax.experimental.pallas import tpu as pltpu

F32 = jnp.float32
BF16 = jnp.bfloat16

HEAD_DIM = 64
CHUNK = 64
GRID_W = 64
WIN_H = 8
WIN_W = 16
ROPE_THETA = 10000.0
RMS_EPS = 1e-6
RW_GN_EPS = 64e-5
LRU_C = 8.0
SUBLANES = 8
NEG_BIG = -1e30
VMEM_LIMIT = 56 * 1024 * 1024


def _cparams(n_axes):
    return pltpu.CompilerParams(
        dimension_semantics=("arbitrary",) * n_axes, vmem_limit_bytes=VMEM_LIMIT)


def _bf(x):
    return x.astype(BF16)


def _dot(a, b):
    return jnp.dot(_bf(a), _bf(b), preferred_element_type=F32)


def _dot_nt(a, b):
    return lax.dot_general(_bf(a), _bf(b), (((1,), (1,)), ((), ())), preferred_element_type=F32)


def _dot_tn(a, b):
    return lax.dot_general(_bf(a), _bf(b), (((0,), (0,)), ((), ())), preferred_element_type=F32)


def _dot3(a_exact_bf16, b):
    hi = _bf(b)
    r1 = b - hi.astype(F32)
    mid = _bf(r1)
    lo = _bf(r1 - mid.astype(F32))
    d = lambda t: jnp.dot(a_exact_bf16, t, preferred_element_type=F32)
    return d(hi) + d(mid) + d(lo)


def _sigmoid(x):
    return 1.0 / (1.0 + jnp.exp(-x))


def _silu(x):
    return x * _sigmoid(x)


def _softplus(x):
    return jnp.maximum(x, 0.0) + jnp.log(1.0 + jnp.exp(-jnp.abs(x)))


def _rows(shape):
    return lax.broadcasted_iota(jnp.int32, shape, 0)


def _shift_down(x, s, fill):
    return jnp.where(_rows(x.shape) >= s, pltpu.roll(x, s, 0), fill)


def _shift_up(x, s, fill):
    n = x.shape[0]
    return jnp.where(_rows(x.shape) < n - s, pltpu.roll(x, n - s, 0), fill)


def _norm_mod(xb, g, scale1, shift):
    xf = xb.astype(F32)
    ms = jnp.mean(xf * xf, axis=-1, keepdims=True)
    return xf * lax.rsqrt(ms + RMS_EPS) * g * scale1 + shift


def _ada_kernel(c_ref, w_ref, b_ref, o_ref):
    o_ref[...] = _dot(_silu(c_ref[...]), w_ref[...]) + b_ref[...]


def ada_mod(c_all, ada_w, ada_b):
    m, d = c_all.shape
    n = ada_w.shape[1]
    tn = d
    return pl.pallas_call(
        _ada_kernel,
        grid=(n // tn,),
        in_specs=[pl.BlockSpec((m, d), lambda j: (0, 0)),
                  pl.BlockSpec((d, tn), lambda j: (0, j)),
                  pl.BlockSpec((1, tn), lambda j: (0, j))],
        out_specs=pl.BlockSpec((m, tn), lambda j: (0, j)),
        out_shape=jax.ShapeDtypeStruct((m, n), F32),
        compiler_params=_cparams(1),
        name="ada_mod",
    )(c_all, _bf(ada_w), ada_b.reshape(1, n))


def _seg_of_block(t, n_ctx_blocks):
    return (t >= n_ctx_blocks).astype(jnp.int32)


def _scan_order(c, n_ctx_blocks, n_blocks, reverse):
    if not reverse:
        return c
    return jnp.where(c < n_ctx_blocks, n_ctx_blocks - 1 - c, n_blocks - 1 + n_ctx_blocks - c)


def _in_mm_kernel(x_ref, mod_ref, g_ref, w_ref, *o_refs, splits, acts):
    h = _norm_mod(x_ref[0], g_ref[...], mod_ref[0, 0, 0:1], mod_ref[0, 0, 1:2])
    hb = _bf(h)
    off = 0
    for o_ref, n, act in zip(o_refs, splits, acts):
        z = jnp.dot(hb, w_ref[:, off:off + n], preferred_element_type=F32)
        if act == "silu":
            z = _silu(z)
        o_ref[0] = z.astype(o_ref.dtype)
        off += n


def in_matmul(x, mod, norm_g, w, splits, acts, n_ctx, tb):
    B, Tt, D = x.shape
    ncb = n_ctx // tb
    n = w.shape[1]
    assert sum(splits) == n
    kern = functools.partial(_in_mm_kernel, splits=tuple(splits), acts=tuple(acts))
    return pl.pallas_call(
        kern,
        grid=(B, Tt // tb),
        in_specs=[pl.BlockSpec((1, tb, D), lambda b, t: (b, t, 0)),
                  pl.BlockSpec((1, 1, 2, D), lambda b, t: (b, _seg_of_block(t, ncb), 0, 0)),
                  pl.BlockSpec((1, D), lambda b, t: (0, 0)),
                  pl.BlockSpec((D, n), lambda b, t: (0, 0))],
        out_specs=[pl.BlockSpec((1, tb, s), lambda b, t: (b, t, 0)) for s in splits],
        out_shape=[jax.ShapeDtypeStruct((B, Tt, s), F32) for s in splits],
        compiler_params=_cparams(2),
        name="in_matmul",
    )(x, mod, norm_g.reshape(1, D), _bf(w))


def _out_mm_kernel(*refs, n_a):
    a_refs = refs[:n_a]
    w_ref, x_ref, gate_ref, o_ref = refs[n_a:]
    a = a_refs[0][0]
    for r in a_refs[1:]:
        a = a * r[0]
    o_ref[0] = x_ref[0] + gate_ref[0, 0] * _dot(a, w_ref[...])


def out_matmul(a_list, w, x, gate, n_ctx, tb):
    B, Tt, D = x.shape
    K = w.shape[0]
    ncb = n_ctx // tb
    kern = functools.partial(_out_mm_kernel, n_a=len(a_list))
    return pl.pallas_call(
        kern,
        grid=(B, Tt // tb),
        in_specs=[pl.BlockSpec((1, tb, K), lambda b, t: (b, t, 0)) for _ in a_list] + [
            pl.BlockSpec((K, D), lambda b, t: (0, 0)),
            pl.BlockSpec((1, tb, D), lambda b, t: (b, t, 0)),
            pl.BlockSpec((1, 1, 1, D), lambda b, t: (b, _seg_of_block(t, ncb), 0, 0))],
        out_specs=pl.BlockSpec((1, tb, D), lambda b, t: (b, t, 0)),
        out_shape=jax.ShapeDtypeStruct((B, Tt, D), F32),
        compiler_params=_cparams(2),
        name="out_matmul",
    )(*a_list, _bf(w), x, gate)


_V_NORM_G, _V_MU, _V_KK, _V_KA, _V_B0 = 0, 1, 7, 8, 9
_V_ROWS = 16


def _rwkv_proj_kernel(x_ref, xp_ref, xn_ref, mod_ref, vec_ref, win_ref, dw_ref, da_ref, upw_ref,
                      upa_ref, r_ref, v_ref, g_ref, kk_ref, lw0_ref, a0_ref, k0_ref, lw1_ref,
                      a1_ref, k1_ref, *, n_ctx_blocks):
    t = pl.program_id(1)
    nb = pl.num_programs(1)
    vec = lambda i: vec_ref[i:i + 1, :]
    scale1, shift = mod_ref[0, 0, 0:1], mod_ref[0, 0, 1:2]
    g = vec(_V_NORM_G)
    h = _norm_mod(x_ref[0], g, scale1, shift)
    tb = h.shape[0]
    first = jnp.logical_or(t == 0, t == n_ctx_blocks)
    last = jnp.logical_or(t == n_ctx_blocks - 1, t == nb - 1)
    hp = _norm_mod(xp_ref[0], g, scale1, shift)[SUBLANES - 1:SUBLANES]
    hn = _norm_mod(xn_ref[0], g, scale1, shift)[0:1]
    hp = jnp.where(first, 0.0, hp)
    hn = jnp.where(last, 0.0, hn)
    h_prev = _shift_down(h, 1, jnp.broadcast_to(hp, h.shape))
    h_next = _shift_up(h, 1, jnp.broadcast_to(hn, h.shape))
    xx = 0.5 * (h_prev + h_next) - h
    mix = lambda j: _bf(h + xx * vec(_V_MU + j))
    mm = lambda a, w: jnp.dot(a, w, preferred_element_type=F32)
    xr, xw, xk, xv, xa, xg = [mix(j) for j in range(6)]
    r = mm(xr, win_ref[0])
    k = mm(xk, win_ref[1])
    v = mm(xv, win_ref[2])
    r_ref[0] = r
    v_ref[0] = v
    g_ref[0] = _silu(mm(xg, win_ref[3]))
    kk_ref[0] = k * vec(_V_KK)
    tw = _bf(jnp.tanh(mm(xw, dw_ref[...])))
    ta = _bf(mm(xa, da_ref[...]))
    k_a = vec(_V_KA)
    for d, (lw_ref, a_ref, kd_ref) in enumerate(((lw0_ref, a0_ref, k0_ref), (lw1_ref, a1_ref, k1_ref))):
        zw = vec(_V_B0 + 2 * d) + mm(tw, upw_ref[d])
        w_log = -_softplus(-zw) - 0.5
        lw_ref[0] = -jnp.exp(w_log)
        a_d = _sigmoid(vec(_V_B0 + 2 * d + 1) + mm(ta, upa_ref[d]))
        a_ref[0] = a_d
        kd_ref[0] = k * (1.0 + (a_d - 1.0) * k_a)


def rwkv_proj(x, mod, p, n_ctx, tb):
    B, Tt, D = x.shape
    lora = p['lora_down'].shape[-1]
    ncb = n_ctx // tb
    nb8 = Tt // SUBLANES
    r8 = tb // SUBLANES
    vec = jnp.zeros((_V_ROWS, D), F32)
    vec = vec.at[_V_NORM_G].set(p['norm_g']).at[_V_MU:_V_MU + 6].set(p['mu'])
    vec = vec.at[_V_KK].set(p['k_ka'][0]).at[_V_KA].set(p['k_ka'][1])
    vec = vec.at[_V_B0:_V_B0 + 4].set(p['lora_b0'].reshape(4, D))
    down, up = p['lora_down'], p['lora_up']
    dw = _bf(jnp.concatenate([down[0, 0], down[1, 0]], axis=1))
    da = _bf(jnp.concatenate([down[0, 1], down[1, 1]], axis=1))
    z = jnp.zeros((lora, D), F32)
    upw = _bf(jnp.stack([jnp.concatenate([up[0, 0], z], 0), jnp.concatenate([z, up[1, 0]], 0)]))
    upa = _bf(jnp.stack([jnp.concatenate([up[0, 1], z], 0), jnp.concatenate([z, up[1, 1]], 0)]))
    full = lambda shape: pl.BlockSpec(shape, lambda b, t: (0,) * len(shape))
    tok = pl.BlockSpec((1, tb, D), lambda b, t: (b, t, 0))
    kern = functools.partial(_rwkv_proj_kernel, n_ctx_blocks=ncb)
    outs = pl.pallas_call(
        kern,
        grid=(B, Tt // tb),
        in_specs=[tok,
                  pl.BlockSpec((1, SUBLANES, D), lambda b, t: (b, jnp.maximum(t * r8 - 1, 0), 0)),
                  pl.BlockSpec((1, SUBLANES, D), lambda b, t: (b, jnp.minimum((t + 1) * r8, nb8 - 1), 0)),
                  pl.BlockSpec((1, 1, 2, D), lambda b, t: (b, _seg_of_block(t, ncb), 0, 0)),
                  full((_V_ROWS, D)), full((4, D, D)), full((D, 2 * lora)), full((D, 2 * lora)),
                  full((2, 2 * lora, D)), full((2, 2 * lora, D))],
        out_specs=[tok] * 10,
        out_shape=[jax.ShapeDtypeStruct((B, Tt, D), F32)] * 10,
        compiler_params=_cparams(2),
        name="rwkv_proj",
    )(x, x, x, mod, vec, _bf(p['w_in']), dw, da, upw, upa)
    names = ('r', 'v', 'gate', 'kk', 'lw0', 'a0', 'k0', 'lw1', 'a1', 'k1')
    return dict(zip(names, outs))


def _level_masks(L, reverse):
    ri = lax.broadcasted_iota(jnp.int32, (L, L), 0)
    ci = lax.broadcasted_iota(jnp.int32, (L, L), 1)
    if reverse:
        ri, ci = ci, ri
    masks = []
    for j in range(int(np.log2(L))):
        same = (ri >> (j + 1)) == (ci >> (j + 1))
        masks.append(same & (((ri >> j) & 1) == 1) & (((ci >> j) & 1) == 0))
    return masks


def _unit_tri_inverse(n, masks, eye_l):
    t = eye_l - jnp.where(masks[0], n, 0.0)
    for m in masks[1:]:
        tb = _bf(t)
        c = _bf(jnp.where(m, n, 0.0))
        ct = jnp.dot(c, tb, preferred_element_type=F32)
        t = t - jnp.dot(tb, _bf(ct), preferred_element_type=F32)
    return t


def _rwkv_chunk_head(rt, kap, bt, kt, v, wl, st, tri_strict, tri_incl, eye, masks, eye_l):
    hd = v.shape[1]
    n_b = jnp.where(tri_strict, _dot_nt(kap, bt), 0.0)
    n_k = jnp.where(tri_strict, _dot_nt(kap, kt), 0.0)
    a_rb = jnp.where(tri_incl, _dot_nt(rt, bt), 0.0)
    a_rk = jnp.where(tri_incl, _dot_nt(rt, kt), 0.0)
    t_inv = _unit_tri_inverse(n_b, masks, eye_l)
    eg = -_dot(t_inv, jnp.concatenate([_dot(n_k, v), kap], axis=1))
    ae = _dot(a_rb, eg)
    qp = rt + ae[:, hd:]
    y0 = ae[:, :hd] + _dot(a_rk, v)
    bg = _dot_tn(bt * wl, eg)
    pt = jnp.where(eye, wl, 0.0) + bg[:, hd:]
    ut = bg[:, :hd] + _dot_tn(kt * wl, v)
    ys = _dot(jnp.concatenate([qp, pt], axis=0), st)
    L = rt.shape[0]
    return ys[:L] + y0, ys[L:] + ut


def _rwkv_scan_kernel(*refs, reverse, finish, n_heads):
    if finish:
        (r_ref, lw_ref, k_ref, kk_ref, a_ref, v_ref, yf_ref, ko_ref, gate_ref, vec_ref,
         o_ref, st_ref) = refs
    else:
        r_ref, lw_ref, k_ref, kk_ref, a_ref, v_ref, o_ref, st_ref = refs

    @pl.when(pl.program_id(1) == 0)
    def _():
        st_ref[...] = jnp.zeros_like(st_ref)

    L = r_ref.shape[1]
    hd = HEAD_DIM
    ri = lax.broadcasted_iota(jnp.int32, (L, L), 0)
    ci = lax.broadcasted_iota(jnp.int32, (L, L), 1)
    if reverse:
        tri_incl, tri_strict = ci >= ri, ci > ri
    else:
        tri_incl, tri_strict = ci <= ri, ci < ri
    eye = lax.broadcasted_iota(jnp.int32, (hd, hd), 0) == lax.broadcasted_iota(jnp.int32, (hd, hd), 1)
    masks = _level_masks(L, reverse)
    eye_l = jnp.where(ri == ci, 1.0, 0.0)

    lw = lw_ref[0]
    cum = _dot3(jnp.where(tri_incl, 1.0, 0.0).astype(BF16), lw)
    w_t = jnp.exp(cum)
    w_i = jnp.exp(-cum)
    w_p = jnp.exp(cum - lw)
    end = 0 if reverse else L - 1
    w_l = w_t[end:end + 1]
    r, k, kk, a, v = r_ref[0], k_ref[0], kk_ref[0], a_ref[0], v_ref[0]
    rt_all = r * w_t
    kap_all = kk * w_p
    bt_all = kk * a * w_i
    kt_all = k * w_i
    if finish:
        rk_all = r * (k + ko_ref[0]) * vec_ref[2:3]
        gn_w, gn_b = vec_ref[0:1], vec_ref[1:2]
        y_fwd = yf_ref[0]
    st_all = st_ref[...]

    outs, states = [], []
    for h in range(n_heads):
        sl = slice(h * hd, (h + 1) * hd)
        kk_h = kk[:, sl]
        inv_n = 1.0 / jnp.maximum(jnp.sqrt(jnp.sum(kk_h * kk_h, axis=-1, keepdims=True)), 1e-12)
        y, st_new = _rwkv_chunk_head(rt_all[:, sl], kap_all[:, sl] * inv_n, bt_all[:, sl] * inv_n,
                                     kt_all[:, sl], v[:, sl], w_l[:, sl], st_all[:, sl],
                                     tri_strict, tri_incl, eye, masks, eye_l)
        states.append(st_new)
        if finish:
            ys = y + y_fwd[:, sl]
            mean = jnp.mean(ys, axis=-1, keepdims=True)
            cen = ys - mean
            var = jnp.mean(cen * cen, axis=-1, keepdims=True)
            yn = cen * lax.rsqrt(var + RW_GN_EPS)
            bonus = jnp.sum(rk_all[:, sl], axis=-1, keepdims=True) * v[:, sl]
            y = yn * gn_w[:, sl] + gn_b[:, sl] + bonus
        outs.append(y)
    st_ref[...] = jnp.concatenate(states, axis=1)
    out = jnp.concatenate(outs, axis=1)
    if finish:
        out = out * gate_ref[0]
    o_ref[0] = out


def rwkv_scan(pr, d, n_ctx, reverse, finish=None):
    r = pr['r']
    B, Tt, D = r.shape
    L = CHUNK
    nc, ncc = Tt // L, n_ctx // L
    tok = pl.BlockSpec((1, L, D), lambda b, c: (b, _scan_order(c, ncc, nc, reverse), 0))
    ins = [r, pr['lw%d' % d], pr['k%d' % d], pr['kk'], pr['a%d' % d], pr['v']]
    specs = [tok] * 6
    if finish is not None:
        y_fwd, gn, r_k = finish
        vec = jnp.zeros((SUBLANES, D), F32).at[0].set(gn[0]).at[1].set(gn[1]).at[2].set(r_k.reshape(D))
        ins += [y_fwd, pr['k%d' % (1 - d)], pr['gate'], vec]
        specs += [tok, tok, tok, pl.BlockSpec((SUBLANES, D), lambda b, c: (0, 0))]
    kern = functools.partial(_rwkv_scan_kernel, reverse=reverse, finish=finish is not None,
                             n_heads=D // HEAD_DIM)
    return pl.pallas_call(
        kern,
        grid=(B, nc),
        in_specs=specs,
        out_specs=tok,
        out_shape=jax.ShapeDtypeStruct((B, Tt, D), F32),
        scratch_shapes=[pltpu.VMEM((HEAD_DIM, D), F32)],
        compiler_params=_cparams(2),
        name="rwkv_scan_rev" if reverse else "rwkv_scan_fwd",
    )(*ins)


def rwkv_layer(x, mod, gate, p, n_ctx, tb):
    pr = rwkv_proj(x, mod, p, n_ctx, tb)
    y_fwd = rwkv_scan(pr, 0, n_ctx, reverse=False)
    z = rwkv_scan(pr, 1, n_ctx, reverse=True, finish=(y_fwd, p['gn'], p['r_k']))
    return out_matmul([z], p['w_out'], x, gate, n_ctx, tb)


def _rglru_scan_kernel(*refs, reverse, finish, n_ctx_blocks):
    if finish:
        (x_ref, xp_ref, xn_ref, cw_ref, vec_ref, wg_ref, hf_ref, gs_ref, o_ref,
         a_scr, b_scr, carry_ref) = refs
    else:
        x_ref, xp_ref, xn_ref, cw_ref, vec_ref, wg_ref, o_ref, a_scr, b_scr, carry_ref = refs
    c = pl.program_id(1)
    nb = pl.num_programs(1)
    t = _scan_order(c, n_ctx_blocks, nb, reverse)

    @pl.when(c == 0)
    def _():
        carry_ref[...] = jnp.zeros_like(carry_ref)

    x = x_ref[0]
    tb, W = x.shape
    first = jnp.logical_or(t == 0, t == n_ctx_blocks)
    last = jnp.logical_or(t == n_ctx_blocks - 1, t == nb - 1)
    xp = jnp.where(first, 0.0, xp_ref[0])
    xn = jnp.where(last, 0.0, xn_ref[0])
    row = lambda a, i: jnp.broadcast_to(a[i:i + 1], x.shape)
    rows = _rows(x.shape)
    x_m1 = _shift_down(x, 1, row(xp, SUBLANES - 1))
    x_m2 = _shift_down(x, 2, jnp.where(rows == 0, row(xp, SUBLANES - 2), row(xp, SUBLANES - 1)))
    x_p1 = _shift_up(x, 1, row(xn, 0))
    cw = lambda i: cw_ref[i:i + 1, :]
    xc = cw(4) + x_m2 * cw(0) + x_m1 * cw(1) + x * cw(2) + x_p1 * cw(3)
    gates = _dot(xc, wg_ref[...])
    r = _sigmoid(gates[:, :W] + vec_ref[0:1])
    i = _sigmoid(gates[:, W:] + vec_ref[1:2])
    log_a = -LRU_C * r * _softplus(-vec_ref[2:3])
    a_scr[...] = jnp.exp(log_a)
    b_scr[...] = jnp.sqrt(1.0 - jnp.exp(2.0 * log_a)) * (i * xc)

    S = SUBLANES
    ng = tb // S
    sub = _rows((S, W))

    def group(gi, carry):
        g = (ng - 1 - gi) if reverse else gi
        a = a_scr[pl.ds(pl.multiple_of(g * S, S), S), :]
        b = b_scr[pl.ds(pl.multiple_of(g * S, S), S), :]
        for s in (1, 2, 4):
            if reverse:
                ok = sub < S - s
                a_s = jnp.where(ok, pltpu.roll(a, S - s, 0), 1.0)
                b_s = jnp.where(ok, pltpu.roll(b, S - s, 0), 0.0)
            else:
                ok = sub >= s
                a_s = jnp.where(ok, pltpu.roll(a, s, 0), 1.0)
                b_s = jnp.where(ok, pltpu.roll(b, s, 0), 0.0)
            b = a * b_s + b
            a = a * a_s
        h = b + a * carry
        b_scr[pl.ds(pl.multiple_of(g * S, S), S), :] = h
        e = 0 if reverse else S - 1
        return jnp.broadcast_to(h[e:e + 1], (S, W))

    carry_ref[...] = lax.fori_loop(0, ng, group, carry_ref[...])
    h_all = b_scr[...]
    if finish:
        h_all = (h_all + hf_ref[0]) * gs_ref[0]
    o_ref[0] = h_all


def rglru_scan(xr, p, d, n_ctx, tb, reverse, finish=None):
    B, Tt, W = xr.shape
    nblk = p['gate_w'].shape[2]
    bd = W // nblk
    ncb = n_ctx // tb
    nb = Tt // tb
    nb8 = Tt // SUBLANES
    r8 = tb // SUBLANES

    def dense(wb):
        eye = jnp.eye(nblk, dtype=wb.dtype)
        return jnp.einsum('ncd,nm->ncmd', wb, eye).reshape(W, W)

    wg = _bf(jnp.concatenate([dense(p['gate_w'][d, 0]), dense(p['gate_w'][d, 1])], axis=1))
    cw = jnp.zeros((SUBLANES, W), F32).at[0:4].set(p['conv_w']).at[4].set(p['conv_b'])
    vec = jnp.zeros((SUBLANES, W), F32).at[0:2].set(p['gate_b'][d]).at[2].set(p['lam'][d])
    order = lambda c: _scan_order(c, ncb, nb, reverse)
    tok = pl.BlockSpec((1, tb, W), lambda b, c: (b, order(c), 0))
    full = lambda shape: pl.BlockSpec(shape, lambda b, c: (0,) * len(shape))
    ins = [xr, xr, xr, cw, vec, wg]
    specs = [tok,
             pl.BlockSpec((1, SUBLANES, W), lambda b, c: (b, jnp.maximum(order(c) * r8 - 1, 0), 0)),
             pl.BlockSpec((1, SUBLANES, W), lambda b, c: (b, jnp.minimum((order(c) + 1) * r8, nb8 - 1), 0)),
             full((SUBLANES, W)), full((SUBLANES, W)), full((W, 2 * W))]
    if finish is not None:
        ins += list(finish)
        specs += [tok, tok]
    kern = functools.partial(_rglru_scan_kernel, reverse=reverse, finish=finish is not None,
                             n_ctx_blocks=ncb)
    return pl.pallas_call(
        kern,
        grid=(B, nb),
        in_specs=specs,
        out_specs=tok,
        out_shape=jax.ShapeDtypeStruct((B, Tt, W), F32),
        scratch_shapes=[pltpu.VMEM((tb, W), F32), pltpu.VMEM((tb, W), F32), pltpu.VMEM((SUBLANES, W), F32)],
        compiler_params=_cparams(2),
        name="rglru_scan_rev" if reverse else "rglru_scan_fwd",
    )(*ins)


def rglru_layer(x, mod, gate, p, n_ctx, tb):
    W = p['conv_w'].shape[1]
    xr, gs = in_matmul(x, mod, p['norm_g'], p['w_in'], (W, W), (None, "silu"), n_ctx, tb)
    h_fwd = rglru_scan(xr, p, 0, n_ctx, tb, reverse=False)
    z = rglru_scan(xr, p, 1, n_ctx, tb, reverse=True, finish=(h_fwd, gs))
    return out_matmul([z], p['w_out'], x, gate, n_ctx, tb)


def _head_mean_sq(x, bd_ref):
    sq = x * x
    hi = _bf(sq)
    lo = _bf(sq - hi.astype(F32))
    d = lambda t: jnp.dot(t, bd_ref[...], preferred_element_type=F32)
    return (d(hi) + d(lo)) * (1.0 / HEAD_DIM)


def _qk_prep_kernel(q_ref, k_ref, v_ref, cs_ref, g_ref, bd_ref, qn_ref, kn_ref, qr_ref, kr_ref, vb_ref):
    reps = q_ref.shape[2] // cs_ref.shape[2]
    cos = jnp.concatenate([cs_ref[0]] * reps, axis=1)
    sin = jnp.concatenate([cs_ref[1]] * reps, axis=1)
    lane = lax.broadcasted_iota(jnp.int32, cos.shape, 1)
    quarter = HEAD_DIM // 4
    low = (lane % (2 * quarter)) < quarter
    D = cos.shape[1]

    def prep(x, g):
        xn = x * lax.rsqrt(_head_mean_sq(x, bd_ref) + RMS_EPS) * g
        partner = jnp.where(low, pltpu.roll(xn, D - quarter, 1), pltpu.roll(xn, quarter, 1))
        return xn, xn * cos + partner * sin

    qn, qr = prep(q_ref[0], g_ref[0:1])
    kn, kr = prep(k_ref[0], g_ref[1:2])
    qn_ref[0] = _bf(qn)
    kn_ref[0] = _bf(kn)
    qr_ref[0] = _bf(qr)
    kr_ref[0] = _bf(kr)
    vb_ref[0] = _bf(v_ref[0])


def qk_prep(q, k, v, qk_g, n_ctx, tb):
    B, Tt, D = q.shape
    H = D // HEAD_DIM
    T = Tt - n_ctx
    nfreq = HEAD_DIM // 4
    pos = jnp.arange(T)
    inv = ROPE_THETA ** (-jnp.arange(nfreq, dtype=F32) / nfreq)
    ang_r = (pos // GRID_W).astype(F32)[:, None] * inv
    ang_c = (pos % GRID_W).astype(F32)[:, None] * inv
    cos = jnp.concatenate([jnp.cos(ang_r)] * 2 + [jnp.cos(ang_c)] * 2, axis=1)
    sin = jnp.concatenate([-jnp.sin(ang_r), jnp.sin(ang_r), -jnp.sin(ang_c), jnp.sin(ang_c)], axis=1)
    cs = jnp.stack([jnp.concatenate([jnp.ones((n_ctx, HEAD_DIM), F32), cos], 0),
                    jnp.concatenate([jnp.zeros((n_ctx, HEAD_DIM), F32), sin], 0)])
    cs = jnp.concatenate([cs, cs], axis=2)
    g = jnp.zeros((SUBLANES, D), F32).at[0].set(jnp.tile(qk_g[0], H)).at[1].set(jnp.tile(qk_g[1], H))
    hid = jnp.arange(D) // HEAD_DIM
    bd = (hid[:, None] == hid[None, :]).astype(BF16)
    tok = pl.BlockSpec((1, tb, D), lambda b, t: (b, t, 0))
    return pl.pallas_call(
        _qk_prep_kernel,
        grid=(B, Tt // tb),
        in_specs=[tok, tok, tok,
                  pl.BlockSpec((2, tb, 2 * HEAD_DIM), lambda b, t: (0, t, 0)),
                  pl.BlockSpec((SUBLANES, D), lambda b, t: (0, 0)),
                  pl.BlockSpec((D, D), lambda b, t: (0, 0))],
        out_specs=[tok] * 5,
        out_shape=[jax.ShapeDtypeStruct((B, Tt, D), BF16)] * 5,
        compiler_params=_cparams(2),
        name="qk_prep",
    )(q, k, v, cs, g, bd)


def _softmax_pv(s_list, v_list):
    m = s_list[0].max(axis=-1, keepdims=True)
    for s in s_list[1:]:
        m = jnp.maximum(m, s.max(axis=-1, keepdims=True))
    ps = [jnp.exp(s - m) for s in s_list]
    den = ps[0].sum(axis=-1, keepdims=True)
    for p_ in ps[1:]:
        den = den + p_.sum(axis=-1, keepdims=True)
    o = jnp.dot(_bf(ps[0]), v_list[0], preferred_element_type=F32)
    for p_, v_ in zip(ps[1:], v_list[1:]):
        o = o + jnp.dot(_bf(p_), v_, preferred_element_type=F32)
    return o / den


def _natten_kernel(qr_ref, qn_ref, kr_ref, v_ref, kc_ref, vc_ref, bias_ref, o_ref, *, rows, kh):
    gw = GRID_W
    scale = HEAD_DIM ** -0.5
    lane = lax.broadcasted_iota(jnp.int32, (gw, 2 * HEAD_DIM), 1)
    head_lanes = [lane < HEAD_DIM, lane >= HEAD_DIM]
    kc = kc_ref[0]
    vc = vc_ref[0]

    def row_step(r, carry):
        start = jnp.clip(r - kh // 2, 0, rows - kh)
        d0 = start - r + kh - 1
        q0 = pl.multiple_of(r * gw, gw)
        k0 = pl.multiple_of(start * gw, gw)
        qr = qr_ref[0, pl.ds(q0, gw), :]
        qn = qn_ref[0, pl.ds(q0, gw), :]
        kb = kr_ref[0, pl.ds(k0, kh * gw), :]
        vb = v_ref[0, pl.ds(k0, kh * gw), :]
        o = jnp.zeros((gw, 2 * HEAD_DIM), F32)
        for h in range(2):
            zero = jnp.zeros_like(qr)
            qr_h = jnp.where(head_lanes[h], qr, zero)
            qn_h = jnp.where(head_lanes[h], qn, zero)
            s_band = lax.dot_general(qr_h, kb, (((1,), (1,)), ((), ())), preferred_element_type=F32)
            s_band = s_band * scale + bias_ref[h, d0]
            s_ctx = lax.dot_general(qn_h, kc, (((1,), (1,)), ((), ())), preferred_element_type=F32) * scale
            o_h = _softmax_pv([s_band, s_ctx], [vb, vc])
            o = jnp.where(head_lanes[h], o_h, o)
        o_ref[0, pl.ds(q0, gw), :] = o
        return carry

    lax.fori_loop(0, rows, row_step, 0)


def _natten_bias(rpb, rows, kh):
    cols = np.arange(GRID_W)
    c_start = np.clip(cols - WIN_W // 2, 0, GRID_W - WIN_W)
    col_ok = (cols[None, :] >= c_start[:, None]) & (cols[None, :] < c_start[:, None] + WIN_W)
    dc_idx = np.clip(cols[None, :] - cols[:, None] + WIN_W - 1, 0, 2 * WIN_W - 2)
    v = np.arange(kh)[:, None]
    dr_idx = v - (kh - 1) + np.arange(kh)[None, :] + WIN_H - 1
    bias = rpb[:, dr_idx[:, None, :, None], dc_idx[None, :, None, :]]
    bias = jnp.where(jnp.asarray(col_ok)[None, None, :, None, :], bias.astype(F32), NEG_BIG)
    H = rpb.shape[0]
    return bias.reshape(H, kh, GRID_W, kh * GRID_W)


def natten_attention(qr, qn, kr, vb, rpb, n_ctx):
    B, Tt, D = qr.shape
    T = Tt - n_ctx
    rows = T // GRID_W
    kh = min(WIN_H, rows)
    HP = D // (2 * HEAD_DIM)
    bias = _natten_bias(rpb, rows, kh)
    lat = pl.BlockSpec((1, T, 2 * HEAD_DIM), lambda hp, b: (b, 0, hp))
    ctx = pl.BlockSpec((1, n_ctx, 2 * HEAD_DIM), lambda hp, b: (b, 0, hp))
    kern = functools.partial(_natten_kernel, rows=rows, kh=kh)
    q_l, qn_l, k_l, v_l = qr[:, n_ctx:], qn[:, n_ctx:], kr[:, n_ctx:], vb[:, n_ctx:]
    return pl.pallas_call(
        kern,
        grid=(HP, B),
        in_specs=[lat, lat, lat, lat, ctx, ctx,
                  pl.BlockSpec((2, kh, GRID_W, kh * GRID_W), lambda hp, b: (hp, 0, 0, 0))],
        out_specs=lat,
        out_shape=jax.ShapeDtypeStruct((B, T, D), F32),
        compiler_params=_cparams(2),
        name="natten",
    )(q_l, qn_l, k_l, v_l, kr[:, :n_ctx], vb[:, :n_ctx], bias)


def _ctx_attn_kernel(q_ref, k_ref, v_ref, o_ref):
    scale = HEAD_DIM ** -0.5
    q, k, v = q_ref[0], k_ref[0], v_ref[0]
    lane = lax.broadcasted_iota(jnp.int32, q.shape, 1)
    o = jnp.zeros(q.shape, F32)
    for h in range(2):
        mine = (lane >= HEAD_DIM) if h else (lane < HEAD_DIM)
        q_h = jnp.where(mine, q, jnp.zeros_like(q))
        s = lax.dot_general(q_h, k, (((1,), (1,)), ((), ())), preferred_element_type=F32) * scale
        o = jnp.where(mine, _softmax_pv([s], [v]), o)
    o_ref[0] = o


def ctx_attention(qn, kn, vb, n_ctx):
    B, Tt, D = qn.shape
    HP = D // (2 * HEAD_DIM)
    blk = pl.BlockSpec((1, n_ctx, 2 * HEAD_DIM), lambda b, hp: (b, 0, hp))
    return pl.pallas_call(
        _ctx_attn_kernel,
        grid=(B, HP),
        in_specs=[blk, blk, blk],
        out_specs=blk,
        out_shape=jax.ShapeDtypeStruct((B, n_ctx, D), F32),
        compiler_params=_cparams(2),
        name="ctx_attn",
    )(qn, kn, vb)


def natten_layer(x, mod, gate, p, n_ctx, tb):
    D = x.shape[2]
    q, k, v, gs = in_matmul(x, mod, p['norm_g'], p['w_in'], (D,) * 4, (None, None, None, "silu"),
                            n_ctx, tb)
    qn, kn, qr, kr, vb = qk_prep(q, k, v, p['qk_g'], n_ctx, tb)
    o_l = natten_attention(qr, qn, kr, vb, p['rpb'], n_ctx)
    o_c = ctx_attention(qn, kn, vb, n_ctx)
    o = jnp.concatenate([o_c, o_l], axis=1)
    return out_matmul([o, gs], p['w_out'], x, gate, n_ctx, tb)


_LAYER_KEYS = (
    ('norm_g', 'ada_w', 'ada_b', 'w_in', 'mu', 'lora_b0', 'lora_down', 'lora_up', 'k_ka', 'r_k', 'gn', 'w_out'),
    ('norm_g', 'ada_w', 'ada_b', 'w_in', 'conv_w', 'conv_b', 'gate_w', 'gate_b', 'lam', 'w_out'),
    ('norm_g', 'ada_w', 'ada_b', 'w_in', 'qk_g', 'rpb', 'w_out'),
)
_LAYERS = (rwkv_layer, rglru_layer, natten_layer)


def _forward(x, c, ctx, c_ctx, layer_params, tb):
    B, T, D = x.shape
    n_ctx = ctx.shape[1]
    xs = jnp.concatenate([ctx, x], axis=1)
    m_rows = -(-(B + 1) // SUBLANES) * SUBLANES
    c_all = jnp.zeros((m_rows, D), F32).at[:B].set(c).at[B].set(c_ctx)
    for i, p in enumerate(layer_params):
        m = ada_mod(c_all, p['ada_w'], p['ada_b'])
        m_l = m[:B].reshape(B, 3, D)
        m_c = jnp.broadcast_to(m[B].reshape(1, 3, D), (B, 3, D))
        both = jnp.stack([m_c, m_l], axis=1)
        mod = jnp.stack([1.0 + both[:, :, 1], both[:, :, 0]], axis=2)
        gate = both[:, :, 2:3]
        xs = _LAYERS[i % 3](xs, mod, gate, p, n_ctx, tb)
    return xs[:, n_ctx:]


def kernel(x, c, ctx, c_ctx, l0_norm_g, l0_ada_w, l0_ada_b, l0_w_in, l0_mu, l0_lora_b0, l0_lora_down, l0_lora_up, l0_k_ka, l0_r_k, l0_gn, l0_w_out, l1_norm_g, l1_ada_w, l1_ada_b, l1_w_in, l1_conv_w, l1_conv_b, l1_gate_w, l1_gate_b, l1_lam, l1_w_out, l2_norm_g, l2_ada_w, l2_ada_b, l2_w_in, l2_qk_g, l2_rpb, l2_w_out, l3_norm_g, l3_ada_w, l3_ada_b, l3_w_in, l3_mu, l3_lora_b0, l3_lora_down, l3_lora_up, l3_k_ka, l3_r_k, l3_gn, l3_w_out):
    args = (l0_norm_g, l0_ada_w, l0_ada_b, l0_w_in, l0_mu, l0_lora_b0, l0_lora_down, l0_lora_up, l0_k_ka, l0_r_k, l0_gn, l0_w_out, l1_norm_g, l1_ada_w, l1_ada_b, l1_w_in, l1_conv_w, l1_conv_b, l1_gate_w, l1_gate_b, l1_lam, l1_w_out, l2_norm_g, l2_ada_w, l2_ada_b, l2_w_in, l2_qk_g, l2_rpb, l2_w_out, l3_norm_g, l3_ada_w, l3_ada_b, l3_w_in, l3_mu, l3_lora_b0, l3_lora_down, l3_lora_up, l3_k_ka, l3_r_k, l3_gn, l3_w_out)
    layer_params, pos = [], 0
    for i in range(4):
        keys = _LAYER_KEYS[i % 3]
        layer_params.append(dict(zip(keys, args[pos:pos + len(keys)])))
        pos += len(keys)
    return _forward(x, c, ctx, c_ctx, layer_params, tb=128)
```

```python
import functools

import jax
import jax.numpy as jnp
import numpy as np
from jax import lax
from jax.experimental import pallas as pl
from jax.experimental.pallas import tpu as pltpu

F32 = jnp.float32
BF16 = jnp.bfloat16

HEAD_DIM = 64
CHUNK = 64
GRID_W = 64
WIN_H = 8
WIN_W = 16
ROPE_THETA = 10000.0
RMS_EPS = 1e-6
RW_GN_EPS = 64e-5
LRU_C = 8.0
SUBLANES = 8
NEG_BIG = -1e30
VMEM_LIMIT = 56 * 1024 * 1024


def _cparams(n_axes):
    return pltpu.CompilerParams(
        dimension_semantics=("arbitrary",) * n_axes, vmem_limit_bytes=VMEM_LIMIT)


def _bf(x):
    return x.astype(BF16)


def _dot(a, b):
    return jnp.dot(_bf(a), _bf(b), preferred_element_type=F32)


def _dot_nt(a, b):
    return lax.dot_general(_bf(a), _bf(b), (((1,), (1,)), ((), ())), preferred_element_type=F32)


def _dot_tn(a, b):
    return lax.dot_general(_bf(a), _bf(b), (((0,), (0,)), ((), ())), preferred_element_type=F32)


def _dot3(a_exact_bf16, b):
    hi = _bf(b)
    r1 = b - hi.astype(F32)
    mid = _bf(r1)
    lo = _bf(r1 - mid.astype(F32))
    d = lambda t: jnp.dot(a_exact_bf16, t, preferred_element_type=F32)
    return d(hi) + d(mid) + d(lo)


def _sigmoid(x):
    return 1.0 / (1.0 + jnp.exp(-x))


def _silu(x):
    return x * _sigmoid(x)


def _softplus(x):
    return jnp.maximum(x, 0.0) + jnp.log(1.0 + jnp.exp(-jnp.abs(x)))


def _rows(shape):
    return lax.broadcasted_iota(jnp.int32, shape, 0)


def _shift_down(x, s, fill):
    return jnp.where(_rows(x.shape) >= s, pltpu.roll(x, s, 0), fill)


def _shift_up(x, s, fill):
    n = x.shape[0]
    return jnp.where(_rows(x.shape) < n - s, pltpu.roll(x, n - s, 0), fill)


def _norm_mod(xb, g, scale1, shift):
    xf = xb.astype(F32)
    ms = jnp.mean(xf * xf, axis=-1, keepdims=True)
    return xf * lax.rsqrt(ms + RMS_EPS) * g * scale1 + shift


def _ada_kernel(c_ref, w_ref, b_ref, o_ref):
    o_ref[...] = _dot(_silu(c_ref[...]), w_ref[...]) + b_ref[...]


def ada_mod(c_all, ada_w, ada_b):
    m, d = c_all.shape
    n = ada_w.shape[1]
    tn = d
    return pl.pallas_call(
        _ada_kernel,
        grid=(n // tn,),
        in_specs=[pl.BlockSpec((m, d), lambda j: (0, 0)),
                  pl.BlockSpec((d, tn), lambda j: (0, j)),
                  pl.BlockSpec((1, tn), lambda j: (0, j))],
        out_specs=pl.BlockSpec((m, tn), lambda j: (0, j)),
        out_shape=jax.ShapeDtypeStruct((m, n), F32),
        compiler_params=_cparams(1),
        name="ada_mod",
    )(c_all, _bf(ada_w), ada_b.reshape(1, n))


def _seg_of_block(t, n_ctx_blocks):
    return (t >= n_ctx_blocks).astype(jnp.int32)


def _scan_order(c, n_ctx_blocks, n_blocks, reverse):
    if not reverse:
        return c
    return jnp.where(c < n_ctx_blocks, n_ctx_blocks - 1 - c, n_blocks - 1 + n_ctx_blocks - c)


def _in_mm_kernel(x_ref, mod_ref, g_ref, w_ref, *o_refs, splits, acts):
    h = _norm_mod(x_ref[0], g_ref[...], mod_ref[0, 0, 0:1], mod_ref[0, 0, 1:2])
    hb = _bf(h)
    off = 0
    for o_ref, n, act in zip(o_refs, splits, acts):
        z = jnp.dot(hb, w_ref[:, off:off + n], preferred_element_type=F32)
        if act == "silu":
            z = _silu(z)
        o_ref[0] = z.astype(o_ref.dtype)
        off += n


def in_matmul(x, mod, norm_g, w, splits, acts, n_ctx, tb):
    B, Tt, D = x.shape
    ncb = n_ctx // tb
    n = w.shape[1]
    assert sum(splits) == n
    kern = functools.partial(_in_mm_kernel, splits=tuple(splits), acts=tuple(acts))
    return pl.pallas_call(
        kern,
        grid=(B, Tt // tb),
        in_specs=[pl.BlockSpec((1, tb, D), lambda b, t: (b, t, 0)),
                  pl.BlockSpec((1, 1, 2, D), lambda b, t: (b, _seg_of_block(t, ncb), 0, 0)),
                  pl.BlockSpec((1, D), lambda b, t: (0, 0)),
                  pl.BlockSpec((D, n), lambda b, t: (0, 0))],
        out_specs=[pl.BlockSpec((1, tb, s), lambda b, t: (b, t, 0)) for s in splits],
        out_shape=[jax.ShapeDtypeStruct((B, Tt, s), F32) for s in splits],
        compiler_params=_cparams(2),
        name="in_matmul",
    )(x, mod, norm_g.reshape(1, D), _bf(w))


def _out_mm_kernel(*refs, n_a):
    a_refs = refs[:n_a]
    w_ref, x_ref, gate_ref, o_ref = refs[n_a:]
    a = a_refs[0][0]
    for r in a_refs[1:]:
        a = a * r[0]
    o_ref[0] = x_ref[0] + gate_ref[0, 0] * _dot(a, w_ref[...])


def out_matmul(a_list, w, x, gate, n_ctx, tb):
    B, Tt, D = x.shape
    K = w.shape[0]
    ncb = n_ctx // tb
    kern = functools.partial(_out_mm_kernel, n_a=len(a_list))
    return pl.pallas_call(
        kern,
        grid=(B, Tt // tb),
        in_specs=[pl.BlockSpec((1, tb, K), lambda b, t: (b, t, 0)) for _ in a_list] + [
            pl.BlockSpec((K, D), lambda b, t: (0, 0)),
            pl.BlockSpec((1, tb, D), lambda b, t: (b, t, 0)),
            pl.BlockSpec((1, 1, 1, D), lambda b, t: (b, _seg_of_block(t, ncb), 0, 0))],
        out_specs=pl.BlockSpec((1, tb, D), lambda b, t: (b, t, 0)),
        out_shape=jax.ShapeDtypeStruct((B, Tt, D), F32),
        compiler_params=_cparams(2),
        name="out_matmul",
    )(*a_list, _bf(w), x, gate)


_V_NORM_G, _V_MU, _V_KK, _V_KA, _V_B0, _V_RK = 0, 1, 7, 8, 9, 13
_V_ROWS = 16
LANES = 128


def _head_sum(x, sel_ref, selt_ref):
    mm = lambda a, w: jnp.dot(a, w, preferred_element_type=F32)
    hi = _bf(x)
    lo = _bf(x - hi.astype(F32))
    s = mm(hi, sel_ref[...]) + mm(lo, sel_ref[...])
    s_hi = _bf(s)
    s_lo = _bf(s - s_hi.astype(F32))
    return mm(s_hi, selt_ref[...]) + mm(s_lo, selt_ref[...])


def _head_selectors(D):
    hid = jnp.arange(D) // HEAD_DIM
    sel = (hid[:, None] == jnp.arange(LANES)[None, :]).astype(BF16)
    return sel, sel.T


def _rwkv_proj_kernel(x_ref, xp_ref, xn_ref, mod_ref, vec_ref, win_ref, dw_ref, da_ref, upw_ref,
                      upa_ref, sel_ref, selt_ref, r_ref, v_ref, g_ref, kk_ref, bonus_ref, lw0_ref,
                      a0_ref, k0_ref, lw1_ref, a1_ref, k1_ref, *, n_ctx_blocks):
    t = pl.program_id(1)
    nb = pl.num_programs(1)
    vec = lambda i: vec_ref[i:i + 1, :]
    scale1, shift = mod_ref[0, 0, 0:1], mod_ref[0, 0, 1:2]
    g = vec(_V_NORM_G)
    h = _norm_mod(x_ref[0], g, scale1, shift)
    tb = h.shape[0]
    first = jnp.logical_or(t == 0, t == n_ctx_blocks)
    last = jnp.logical_or(t == n_ctx_blocks - 1, t == nb - 1)
    hp = _norm_mod(xp_ref[0], g, scale1, shift)[SUBLANES - 1:SUBLANES]
    hn = _norm_mod(xn_ref[0], g, scale1, shift)[0:1]
    hp = jnp.where(first, 0.0, hp)
    hn = jnp.where(last, 0.0, hn)
    h_prev = _shift_down(h, 1, jnp.broadcast_to(hp, h.shape))
    h_next = _shift_up(h, 1, jnp.broadcast_to(hn, h.shape))
    xx = 0.5 * (h_prev + h_next) - h
    mix = lambda j: _bf(h + xx * vec(_V_MU + j))
    mm = lambda a, w: jnp.dot(a, w, preferred_element_type=F32)
    xr, xw, xk, xv, xa, xg = [mix(j) for j in range(6)]
    r = mm(xr, win_ref[0])
    k = mm(xk, win_ref[1])
    v = mm(xv, win_ref[2])
    r_ref[0] = r
    v_ref[0] = v
    g_ref[0] = _silu(mm(xg, win_ref[3]))
    kk_raw = k * vec(_V_KK)
    kk_norm = jnp.sqrt(_head_sum(kk_raw * kk_raw, sel_ref, selt_ref))
    kk_ref[0] = kk_raw / jnp.maximum(kk_norm, 1e-12)
    tw = _bf(jnp.tanh(mm(xw, dw_ref[...])))
    ta = _bf(mm(xa, da_ref[...]))
    k_a = vec(_V_KA)
    k_sum = None
    for d, (lw_ref, a_ref, kd_ref) in enumerate(((lw0_ref, a0_ref, k0_ref), (lw1_ref, a1_ref, k1_ref))):
        zw = vec(_V_B0 + 2 * d) + mm(tw, upw_ref[d])
        w_log = -_softplus(-zw) - 0.5
        lw_ref[0] = -jnp.exp(w_log)
        a_d = _sigmoid(vec(_V_B0 + 2 * d + 1) + mm(ta, upa_ref[d]))
        a_ref[0] = a_d
        k_d = k * (1.0 + (a_d - 1.0) * k_a)
        kd_ref[0] = k_d
        k_sum = k_d if k_sum is None else k_sum + k_d
    bonus_ref[0] = _head_sum(r * k_sum * vec(_V_RK), sel_ref, selt_ref) * v


def rwkv_proj(x, mod, p, n_ctx, tb):
    B, Tt, D = x.shape
    lora = p['lora_down'].shape[-1]
    ncb = n_ctx // tb
    nb8 = Tt // SUBLANES
    r8 = tb // SUBLANES
    vec = jnp.zeros((_V_ROWS, D), F32)
    vec = vec.at[_V_NORM_G].set(p['norm_g']).at[_V_MU:_V_MU + 6].set(p['mu'])
    vec = vec.at[_V_KK].set(p['k_ka'][0]).at[_V_KA].set(p['k_ka'][1])
    vec = vec.at[_V_B0:_V_B0 + 4].set(p['lora_b0'].reshape(4, D)).at[_V_RK].set(p['r_k'].reshape(D))
    sel, selt = _head_selectors(D)
    down, up = p['lora_down'], p['lora_up']
    dw = _bf(jnp.concatenate([down[0, 0], down[1, 0]], axis=1))
    da = _bf(jnp.concatenate([down[0, 1], down[1, 1]], axis=1))
    z = jnp.zeros((lora, D), F32)
    upw = _bf(jnp.stack([jnp.concatenate([up[0, 0], z], 0), jnp.concatenate([z, up[1, 0]], 0)]))
    upa = _bf(jnp.stack([jnp.concatenate([up[0, 1], z], 0), jnp.concatenate([z, up[1, 1]], 0)]))
    full = lambda shape: pl.BlockSpec(shape, lambda b, t: (0,) * len(shape))
    tok = pl.BlockSpec((1, tb, D), lambda b, t: (b, t, 0))
    kern = functools.partial(_rwkv_proj_kernel, n_ctx_blocks=ncb)
    outs = pl.pallas_call(
        kern,
        grid=(B, Tt // tb),
        in_specs=[tok,
                  pl.BlockSpec((1, SUBLANES, D), lambda b, t: (b, jnp.maximum(t * r8 - 1, 0), 0)),
                  pl.BlockSpec((1, SUBLANES, D), lambda b, t: (b, jnp.minimum((t + 1) * r8, nb8 - 1), 0)),
                  pl.BlockSpec((1, 1, 2, D), lambda b, t: (b, _seg_of_block(t, ncb), 0, 0)),
                  full((_V_ROWS, D)), full((4, D, D)), full((D, 2 * lora)), full((D, 2 * lora)),
                  full((2, 2 * lora, D)), full((2, 2 * lora, D)), full((D, LANES)), full((LANES, D))],
        out_specs=[tok] * 11,
        out_shape=[jax.ShapeDtypeStruct((B, Tt, D), F32)] * 11,
        compiler_params=_cparams(2),
        name="rwkv_proj",
    )(x, x, x, mod, vec, _bf(p['w_in']), dw, da, upw, upa, sel, selt)
    names = ('r', 'v', 'gate', 'kk', 'bonus', 'lw0', 'a0', 'k0', 'lw1', 'a1', 'k1')
    return dict(zip(names, outs))


def _level_masks(L, reverse):
    ri = lax.broadcasted_iota(jnp.int32, (L, L), 0)
    ci = lax.broadcasted_iota(jnp.int32, (L, L), 1)
    if reverse:
        ri, ci = ci, ri
    masks = []
    for j in range(int(np.log2(L))):
        same = (ri >> (j + 1)) == (ci >> (j + 1))
        masks.append(same & (((ri >> j) & 1) == 1) & (((ci >> j) & 1) == 0))
    return masks


def _rwkv_chunk_heads(rt, kap, bt, kt, v, wl, st, tri_strict, tri_incl, eye, masks, eye_l):
    hs = range(len(rt))
    L, hd = v[0].shape
    mm = lambda a, b: jnp.dot(a, b, preferred_element_type=F32)
    mm_nt = lambda a, b: lax.dot_general(a, b, (((1,), (1,)), ((), ())), preferred_element_type=F32)
    mm_tn = lambda a, b: lax.dot_general(a, b, (((0,), (0,)), ((), ())), preferred_element_type=F32)
    v_b = [_bf(v[h]) for h in hs]
    lhs = [jnp.concatenate([_bf(kap[h]), _bf(rt[h])], axis=0) for h in hs]
    sb = [mm_nt(lhs[h], _bf(bt[h])) for h in hs]
    sk = [mm_nt(lhs[h], _bf(kt[h])) for h in hs]
    n_b = [jnp.where(tri_strict, sb[h][:L], 0.0) for h in hs]
    a_rb = [_bf(jnp.where(tri_incl, sb[h][L:], 0.0)) for h in hs]
    n_k = [_bf(jnp.where(tri_strict, sk[h][:L], 0.0)) for h in hs]
    a_rk = [_bf(jnp.where(tri_incl, sk[h][L:], 0.0)) for h in hs]
    t = [eye_l - jnp.where(masks[0], n_b[h], 0.0) for h in hs]
    for m in masks[1:]:
        t_b = [_bf(t[h]) for h in hs]
        ct = [_bf(mm(_bf(jnp.where(m, n_b[h], 0.0)), t_b[h])) for h in hs]
        t = [t[h] - mm(t_b[h], ct[h]) for h in hs]
    nkv = [mm(n_k[h], v_b[h]) for h in hs]
    eg = [_bf(-mm(_bf(t[h]), _bf(jnp.concatenate([nkv[h], kap[h]], axis=1)))) for h in hs]
    ae = [mm(a_rb[h], eg[h]) for h in hs]
    y0 = [ae[h][:, :hd] + mm(a_rk[h], v_b[h]) for h in hs]
    qp = [rt[h] + ae[h][:, hd:] for h in hs]
    bg = [mm_tn(_bf(bt[h] * wl[h]), eg[h]) for h in hs]
    ut = [bg[h][:, :hd] + mm_tn(_bf(kt[h] * wl[h]), v_b[h]) for h in hs]
    pt = [jnp.where(eye, wl[h], 0.0) + bg[h][:, hd:] for h in hs]
    ys = [mm(_bf(jnp.concatenate([qp[h], pt[h]], axis=0)), _bf(st[h])) for h in hs]
    return [ys[h][:L] + y0[h] for h in hs], [ys[h][L:] + ut[h] for h in hs]


def _rwkv_scan_kernel(*refs, reverse, finish, n_heads):
    if finish:
        (r_ref, lw_ref, k_ref, kk_ref, a_ref, v_ref, yf_ref, bonus_ref, gate_ref, vec_ref,
         o_ref, st_ref) = refs
    else:
        r_ref, lw_ref, k_ref, kk_ref, a_ref, v_ref, o_ref, st_ref = refs

    @pl.when(pl.program_id(1) == 0)
    def _():
        st_ref[...] = jnp.zeros_like(st_ref)

    L = r_ref.shape[1]
    hd = HEAD_DIM
    ri = lax.broadcasted_iota(jnp.int32, (L, L), 0)
    ci = lax.broadcasted_iota(jnp.int32, (L, L), 1)
    if reverse:
        tri_incl, tri_strict = ci >= ri, ci > ri
    else:
        tri_incl, tri_strict = ci <= ri, ci < ri
    eye = lax.broadcasted_iota(jnp.int32, (hd, hd), 0) == lax.broadcasted_iota(jnp.int32, (hd, hd), 1)
    masks = _level_masks(L, reverse)
    eye_l = jnp.where(ri == ci, 1.0, 0.0)

    lw = lw_ref[0]
    cum = _dot3(jnp.where(tri_incl, 1.0, 0.0).astype(BF16), lw)
    w_t = jnp.exp(cum)
    w_i = jnp.exp(-cum)
    w_p = jnp.exp(cum - lw)
    end = 0 if reverse else L - 1
    w_l = w_t[end:end + 1]
    kk, v = kk_ref[0], v_ref[0]
    rt_all = r_ref[0] * w_t
    kap_all = kk * w_p
    bt_all = kk * a_ref[0] * w_i
    kt_all = k_ref[0] * w_i
    st_all = st_ref[...]

    sls = [slice(h * hd, (h + 1) * hd) for h in range(n_heads)]
    split = lambda x: [x[:, sl] for sl in sls]
    outs, states = _rwkv_chunk_heads(split(rt_all), split(kap_all), split(bt_all), split(kt_all),
                                     split(v), split(w_l), split(st_all),
                                     tri_strict, tri_incl, eye, masks, eye_l)
    st_ref[...] = jnp.concatenate(states, axis=1)
    if finish:
        ones = jnp.ones((hd, hd), BF16)
        lane_sum = lambda x: jnp.dot(x, ones, preferred_element_type=F32)
        y_fwd = split(yf_ref[0])
        ys = jnp.concatenate([outs[h] + y_fwd[h] for h in range(n_heads)], axis=0)
        ys_hi = _bf(ys)
        mean = (lane_sum(ys_hi) + lane_sum(_bf(ys - ys_hi.astype(F32)))) * (1.0 / hd)
        cen = ys - mean
        c2 = cen * cen
        c2_hi = _bf(c2)
        var = (lane_sum(c2_hi) + lane_sum(_bf(c2 - c2_hi.astype(F32)))) * (1.0 / hd)
        yn = cen * lax.rsqrt(var + RW_GN_EPS)
        out = jnp.concatenate([yn[h * L:(h + 1) * L] for h in range(n_heads)], axis=1)
        out = (out * vec_ref[0:1] + vec_ref[1:2] + bonus_ref[0]) * gate_ref[0]
    else:
        out = jnp.concatenate(outs, axis=1)
    o_ref[0] = out


def rwkv_scan(pr, d, n_ctx, reverse, finish=None):
    r = pr['r']
    B, Tt, D = r.shape
    L = CHUNK
    nc, ncc = Tt // L, n_ctx // L
    tok = pl.BlockSpec((1, L, D), lambda b, c: (b, _scan_order(c, ncc, nc, reverse), 0))
    ins = [r, pr['lw%d' % d], pr['k%d' % d], pr['kk'], pr['a%d' % d], pr['v']]
    specs = [tok] * 6
    if finish is not None:
        y_fwd, gn = finish
        vec = jnp.zeros((SUBLANES, D), F32).at[0].set(gn[0]).at[1].set(gn[1])
        ins += [y_fwd, pr['bonus'], pr['gate'], vec]
        specs += [tok, tok, tok, pl.BlockSpec((SUBLANES, D), lambda b, c: (0, 0))]
    kern = functools.partial(_rwkv_scan_kernel, reverse=reverse, finish=finish is not None,
                             n_heads=D // HEAD_DIM)
    return pl.pallas_call(
        kern,
        grid=(B, nc),
        in_specs=specs,
        out_specs=tok,
        out_shape=jax.ShapeDtypeStruct((B, Tt, D), F32),
        scratch_shapes=[pltpu.VMEM((HEAD_DIM, D), F32)],
        compiler_params=_cparams(2),
        name="rwkv_scan_rev" if reverse else "rwkv_scan_fwd",
    )(*ins)


def rwkv_layer(x, mod, gate, p, n_ctx, tb):
    pr = rwkv_proj(x, mod, p, n_ctx, tb)
    y_fwd = rwkv_scan(pr, 0, n_ctx, reverse=False)
    z = rwkv_scan(pr, 1, n_ctx, reverse=True, finish=(y_fwd, p['gn']))
    return out_matmul([z], p['w_out'], x, gate, n_ctx, tb)


def _rglru_scan_kernel(*refs, reverse, finish, n_ctx_blocks):
    if finish:
        (x_ref, xp_ref, xn_ref, cw_ref, vec_ref, wg_ref, hf_ref, gs_ref, o_ref,
         a_scr, b_scr, carry_ref) = refs
    else:
        x_ref, xp_ref, xn_ref, cw_ref, vec_ref, wg_ref, o_ref, a_scr, b_scr, carry_ref = refs
    c = pl.program_id(1)
    nb = pl.num_programs(1)
    t = _scan_order(c, n_ctx_blocks, nb, reverse)

    @pl.when(c == 0)
    def _():
        carry_ref[...] = jnp.zeros_like(carry_ref)

    x = x_ref[0]
    tb, W = x.shape
    first = jnp.logical_or(t == 0, t == n_ctx_blocks)
    last = jnp.logical_or(t == n_ctx_blocks - 1, t == nb - 1)
    xp = jnp.where(first, 0.0, xp_ref[0])
    xn = jnp.where(last, 0.0, xn_ref[0])
    row = lambda a, i: jnp.broadcast_to(a[i:i + 1], x.shape)
    rows = _rows(x.shape)
    x_m1 = _shift_down(x, 1, row(xp, SUBLANES - 1))
    x_m2 = _shift_down(x, 2, jnp.where(rows == 0, row(xp, SUBLANES - 2), row(xp, SUBLANES - 1)))
    x_p1 = _shift_up(x, 1, row(xn, 0))
    cw = lambda i: cw_ref[i:i + 1, :]
    xc = cw(4) + x_m2 * cw(0) + x_m1 * cw(1) + x * cw(2) + x_p1 * cw(3)
    gates = _dot(xc, wg_ref[...])
    r = _sigmoid(gates[:, :W] + vec_ref[0:1])
    i = _sigmoid(gates[:, W:] + vec_ref[1:2])
    log_a = -LRU_C * r * _softplus(-vec_ref[2:3])
    a_scr[...] = jnp.exp(log_a)
    b_scr[...] = jnp.sqrt(1.0 - jnp.exp(2.0 * log_a)) * (i * xc)

    S = SUBLANES
    ng = tb // S
    sub = _rows((S, W))

    def group(gi, carry):
        g = (ng - 1 - gi) if reverse else gi
        a = a_scr[pl.ds(pl.multiple_of(g * S, S), S), :]
        b = b_scr[pl.ds(pl.multiple_of(g * S, S), S), :]
        for s in (1, 2, 4):
            if reverse:
                ok = sub < S - s
                a_s = jnp.where(ok, pltpu.roll(a, S - s, 0), 1.0)
                b_s = jnp.where(ok, pltpu.roll(b, S - s, 0), 0.0)
            else:
                ok = sub >= s
                a_s = jnp.where(ok, pltpu.roll(a, s, 0), 1.0)
                b_s = jnp.where(ok, pltpu.roll(b, s, 0), 0.0)
            b = a * b_s + b
            a = a * a_s
        h = b + a * carry
        b_scr[pl.ds(pl.multiple_of(g * S, S), S), :] = h
        e = 0 if reverse else S - 1
        return jnp.broadcast_to(h[e:e + 1], (S, W))

    carry_ref[...] = lax.fori_loop(0, ng, group, carry_ref[...])
    h_all = b_scr[...]
    if finish:
        h_all = (h_all + hf_ref[0]) * gs_ref[0]
    o_ref[0] = h_all


def rglru_scan(xr, p, d, n_ctx, tb, reverse, finish=None):
    B, Tt, W = xr.shape
    nblk = p['gate_w'].shape[2]
    bd = W // nblk
    ncb = n_ctx // tb
    nb = Tt // tb
    nb8 = Tt // SUBLANES
    r8 = tb // SUBLANES

    def dense(wb):
        eye = jnp.eye(nblk, dtype=wb.dtype)
        return jnp.einsum('ncd,nm->ncmd', wb, eye).reshape(W, W)

    wg = _bf(jnp.concatenate([dense(p['gate_w'][d, 0]), dense(p['gate_w'][d, 1])], axis=1))
    cw = jnp.zeros((SUBLANES, W), F32).at[0:4].set(p['conv_w']).at[4].set(p['conv_b'])
    vec = jnp.zeros((SUBLANES, W), F32).at[0:2].set(p['gate_b'][d]).at[2].set(p['lam'][d])
    order = lambda c: _scan_order(c, ncb, nb, reverse)
    tok = pl.BlockSpec((1, tb, W), lambda b, c: (b, order(c), 0))
    full = lambda shape: pl.BlockSpec(shape, lambda b, c: (0,) * len(shape))
    ins = [xr, xr, xr, cw, vec, wg]
    specs = [tok,
             pl.BlockSpec((1, SUBLANES, W), lambda b, c: (b, jnp.maximum(order(c) * r8 - 1, 0), 0)),
             pl.BlockSpec((1, SUBLANES, W), lambda b, c: (b, jnp.minimum((order(c) + 1) * r8, nb8 - 1), 0)),
             full((SUBLANES, W)), full((SUBLANES, W)), full((W, 2 * W))]
    if finish is not None:
        ins += list(finish)
        specs += [tok, tok]
    kern = functools.partial(_rglru_scan_kernel, reverse=reverse, finish=finish is not None,
                             n_ctx_blocks=ncb)
    return pl.pallas_call(
        kern,
        grid=(B, nb),
        in_specs=specs,
        out_specs=tok,
        out_shape=jax.ShapeDtypeStruct((B, Tt, W), F32),
        scratch_shapes=[pltpu.VMEM((tb, W), F32), pltpu.VMEM((tb, W), F32), pltpu.VMEM((SUBLANES, W), F32)],
        compiler_params=_cparams(2),
        name="rglru_scan_rev" if reverse else "rglru_scan_fwd",
    )(*ins)


def rglru_layer(x, mod, gate, p, n_ctx, tb):
    W = p['conv_w'].shape[1]
    xr, gs = in_matmul(x, mod, p['norm_g'], p['w_in'], (W, W), (None, "silu"), n_ctx, tb)
    h_fwd = rglru_scan(xr, p, 0, n_ctx, tb, reverse=False)
    z = rglru_scan(xr, p, 1, n_ctx, tb, reverse=True, finish=(h_fwd, gs))
    return out_matmul([z], p['w_out'], x, gate, n_ctx, tb)


def _qk_prep_kernel(q_ref, k_ref, v_ref, cs_ref, g_ref, sel_ref, selt_ref, qn_ref, qr_ref, kr_ref, vb_ref):
    reps = q_ref.shape[2] // cs_ref.shape[2]
    cos = jnp.concatenate([cs_ref[0]] * reps, axis=1)
    sin = jnp.concatenate([cs_ref[1]] * reps, axis=1)
    lane = lax.broadcasted_iota(jnp.int32, cos.shape, 1)
    quarter = HEAD_DIM // 4
    low = (lane % (2 * quarter)) < quarter
    D = cos.shape[1]

    def prep(x, g):
        ms = _head_sum(x * x, sel_ref, selt_ref) * (1.0 / HEAD_DIM)
        xn = x * lax.rsqrt(ms + RMS_EPS) * g
        partner = jnp.where(low, pltpu.roll(xn, D - quarter, 1), pltpu.roll(xn, quarter, 1))
        return xn, xn * cos + partner * sin

    qn, qr = prep(q_ref[0], g_ref[0:1])
    _, kr = prep(k_ref[0], g_ref[1:2])
    scale = HEAD_DIM ** -0.5
    qn_ref[0] = _bf(qn * scale)
    qr_ref[0] = _bf(qr * scale)
    kr_ref[0] = _bf(kr)
    vb_ref[0] = _bf(v_ref[0])


def qk_prep(q, k, v, qk_g, n_ctx, tb):
    B, Tt, D = q.shape
    H = D // HEAD_DIM
    T = Tt - n_ctx
    nfreq = HEAD_DIM // 4
    pos = jnp.arange(T)
    inv = ROPE_THETA ** (-jnp.arange(nfreq, dtype=F32) / nfreq)
    ang_r = (pos // GRID_W).astype(F32)[:, None] * inv
    ang_c = (pos % GRID_W).astype(F32)[:, None] * inv
    cos = jnp.concatenate([jnp.cos(ang_r)] * 2 + [jnp.cos(ang_c)] * 2, axis=1)
    sin = jnp.concatenate([-jnp.sin(ang_r), jnp.sin(ang_r), -jnp.sin(ang_c), jnp.sin(ang_c)], axis=1)
    cs = jnp.stack([jnp.concatenate([jnp.ones((n_ctx, HEAD_DIM), F32), cos], 0),
                    jnp.concatenate([jnp.zeros((n_ctx, HEAD_DIM), F32), sin], 0)])
    cs = jnp.concatenate([cs, cs], axis=2)
    g = jnp.zeros((SUBLANES, D), F32).at[0].set(jnp.tile(qk_g[0], H)).at[1].set(jnp.tile(qk_g[1], H))
    sel, selt = _head_selectors(D)
    tok = pl.BlockSpec((1, tb, D), lambda b, t: (b, t, 0))
    return pl.pallas_call(
        _qk_prep_kernel,
        grid=(B, Tt // tb),
        in_specs=[tok, tok, tok,
                  pl.BlockSpec((2, tb, 2 * HEAD_DIM), lambda b, t: (0, t, 0)),
                  pl.BlockSpec((SUBLANES, D), lambda b, t: (0, 0)),
                  pl.BlockSpec((D, LANES), lambda b, t: (0, 0)),
                  pl.BlockSpec((LANES, D), lambda b, t: (0, 0))],
        out_specs=[tok] * 4,
        out_shape=[jax.ShapeDtypeStruct((B, Tt, D), BF16)] * 4,
        compiler_params=_cparams(2),
        name="qk_prep",
    )(q, k, v, cs, g, sel, selt)


def _natten_kernel(qr_ref, qn_ref, kr_ref, v_ref, bias_ref, o_ref, *, rows, kh, rb, n_ctx):
    gw = GRID_W
    lane = lax.broadcasted_iota(jnp.int32, (gw, 2 * HEAD_DIM), 1)
    head_lanes = [lane < HEAD_DIM, lane >= HEAD_DIM]
    kc = kr_ref[0, 0:n_ctx, :]
    vc = v_ref[0, 0:n_ctx, :]
    nt = lambda a, b: lax.dot_general(a, b, (((1,), (1,)), ((), ())), preferred_element_type=F32)
    mm = lambda a, b: jnp.dot(a, b, preferred_element_type=F32)

    q_c = qn_ref[0, 0:n_ctx, :]
    lane_c = lax.broadcasted_iota(jnp.int32, q_c.shape, 1)
    o_c = []
    for h in range(2):
        mine = (lane_c >= HEAD_DIM) if h else (lane_c < HEAD_DIM)
        s = nt(jnp.where(mine, q_c, jnp.zeros_like(q_c)), kc)
        p = jnp.exp(s - s.max(axis=-1, keepdims=True))
        o_c.append(mm(_bf(p), vc) / p.sum(axis=-1, keepdims=True))
    o_ref[0, 0:n_ctx, :] = jnp.where(lane_c < HEAD_DIM, o_c[0], o_c[1])

    def row_group(g, carry):
        q0s, qrs, qns, kbs, vbs, biases = [], [], [], [], [], []
        for j in range(rb):
            r = g * rb + j
            start = jnp.clip(r - kh // 2, 0, rows - kh)
            d0 = start - r + kh - 1
            q0 = pl.multiple_of(n_ctx + r * gw, gw)
            k0 = pl.multiple_of(n_ctx + start * gw, gw)
            qr = qr_ref[0, pl.ds(q0, gw), :]
            qn = qn_ref[0, pl.ds(q0, gw), :]
            kb = kr_ref[0, pl.ds(k0, kh * gw), :]
            vb = v_ref[0, pl.ds(k0, kh * gw), :]
            zero = jnp.zeros_like(qr)
            for h in range(2):
                q0s.append(q0)
                qrs.append(jnp.where(head_lanes[h], qr, zero))
                qns.append(jnp.where(head_lanes[h], qn, zero))
                kbs.append(kb)
                vbs.append(vb)
                biases.append(bias_ref[h, d0])
        n = range(len(qrs))
        s_band = [nt(qrs[i], kbs[i]) + biases[i] for i in n]
        s_ctx = [nt(qns[i], kc) for i in n]
        m = [jnp.maximum(s_band[i].max(axis=-1, keepdims=True), s_ctx[i].max(axis=-1, keepdims=True))
             for i in n]
        p_band = [jnp.exp(s_band[i] - m[i]) for i in n]
        p_ctx = [jnp.exp(s_ctx[i] - m[i]) for i in n]
        den = [p_band[i].sum(axis=-1, keepdims=True) + p_ctx[i].sum(axis=-1, keepdims=True) for i in n]
        o = [(mm(_bf(p_band[i]), vbs[i]) + mm(_bf(p_ctx[i]), vc)) / den[i] for i in n]
        for j in range(rb):
            o_ref[0, pl.ds(q0s[2 * j], gw), :] = jnp.where(head_lanes[0], o[2 * j], o[2 * j + 1])
        return carry

    lax.fori_loop(0, rows // rb, row_group, 0)


def _natten_bias(rpb, rows, kh):
    cols = np.arange(GRID_W)
    c_start = np.clip(cols - WIN_W // 2, 0, GRID_W - WIN_W)
    col_ok = (cols[None, :] >= c_start[:, None]) & (cols[None, :] < c_start[:, None] + WIN_W)
    dc_idx = np.clip(cols[None, :] - cols[:, None] + WIN_W - 1, 0, 2 * WIN_W - 2)
    by_dr = jnp.where(jnp.asarray(col_ok), rpb[:, :, dc_idx].astype(F32), NEG_BIG)
    base = WIN_H - kh
    per_v = [jnp.concatenate([by_dr[:, base + v + i] for i in range(kh)], axis=-1) for v in range(kh)]
    return jnp.stack(per_v, axis=1)


def natten_attention(qr, qn, kr, vb, rpb, n_ctx):
    B, Tt, D = qr.shape
    T = Tt - n_ctx
    rows = T // GRID_W
    kh = min(WIN_H, rows)
    HP = D // (2 * HEAD_DIM)
    assert n_ctx % GRID_W == 0
    bias = _natten_bias(rpb, rows, kh)
    seq = pl.BlockSpec((1, Tt, 2 * HEAD_DIM), lambda hp, b: (b, 0, hp))
    rb = 2 if rows % 2 == 0 else 1
    kern = functools.partial(_natten_kernel, rows=rows, kh=kh, rb=rb, n_ctx=n_ctx)
    return pl.pallas_call(
        kern,
        grid=(HP, B),
        in_specs=[seq, seq, seq, seq,
                  pl.BlockSpec((2, kh, GRID_W, kh * GRID_W), lambda hp, b: (hp, 0, 0, 0))],
        out_specs=seq,
        out_shape=jax.ShapeDtypeStruct((B, Tt, D), F32),
        compiler_params=_cparams(2),
        name="natten",
    )(qr, qn, kr, vb, bias)


def natten_layer(x, mod, gate, p, n_ctx, tb):
    D = x.shape[2]
    q, k, v, gs = in_matmul(x, mod, p['norm_g'], p['w_in'], (D,) * 4, (None, None, None, "silu"),
                            n_ctx, tb)
    qn, qr, kr, vb = qk_prep(q, k, v, p['qk_g'], n_ctx, tb)
    o = natten_attention(qr, qn, kr, vb, p['rpb'], n_ctx)
    return out_matmul([o, gs], p['w_out'], x, gate, n_ctx, tb)


_LAYER_KEYS = (
    ('norm_g', 'ada_w', 'ada_b', 'w_in', 'mu', 'lora_b0', 'lora_down', 'lora_up', 'k_ka', 'r_k', 'gn', 'w_out'),
    ('norm_g', 'ada_w', 'ada_b', 'w_in', 'conv_w', 'conv_b', 'gate_w', 'gate_b', 'lam', 'w_out'),
    ('norm_g', 'ada_w', 'ada_b', 'w_in', 'qk_g', 'rpb', 'w_out'),
)
_LAYERS = (rwkv_layer, rglru_layer, natten_layer)


def _forward(x, c, ctx, c_ctx, layer_params, tb):
    B, T, D = x.shape
    n_ctx = ctx.shape[1]
    xs = jnp.concatenate([ctx, x], axis=1)
    m_rows = -(-(B + 1) // SUBLANES) * SUBLANES
    c_all = jnp.zeros((m_rows, D), F32).at[:B].set(c).at[B].set(c_ctx)
    for i, p in enumerate(layer_params):
        m = ada_mod(c_all, p['ada_w'], p['ada_b'])
        m_l = m[:B].reshape(B, 3, D)
        m_c = jnp.broadcast_to(m[B].reshape(1, 3, D), (B, 3, D))
        both = jnp.stack([m_c, m_l], axis=1)
        mod = jnp.stack([1.0 + both[:, :, 1], both[:, :, 0]], axis=2)
        gate = both[:, :, 2:3]
        xs = _LAYERS[i % 3](xs, mod, gate, p, n_ctx, tb)
    return xs[:, n_ctx:]


def kernel(x, c, ctx, c_ctx, l0_norm_g, l0_ada_w, l0_ada_b, l0_w_in, l0_mu, l0_lora_b0, l0_lora_down, l0_lora_up, l0_k_ka, l0_r_k, l0_gn, l0_w_out, l1_norm_g, l1_ada_w, l1_ada_b, l1_w_in, l1_conv_w, l1_conv_b, l1_gate_w, l1_gate_b, l1_lam, l1_w_out, l2_norm_g, l2_ada_w, l2_ada_b, l2_w_in, l2_qk_g, l2_rpb, l2_w_out, l3_norm_g, l3_ada_w, l3_ada_b, l3_w_in, l3_mu, l3_lora_b0, l3_lora_down, l3_lora_up, l3_k_ka, l3_r_k, l3_gn, l3_w_out):
    args = (l0_norm_g, l0_ada_w, l0_ada_b, l0_w_in, l0_mu, l0_lora_b0, l0_lora_down, l0_lora_up, l0_k_ka, l0_r_k, l0_gn, l0_w_out, l1_norm_g, l1_ada_w, l1_ada_b, l1_w_in, l1_conv_w, l1_conv_b, l1_gate_w, l1_gate_b, l1_lam, l1_w_out, l2_norm_g, l2_ada_w, l2_ada_b, l2_w_in, l2_qk_g, l2_rpb, l2_w_out, l3_norm_g, l3_ada_w, l3_ada_b, l3_w_in, l3_mu, l3_lora_b0, l3_lora_down, l3_lora_up, l3_k_ka, l3_r_k, l3_gn, l3_w_out)
    layer_params, pos = [], 0
    for i in range(4):
        keys = _LAYER_KEYS[i % 3]
        layer_params.append(dict(zip(keys, args[pos:pos + len(keys)])))
        pos += len(keys)
    return _forward(x, c, ctx, c_ctx, layer_params, tb=256)
```

```python
import functools

import jax
import jax.numpy as jnp
import numpy as np
from jax import lax
from jax.experimental import pallas as pl
from jax.experimental.pallas import tpu as pltpu

F32 = jnp.float32
BF16 = jnp.bfloat16

HEAD_DIM = 64
CHUNK = 64
SCAN_BATCH = 4
GRID_W = 64
WIN_H = 8
WIN_W = 16
ROPE_THETA = 10000.0
RMS_EPS = 1e-6
RW_GN_EPS = 64e-5
LRU_C = 8.0
SUBLANES = 8
LANES = 128
NEG_BIG = -1e30
VMEM_LIMIT = 56 * 1024 * 1024


def _cparams(n_axes):
    return pltpu.CompilerParams(
        dimension_semantics=("arbitrary",) * n_axes, vmem_limit_bytes=VMEM_LIMIT)


def _bf(x):
    return x.astype(BF16)


def _dot(a, b):
    return jnp.dot(_bf(a), _bf(b), preferred_element_type=F32)


def _dot3(a_exact_bf16, b):
    hi = _bf(b)
    r1 = b - hi.astype(F32)
    mid = _bf(r1)
    lo = _bf(r1 - mid.astype(F32))
    d = lambda t: jnp.dot(a_exact_bf16, t, preferred_element_type=F32)
    return d(hi) + d(mid) + d(lo)


def _sigmoid(x):
    return 0.5 * jnp.tanh(0.5 * x) + 0.5


def _silu(x):
    return x * _sigmoid(x)


def _softplus(x):
    return jnp.maximum(x, 0.0) + jnp.log(1.0 + jnp.exp(-jnp.abs(x)))


def _rows(shape):
    return lax.broadcasted_iota(jnp.int32, shape, 0)


def _shift_down(x, s, fill):
    return jnp.where(_rows(x.shape) >= s, pltpu.roll(x, s, 0), fill)


def _shift_up(x, s, fill):
    n = x.shape[0]
    return jnp.where(_rows(x.shape) < n - s, pltpu.roll(x, n - s, 0), fill)


def _norm_mod(xb, g, scale1, shift):
    xf = xb.astype(F32)
    ms = jnp.mean(xf * xf, axis=-1, keepdims=True)
    return xf * lax.rsqrt(ms + RMS_EPS) * g * scale1 + shift


def _ada_kernel(c_ref, w_ref, b_ref, o_ref):
    o_ref[...] = _dot(_silu(c_ref[...]), w_ref[...]) + b_ref[...]


def ada_mod(c_all, ada_w, ada_b):
    m, d = c_all.shape
    n = ada_w.shape[1]
    tn = d
    return pl.pallas_call(
        _ada_kernel,
        grid=(n // tn,),
        in_specs=[pl.BlockSpec((m, d), lambda j: (0, 0)),
                  pl.BlockSpec((d, tn), lambda j: (0, j)),
                  pl.BlockSpec((1, tn), lambda j: (0, j))],
        out_specs=pl.BlockSpec((m, tn), lambda j: (0, j)),
        out_shape=jax.ShapeDtypeStruct((m, n), F32),
        compiler_params=_cparams(1),
        name="ada_mod",
    )(c_all, _bf(ada_w), ada_b.reshape(1, n))


def _seg_of_block(t, n_ctx_blocks):
    return (t >= n_ctx_blocks).astype(jnp.int32)


def _scan_order(c, n_ctx_blocks, n_blocks, reverse):
    if not reverse:
        return c
    return jnp.where(c < n_ctx_blocks, n_ctx_blocks - 1 - c, n_blocks - 1 + n_ctx_blocks - c)


def _in_mm_kernel(x_ref, mod_ref, g_ref, w_ref, *o_refs, splits, acts):
    h = _norm_mod(x_ref[0], g_ref[...], mod_ref[0, 0, 0:1], mod_ref[0, 0, 1:2])
    hb = _bf(h)
    off = 0
    for o_ref, n, act in zip(o_refs, splits, acts):
        z = jnp.dot(hb, w_ref[:, off:off + n], preferred_element_type=F32)
        if act == "silu":
            z = _silu(z)
        o_ref[0] = z.astype(o_ref.dtype)
        off += n


def in_matmul(x, mod, norm_g, w, splits, acts, n_ctx, tb):
    B, Tt, D = x.shape
    ncb = n_ctx // tb
    n = w.shape[1]
    assert sum(splits) == n
    kern = functools.partial(_in_mm_kernel, splits=tuple(splits), acts=tuple(acts))
    return pl.pallas_call(
        kern,
        grid=(B, Tt // tb),
        in_specs=[pl.BlockSpec((1, tb, D), lambda b, t: (b, t, 0)),
                  pl.BlockSpec((1, 1, 2, D), lambda b, t: (b, _seg_of_block(t, ncb), 0, 0)),
                  pl.BlockSpec((1, D), lambda b, t: (0, 0)),
                  pl.BlockSpec((D, n), lambda b, t: (0, 0))],
        out_specs=[pl.BlockSpec((1, tb, s), lambda b, t: (b, t, 0)) for s in splits],
        out_shape=[jax.ShapeDtypeStruct((B, Tt, s), F32) for s in splits],
        compiler_params=_cparams(2),
        name="in_matmul",
    )(x, mod, norm_g.reshape(1, D), _bf(w))


def _out_mm_kernel(*refs, n_a):
    a_refs = refs[:n_a]
    w_ref, x_ref, gate_ref, o_ref = refs[n_a:]
    a = a_refs[0][0]
    for r in a_refs[1:]:
        a = a * r[0]
    o_ref[0] = x_ref[0] + gate_ref[0, 0] * _dot(a, w_ref[...])


def out_matmul(a_list, w, x, gate, n_ctx, tb):
    B, Tt, D = x.shape
    K = w.shape[0]
    ncb = n_ctx // tb
    kern = functools.partial(_out_mm_kernel, n_a=len(a_list))
    return pl.pallas_call(
        kern,
        grid=(B, Tt // tb),
        in_specs=[pl.BlockSpec((1, tb, K), lambda b, t: (b, t, 0)) for _ in a_list] + [
            pl.BlockSpec((K, D), lambda b, t: (0, 0)),
            pl.BlockSpec((1, tb, D), lambda b, t: (b, t, 0)),
            pl.BlockSpec((1, 1, 1, D), lambda b, t: (b, _seg_of_block(t, ncb), 0, 0))],
        out_specs=pl.BlockSpec((1, tb, D), lambda b, t: (b, t, 0)),
        out_shape=jax.ShapeDtypeStruct((B, Tt, D), F32),
        compiler_params=_cparams(2),
        name="out_matmul",
    )(*a_list, _bf(w), x, gate)


_V_NORM_G, _V_MU, _V_KK, _V_KA, _V_B0, _V_RK = 0, 1, 7, 8, 9, 13
_V_ROWS = 16


def _head_sum(x, sel_ref, selt_ref):
    mm = lambda a, w: jnp.dot(a, w, preferred_element_type=F32)
    hi = _bf(x)
    lo = _bf(x - hi.astype(F32))
    s = mm(hi, sel_ref[...]) + mm(lo, sel_ref[...])
    s_hi = _bf(s)
    s_lo = _bf(s - s_hi.astype(F32))
    return mm(s_hi, selt_ref[...]) + mm(s_lo, selt_ref[...])


def _head_selectors(D):
    hid = jnp.arange(D) // HEAD_DIM
    sel = (hid[:, None] == jnp.arange(LANES)[None, :]).astype(BF16)
    return sel, sel.T


def _rwkv_proj_kernel(x_ref, xp_ref, xn_ref, mod_ref, vec_ref, win_ref, dw_ref, da_ref, upw_ref,
                      upa_ref, sel_ref, selt_ref, r_ref, v_ref, g_ref, kk_ref, bonus_ref, lw0_ref,
                      a0_ref, k0_ref, lw1_ref, a1_ref, k1_ref, *, n_ctx_blocks):
    t = pl.program_id(1)
    nb = pl.num_programs(1)
    vec = lambda i: vec_ref[i:i + 1, :]
    scale1, shift = mod_ref[0, 0, 0:1], mod_ref[0, 0, 1:2]
    g = vec(_V_NORM_G)
    h = _norm_mod(x_ref[0], g, scale1, shift)
    first = jnp.logical_or(t == 0, t == n_ctx_blocks)
    last = jnp.logical_or(t == n_ctx_blocks - 1, t == nb - 1)
    hp = _norm_mod(xp_ref[0], g, scale1, shift)[SUBLANES - 1:SUBLANES]
    hn = _norm_mod(xn_ref[0], g, scale1, shift)[0:1]
    hp = jnp.where(first, 0.0, hp)
    hn = jnp.where(last, 0.0, hn)
    h_prev = _shift_down(h, 1, jnp.broadcast_to(hp, h.shape))
    h_next = _shift_up(h, 1, jnp.broadcast_to(hn, h.shape))
    xx = 0.5 * (h_prev + h_next) - h
    mix = lambda j: _bf(h + xx * vec(_V_MU + j))
    mm = lambda a, w: jnp.dot(a, w, preferred_element_type=F32)
    xr, xw, xk, xv, xa, xg = [mix(j) for j in range(6)]
    r = mm(xr, win_ref[0])
    k = mm(xk, win_ref[1])
    v = mm(xv, win_ref[2])
    r_ref[0] = r
    v_ref[0] = v
    g_ref[0] = _silu(mm(xg, win_ref[3]))
    kk_raw = k * vec(_V_KK)
    kk_ref[0] = kk_raw * jnp.minimum(lax.rsqrt(_head_sum(kk_raw * kk_raw, sel_ref, selt_ref)), 1e12)
    tw = _bf(jnp.tanh(mm(xw, dw_ref[...])))
    ta = _bf(mm(xa, da_ref[...]))
    k_a = vec(_V_KA)
    k_sum = None
    for d, (lw_ref, a_ref, kd_ref) in enumerate(((lw0_ref, a0_ref, k0_ref), (lw1_ref, a1_ref, k1_ref))):
        zw = vec(_V_B0 + 2 * d) + mm(tw, upw_ref[d])
        w_log = -_softplus(-zw) - 0.5
        lw_ref[0] = -jnp.exp(w_log)
        a_d = _sigmoid(vec(_V_B0 + 2 * d + 1) + mm(ta, upa_ref[d]))
        a_ref[0] = a_d
        k_d = k * (1.0 + (a_d - 1.0) * k_a)
        kd_ref[0] = k_d
        k_sum = k_d if k_sum is None else k_sum + k_d
    bonus_ref[0] = _head_sum(r * k_sum * vec(_V_RK), sel_ref, selt_ref) * v


def rwkv_proj(x, mod, p, n_ctx, tb):
    B, Tt, D = x.shape
    lora = p['lora_down'].shape[-1]
    ncb = n_ctx // tb
    nb8 = Tt // SUBLANES
    r8 = tb // SUBLANES
    vec = jnp.zeros((_V_ROWS, D), F32)
    vec = vec.at[_V_NORM_G].set(p['norm_g']).at[_V_MU:_V_MU + 6].set(p['mu'])
    vec = vec.at[_V_KK].set(p['k_ka'][0]).at[_V_KA].set(p['k_ka'][1])
    vec = vec.at[_V_B0:_V_B0 + 4].set(p['lora_b0'].reshape(4, D)).at[_V_RK].set(p['r_k'].reshape(D))
    sel, selt = _head_selectors(D)
    down, up = p['lora_down'], p['lora_up']
    dw = _bf(jnp.concatenate([down[0, 0], down[1, 0]], axis=1))
    da = _bf(jnp.concatenate([down[0, 1], down[1, 1]], axis=1))
    z = jnp.zeros((lora, D), F32)
    upw = _bf(jnp.stack([jnp.concatenate([up[0, 0], z], 0), jnp.concatenate([z, up[1, 0]], 0)]))
    upa = _bf(jnp.stack([jnp.concatenate([up[0, 1], z], 0), jnp.concatenate([z, up[1, 1]], 0)]))
    full = lambda shape: pl.BlockSpec(shape, lambda b, t: (0,) * len(shape))
    tok = pl.BlockSpec((1, tb, D), lambda b, t: (b, t, 0))
    kern = functools.partial(_rwkv_proj_kernel, n_ctx_blocks=ncb)
    outs = pl.pallas_call(
        kern,
        grid=(B, Tt // tb),
        in_specs=[tok,
                  pl.BlockSpec((1, SUBLANES, D), lambda b, t: (b, jnp.maximum(t * r8 - 1, 0), 0)),
                  pl.BlockSpec((1, SUBLANES, D), lambda b, t: (b, jnp.minimum((t + 1) * r8, nb8 - 1), 0)),
                  pl.BlockSpec((1, 1, 2, D), lambda b, t: (b, _seg_of_block(t, ncb), 0, 0)),
                  full((_V_ROWS, D)), full((4, D, D)), full((D, 2 * lora)), full((D, 2 * lora)),
                  full((2, 2 * lora, D)), full((2, 2 * lora, D)), full((D, LANES)), full((LANES, D))],
        out_specs=[tok] * 11,
        out_shape=[jax.ShapeDtypeStruct((B, Tt, D), F32)] * 11,
        compiler_params=_cparams(2),
        name="rwkv_proj",
    )(x, x, x, mod, vec, _bf(p['w_in']), dw, da, upw, upa, sel, selt)
    names = ('r', 'v', 'gate', 'kk', 'bonus', 'lw0', 'a0', 'k0', 'lw1', 'a1', 'k1')
    return dict(zip(names, outs))


def _level_masks(L, reverse):
    ri = lax.broadcasted_iota(jnp.int32, (L, L), 0)
    ci = lax.broadcasted_iota(jnp.int32, (L, L), 1)
    if reverse:
        ri, ci = ci, ri
    masks = []
    for j in range(int(np.log2(L))):
        same = (ri >> (j + 1)) == (ci >> (j + 1))
        masks.append(same & (((ri >> j) & 1) == 1) & (((ci >> j) & 1) == 0))
    return masks


def _rwkv_chunk_heads(rt, kap, bt, kt, v, wl, st, tri_strict2, tri_incl2, eye, masks, eye_l):
    hs = range(len(rt))
    L, hd = v[0].shape
    mm = lambda a, b: jnp.dot(a, b, preferred_element_type=F32)
    mm_nt = lambda a, b: lax.dot_general(a, b, (((1,), (1,)), ((), ())), preferred_element_type=F32)
    mm_tn = lambda a, b: lax.dot_general(a, b, (((0,), (0,)), ((), ())), preferred_element_type=F32)
    zeros = jnp.zeros((L, hd), BF16)
    v_b = [_bf(v[h]) for h in hs]
    lhs = [jnp.concatenate([_bf(kap[h]), _bf(rt[h])], axis=0) for h in hs]
    rhs = [jnp.concatenate([_bf(bt[h]), _bf(kt[h])], axis=0) for h in hs]
    s = [mm_nt(lhs[h], rhs[h]) for h in hs]
    u = [jnp.where(tri_strict2, s[h][:L], 0.0) for h in hs]
    a = [_bf(jnp.where(tri_incl2, s[h][L:], 0.0)) for h in hs]
    n_b = [u[h][:, :L] for h in hs]
    t = [eye_l - jnp.where(masks[0], n_b[h], 0.0) for h in hs]
    for m in masks[1:]:
        t_b = [_bf(t[h]) for h in hs]
        ct = [_bf(mm(_bf(jnp.where(m, n_b[h], 0.0)), t_b[h])) for h in hs]
        t = [t[h] - mm(t_b[h], ct[h]) for h in hs]
    nkv = [mm(_bf(u[h]), jnp.concatenate([zeros, v_b[h]], axis=0)) for h in hs]
    eg = [_bf(-mm(_bf(t[h]), _bf(jnp.concatenate([nkv[h], kap[h]], axis=1)))) for h in hs]
    egv = [jnp.concatenate([eg[h], jnp.concatenate([v_b[h], zeros], axis=1)], axis=0) for h in hs]
    ae = [mm(a[h], egv[h]) for h in hs]
    y0 = [ae[h][:, :hd] for h in hs]
    qp = [rt[h] + ae[h][:, hd:] for h in hs]
    bk = [jnp.concatenate([_bf(bt[h] * wl[h]), _bf(kt[h] * wl[h])], axis=0) for h in hs]
    up = [mm_tn(bk[h], egv[h]) for h in hs]
    ut = [up[h][:, :hd] for h in hs]
    pt = [jnp.where(eye, wl[h], 0.0) + up[h][:, hd:] for h in hs]
    ys = [mm(_bf(jnp.concatenate([qp[h], pt[h]], axis=0)), _bf(st[h])) for h in hs]
    return [ys[h][:L] + y0[h] for h in hs], [ys[h][L:] + ut[h] for h in hs]


def _rwkv_scan_kernel(*refs, reverse, finish, n_heads, bb):
    if finish:
        (r_ref, lw_ref, k_ref, kk_ref, a_ref, v_ref, yf_ref, bonus_ref, gate_ref, vec_ref,
         o_ref, st_ref) = refs
    else:
        r_ref, lw_ref, k_ref, kk_ref, a_ref, v_ref, o_ref, st_ref = refs

    @pl.when(pl.program_id(1) == 0)
    def _():
        st_ref[...] = jnp.zeros_like(st_ref)

    L = r_ref.shape[1]
    hd = HEAD_DIM
    ri = lax.broadcasted_iota(jnp.int32, (L, L), 0)
    ci = lax.broadcasted_iota(jnp.int32, (L, L), 1)
    tri_incl = (ci >= ri) if reverse else (ci <= ri)
    ri2 = lax.broadcasted_iota(jnp.int32, (L, 2 * L), 0)
    ci2 = lax.broadcasted_iota(jnp.int32, (L, 2 * L), 1) & (L - 1)
    if reverse:
        tri_incl2, tri_strict2 = ci2 >= ri2, ci2 > ri2
    else:
        tri_incl2, tri_strict2 = ci2 <= ri2, ci2 < ri2
    eye = lax.broadcasted_iota(jnp.int32, (hd, hd), 0) == lax.broadcasted_iota(jnp.int32, (hd, hd), 1)
    masks = _level_masks(L, reverse)
    eye_l = jnp.where(ri == ci, 1.0, 0.0)
    tri_ones = jnp.where(tri_incl, 1.0, 0.0).astype(BF16)
    end = 0 if reverse else L - 1

    sls = [slice(h * hd, (h + 1) * hd) for h in range(n_heads)]
    split = lambda x: [x[:, sl] for sl in sls]
    rt, kap, bt, kt, vs, wl, st = [], [], [], [], [], [], []
    for i in range(bb):
        lw = lw_ref[i]
        cum = _dot3(tri_ones, lw)
        w_t = jnp.exp(cum)
        w_i = jnp.exp(-cum)
        w_p = jnp.exp(cum - lw)
        kk = kk_ref[i]
        rt += split(r_ref[i] * w_t)
        kap += split(kk * w_p)
        bt += split(kk * a_ref[i] * w_i)
        kt += split(k_ref[i] * w_i)
        vs += split(v_ref[i])
        wl += split(w_t[end:end + 1])
        st += split(st_ref[i])
    outs, states = _rwkv_chunk_heads(rt, kap, bt, kt, vs, wl, st,
                                     tri_strict2, tri_incl2, eye, masks, eye_l)
    ones = jnp.ones((hd, hd), BF16)
    lane_sum = lambda x: jnp.dot(x, ones, preferred_element_type=F32)
    for i in range(bb):
        mine = slice(i * n_heads, (i + 1) * n_heads)
        st_ref[i] = jnp.concatenate(states[mine], axis=1)
        if finish:
            y_fwd = split(yf_ref[i])
            ys = jnp.concatenate([o + y for o, y in zip(outs[mine], y_fwd)], axis=0)
            ys_hi = _bf(ys)
            mean = (lane_sum(ys_hi) + lane_sum(_bf(ys - ys_hi.astype(F32)))) * (1.0 / hd)
            cen = ys - mean
            c2 = cen * cen
            c2_hi = _bf(c2)
            var = (lane_sum(c2_hi) + lane_sum(_bf(c2 - c2_hi.astype(F32)))) * (1.0 / hd)
            yn = cen * lax.rsqrt(var + RW_GN_EPS)
            out = jnp.concatenate([yn[h * L:(h + 1) * L] for h in range(n_heads)], axis=1)
            out = (out * vec_ref[0:1] + vec_ref[1:2] + bonus_ref[i]) * gate_ref[i]
        else:
            out = jnp.concatenate(outs[mine], axis=1)
        o_ref[i] = out.astype(o_ref.dtype)


def rwkv_scan(pr, d, n_ctx, reverse, finish=None):
    r = pr['r']
    B, Tt, D = r.shape
    L = CHUNK
    nc, ncc = Tt // L, n_ctx // L
    bb = next(n for n in (SCAN_BATCH, 2, 1) if B % n == 0)
    tok = pl.BlockSpec((bb, L, D), lambda b, c: (b, _scan_order(c, ncc, nc, reverse), 0))
    ins = [r, pr['lw%d' % d], pr['k%d' % d], pr['kk'], pr['a%d' % d], pr['v']]
    specs = [tok] * 6
    if finish is not None:
        y_fwd, gn = finish
        vec = jnp.zeros((SUBLANES, D), F32).at[0].set(gn[0]).at[1].set(gn[1])
        ins += [y_fwd, pr['bonus'], pr['gate'], vec]
        specs += [tok, tok, tok, pl.BlockSpec((SUBLANES, D), lambda b, c: (0, 0))]
    kern = functools.partial(_rwkv_scan_kernel, reverse=reverse, finish=finish is not None,
                             n_heads=D // HEAD_DIM, bb=bb)
    return pl.pallas_call(
        kern,
        grid=(B // bb, nc),
        in_specs=specs,
        out_specs=tok,
        out_shape=jax.ShapeDtypeStruct((B, Tt, D), F32 if finish is None else BF16),
        scratch_shapes=[pltpu.VMEM((bb, HEAD_DIM, D), F32)],
        compiler_params=_cparams(2),
        name="rwkv_scan_rev" if reverse else "rwkv_scan_fwd",
    )(*ins)


def rwkv_layer(x, mod, gate, p, n_ctx, tb):
    pr = rwkv_proj(x, mod, p, n_ctx, tb)
    y_fwd = rwkv_scan(pr, 0, n_ctx, reverse=False)
    z = rwkv_scan(pr, 1, n_ctx, reverse=True, finish=(y_fwd, p['gn']))
    return out_matmul([z], p['w_out'], x, gate, n_ctx, tb)


GATE_COLS = 256


def _gate_tiles(W, bd):
    assert bd <= LANES and W % LANES == 0
    win = min(W, GATE_COLS + 2 * LANES)
    tiles = []
    for c0 in range(0, W, GATE_COLS):
        lo = min(max(c0 - LANES, 0), W - win)
        tiles.append((c0, min(GATE_COLS, W - c0), lo))
    return win, tiles


def _rglru_scan_kernel(*refs, reverse, finish, n_ctx_blocks, block_dim):
    if finish:
        (x_ref, xp_ref, xn_ref, cw_ref, vec_ref, wg_ref, hf_ref, gs_ref, o_ref,
         a_scr, b_scr, carry_ref) = refs
    else:
        x_ref, xp_ref, xn_ref, cw_ref, vec_ref, wg_ref, o_ref, a_scr, b_scr, carry_ref = refs
    c = pl.program_id(1)
    nb = pl.num_programs(1)
    t = _scan_order(c, n_ctx_blocks, nb, reverse)

    @pl.when(c == 0)
    def _():
        carry_ref[...] = jnp.zeros_like(carry_ref)

    x = x_ref[0]
    tb, W = x.shape
    first = jnp.logical_or(t == 0, t == n_ctx_blocks)
    last = jnp.logical_or(t == n_ctx_blocks - 1, t == nb - 1)
    xp = jnp.where(first, 0.0, xp_ref[0])
    xn = jnp.where(last, 0.0, xn_ref[0])
    row = lambda a, i: jnp.broadcast_to(a[i:i + 1], x.shape)
    rows = _rows(x.shape)
    x_m1 = _shift_down(x, 1, row(xp, SUBLANES - 1))
    x_m2 = _shift_down(x, 2, jnp.where(rows == 0, row(xp, SUBLANES - 2), row(xp, SUBLANES - 1)))
    x_p1 = _shift_up(x, 1, row(xn, 0))
    cw = lambda i: cw_ref[i:i + 1, :]
    xc = cw(4) + x_m2 * cw(0) + x_m1 * cw(1) + x * cw(2) + x_p1 * cw(3)
    win, tiles = _gate_tiles(W, block_dim)
    xc_b = _bf(xc)
    z = [jnp.dot(xc_b[:, lo:lo + win], wg_ref[j], preferred_element_type=F32)
         for j, (_, _, lo) in enumerate(tiles)]
    gate_r = jnp.concatenate([z[j][:, :cw_] for j, (_, cw_, _) in enumerate(tiles)], axis=1)
    gate_i = jnp.concatenate([z[j][:, GATE_COLS:GATE_COLS + cw_] for j, (_, cw_, _) in enumerate(tiles)],
                             axis=1)
    r = _sigmoid(gate_r + vec_ref[0:1])
    i = _sigmoid(gate_i + vec_ref[1:2])
    log_a = -LRU_C * r * _softplus(-vec_ref[2:3])
    a = jnp.exp(log_a)
    a_scr[...] = a
    b_scr[...] = jnp.sqrt(1.0 - a * a) * (i * xc)

    S = SUBLANES
    ng = tb // S
    sub = _rows((S, W))

    def group(gi, carry):
        g = (ng - 1 - gi) if reverse else gi
        a = a_scr[pl.ds(pl.multiple_of(g * S, S), S), :]
        b = b_scr[pl.ds(pl.multiple_of(g * S, S), S), :]
        for s in (1, 2, 4):
            if reverse:
                ok = sub < S - s
                a_s = jnp.where(ok, pltpu.roll(a, S - s, 0), 1.0)
                b_s = jnp.where(ok, pltpu.roll(b, S - s, 0), 0.0)
            else:
                ok = sub >= s
                a_s = jnp.where(ok, pltpu.roll(a, s, 0), 1.0)
                b_s = jnp.where(ok, pltpu.roll(b, s, 0), 0.0)
            b = a * b_s + b
            a = a * a_s
        h = b + a * carry
        b_scr[pl.ds(pl.multiple_of(g * S, S), S), :] = h
        e = 0 if reverse else S - 1
        return jnp.broadcast_to(h[e:e + 1], (S, W))

    carry_ref[...] = lax.fori_loop(0, ng, group, carry_ref[...])
    h_all = b_scr[...]
    if finish:
        h_all = (h_all + hf_ref[0]) * gs_ref[0]
    o_ref[0] = h_all.astype(o_ref.dtype)


def rglru_scan(xr, p, d, n_ctx, tb, reverse, finish=None):
    B, Tt, W = xr.shape
    nblk = p['gate_w'].shape[2]
    bd = W // nblk
    ncb = n_ctx // tb
    nb = Tt // tb
    nb8 = Tt // SUBLANES
    r8 = tb // SUBLANES

    def dense(wb):
        eye = jnp.eye(nblk, dtype=wb.dtype)
        return jnp.einsum('ncd,nm->ncmd', wb, eye).reshape(W, W)

    win, tiles = _gate_tiles(W, bd)
    w_r, w_i = dense(p['gate_w'][d, 0]), dense(p['gate_w'][d, 1])
    pad = lambda m: jnp.pad(m, ((0, 0), (0, GATE_COLS - m.shape[1])))
    wg = _bf(jnp.stack([jnp.concatenate([pad(w_r[lo:lo + win, c0:c0 + cw_]), pad(w_i[lo:lo + win, c0:c0 + cw_])],
                                        axis=1) for c0, cw_, lo in tiles]))
    cw = jnp.zeros((SUBLANES, W), F32).at[0:4].set(p['conv_w']).at[4].set(p['conv_b'])
    vec = jnp.zeros((SUBLANES, W), F32).at[0:2].set(p['gate_b'][d]).at[2].set(p['lam'][d])
    order = lambda c: _scan_order(c, ncb, nb, reverse)
    tok = pl.BlockSpec((1, tb, W), lambda b, c: (b, order(c), 0))
    full = lambda shape: pl.BlockSpec(shape, lambda b, c: (0,) * len(shape))
    ins = [xr, xr, xr, cw, vec, wg]
    specs = [tok,
             pl.BlockSpec((1, SUBLANES, W), lambda b, c: (b, jnp.maximum(order(c) * r8 - 1, 0), 0)),
             pl.BlockSpec((1, SUBLANES, W), lambda b, c: (b, jnp.minimum((order(c) + 1) * r8, nb8 - 1), 0)),
             full((SUBLANES, W)), full((SUBLANES, W)), full((len(tiles), win, 2 * GATE_COLS))]
    if finish is not None:
        ins += list(finish)
        specs += [tok, tok]
    kern = functools.partial(_rglru_scan_kernel, reverse=reverse, finish=finish is not None,
                             n_ctx_blocks=ncb, block_dim=bd)
    return pl.pallas_call(
        kern,
        grid=(B, nb),
        in_specs=specs,
        out_specs=tok,
        out_shape=jax.ShapeDtypeStruct((B, Tt, W), F32 if finish is None else BF16),
        scratch_shapes=[pltpu.VMEM((tb, W), F32), pltpu.VMEM((tb, W), F32), pltpu.VMEM((SUBLANES, W), F32)],
        compiler_params=_cparams(2),
        name="rglru_scan_rev" if reverse else "rglru_scan_fwd",
    )(*ins)


def rglru_layer(x, mod, gate, p, n_ctx, tb):
    W = p['conv_w'].shape[1]
    xr, gs = in_matmul(x, mod, p['norm_g'], p['w_in'], (W, W), (None, "silu"), n_ctx, tb)
    h_fwd = rglru_scan(xr, p, 0, n_ctx, tb, reverse=False)
    z = rglru_scan(xr, p, 1, n_ctx, tb, reverse=True, finish=(h_fwd, gs))
    return out_matmul([z], p['w_out'], x, gate, n_ctx, tb)


def _qk_prep_kernel(q_ref, k_ref, v_ref, cs_ref, g_ref, sel_ref, selt_ref, qn_ref, qr_ref, kr_ref, vb_ref):
    reps = q_ref.shape[2] // cs_ref.shape[2]
    cos = jnp.concatenate([cs_ref[0]] * reps, axis=1)
    sin = jnp.concatenate([cs_ref[1]] * reps, axis=1)
    lane = lax.broadcasted_iota(jnp.int32, cos.shape, 1)
    quarter = HEAD_DIM // 4
    low = (lane % (2 * quarter)) < quarter
    D = cos.shape[1]

    def prep(x, g):
        ms = _head_sum(x * x, sel_ref, selt_ref) * (1.0 / HEAD_DIM)
        xn = x * lax.rsqrt(ms + RMS_EPS) * g
        partner = jnp.where(low, pltpu.roll(xn, D - quarter, 1), pltpu.roll(xn, quarter, 1))
        return xn, xn * cos + partner * sin

    qn, qr = prep(q_ref[0], g_ref[0:1])
    _, kr = prep(k_ref[0], g_ref[1:2])
    scale = HEAD_DIM ** -0.5
    qn_ref[0] = _bf(qn * scale)
    qr_ref[0] = _bf(qr * scale)
    kr_ref[0] = _bf(kr)
    vb_ref[0] = _bf(v_ref[0])


def qk_prep(q, k, v, qk_g, n_ctx, tb):
    B, Tt, D = q.shape
    H = D // HEAD_DIM
    T = Tt - n_ctx
    nfreq = HEAD_DIM // 4
    pos = jnp.arange(T)
    inv = ROPE_THETA ** (-jnp.arange(nfreq, dtype=F32) / nfreq)
    ang_r = (pos // GRID_W).astype(F32)[:, None] * inv
    ang_c = (pos % GRID_W).astype(F32)[:, None] * inv
    cos = jnp.concatenate([jnp.cos(ang_r)] * 2 + [jnp.cos(ang_c)] * 2, axis=1)
    sin = jnp.concatenate([-jnp.sin(ang_r), jnp.sin(ang_r), -jnp.sin(ang_c), jnp.sin(ang_c)], axis=1)
    cs = jnp.stack([jnp.concatenate([jnp.ones((n_ctx, HEAD_DIM), F32), cos], 0),
                    jnp.concatenate([jnp.zeros((n_ctx, HEAD_DIM), F32), sin], 0)])
    cs = jnp.concatenate([cs, cs], axis=2)
    g = jnp.zeros((SUBLANES, D), F32).at[0].set(jnp.tile(qk_g[0], H)).at[1].set(jnp.tile(qk_g[1], H))
    sel, selt = _head_selectors(D)
    tok = pl.BlockSpec((1, tb, D), lambda b, t: (b, t, 0))
    return pl.pallas_call(
        _qk_prep_kernel,
        grid=(B, Tt // tb),
        in_specs=[tok, tok, tok,
                  pl.BlockSpec((2, tb, 2 * HEAD_DIM), lambda b, t: (0, t, 0)),
                  pl.BlockSpec((SUBLANES, D), lambda b, t: (0, 0)),
                  pl.BlockSpec((D, LANES), lambda b, t: (0, 0)),
                  pl.BlockSpec((LANES, D), lambda b, t: (0, 0))],
        out_specs=[tok] * 4,
        out_shape=[jax.ShapeDtypeStruct((B, Tt, D), BF16)] * 4,
        compiler_params=_cparams(2),
        name="qk_prep",
    )(q, k, v, cs, g, sel, selt)


def _natten_kernel(qr_ref, qn_ref, kr_ref, v_ref, bias_ref, o_ref, *, rows, kh, rb, n_ctx):
    gw = GRID_W
    lane = lax.broadcasted_iota(jnp.int32, (gw, 2 * HEAD_DIM), 1)
    head_lanes = [lane < HEAD_DIM, lane >= HEAD_DIM]
    kc = kr_ref[0, 0:n_ctx, :]
    vc = v_ref[0, 0:n_ctx, :]
    nt = lambda a, b: lax.dot_general(a, b, (((1,), (1,)), ((), ())), preferred_element_type=F32)
    mm = lambda a, b: jnp.dot(a, b, preferred_element_type=F32)

    q_c = qn_ref[0, 0:n_ctx, :]
    lane_c = lax.broadcasted_iota(jnp.int32, q_c.shape, 1)
    o_c = []
    for h in range(2):
        mine = (lane_c >= HEAD_DIM) if h else (lane_c < HEAD_DIM)
        s = nt(jnp.where(mine, q_c, jnp.zeros_like(q_c)), kc)
        p = jnp.exp(s - s.max(axis=-1, keepdims=True))
        o_c.append(mm(_bf(p), vc) / p.sum(axis=-1, keepdims=True))
    o_ref[0, 0:n_ctx, :] = jnp.where(lane_c < HEAD_DIM, o_c[0], o_c[1])

    def row_group(g, carry):
        q0s, qrs, qns, kbs, vbs, biases = [], [], [], [], [], []
        for j in range(rb):
            r = g * rb + j
            start = jnp.clip(r - kh // 2, 0, rows - kh)
            d0 = start - r + kh - 1
            q0 = pl.multiple_of(n_ctx + r * gw, gw)
            k0 = pl.multiple_of(n_ctx + start * gw, gw)
            qr = qr_ref[0, pl.ds(q0, gw), :]
            qn = qn_ref[0, pl.ds(q0, gw), :]
            zero = jnp.zeros_like(qr)
            by_head = lambda q: jnp.concatenate([jnp.where(m, q, zero) for m in head_lanes], axis=0)
            q0s.append(q0)
            qrs.append(by_head(qr))
            qns.append(by_head(qn))
            kbs.append(kr_ref[0, pl.ds(k0, kh * gw), :])
            vbs.append(v_ref[0, pl.ds(k0, kh * gw), :])
            biases.append(jnp.concatenate([bias_ref[0, d0], bias_ref[1, d0]], axis=0))
        n = range(rb)
        s_band = [nt(qrs[i], kbs[i]) + biases[i] for i in n]
        s_ctx = [nt(qns[i], kc) for i in n]
        m = [jnp.maximum(s_band[i].max(axis=-1, keepdims=True), s_ctx[i].max(axis=-1, keepdims=True))
             for i in n]
        p_band = [jnp.exp(s_band[i] - m[i]) for i in n]
        p_ctx = [jnp.exp(s_ctx[i] - m[i]) for i in n]
        den = [p_band[i].sum(axis=-1, keepdims=True) + p_ctx[i].sum(axis=-1, keepdims=True) for i in n]
        o = [(mm(_bf(p_band[i]), vbs[i]) + mm(_bf(p_ctx[i]), vc)) / den[i] for i in n]
        for i in n:
            o_ref[0, pl.ds(q0s[i], gw), :] = jnp.where(head_lanes[0], o[i][:gw], o[i][gw:])
        return carry

    lax.fori_loop(0, rows // rb, row_group, 0)


def _natten_bias(rpb, rows, kh):
    cols = np.arange(GRID_W)
    c_start = np.clip(cols - WIN_W // 2, 0, GRID_W - WIN_W)
    col_ok = (cols[None, :] >= c_start[:, None]) & (cols[None, :] < c_start[:, None] + WIN_W)
    dc_idx = np.clip(cols[None, :] - cols[:, None] + WIN_W - 1, 0, 2 * WIN_W - 2)
    by_dr = jnp.where(jnp.asarray(col_ok), rpb[:, :, dc_idx].astype(F32), NEG_BIG)
    base = WIN_H - kh
    per_v = [jnp.concatenate([by_dr[:, base + v + i] for i in range(kh)], axis=-1) for v in range(kh)]
    return jnp.stack(per_v, axis=1)


def natten_attention(qr, qn, kr, vb, rpb, n_ctx):
    B, Tt, D = qr.shape
    T = Tt - n_ctx
    rows = T // GRID_W
    kh = min(WIN_H, rows)
    HP = D // (2 * HEAD_DIM)
    assert n_ctx % GRID_W == 0
    bias = _natten_bias(rpb, rows, kh)
    seq = pl.BlockSpec((1, Tt, 2 * HEAD_DIM), lambda hp, b: (b, 0, hp))
    rb = 4 if rows % 4 == 0 else 1
    kern = functools.partial(_natten_kernel, rows=rows, kh=kh, rb=rb, n_ctx=n_ctx)
    return pl.pallas_call(
        kern,
        grid=(HP, B),
        in_specs=[seq, seq, seq, seq,
                  pl.BlockSpec((2, kh, GRID_W, kh * GRID_W), lambda hp, b: (hp, 0, 0, 0))],
        out_specs=seq,
        out_shape=jax.ShapeDtypeStruct((B, Tt, D), F32),
        compiler_params=_cparams(2),
        name="natten",
    )(qr, qn, kr, vb, bias)


def natten_layer(x, mod, gate, p, n_ctx, tb):
    D = x.shape[2]
    q, k, v, gs = in_matmul(x, mod, p['norm_g'], p['w_in'], (D,) * 4, (None, None, None, "silu"),
                            n_ctx, tb)
    qn, qr, kr, vb = qk_prep(q, k, v, p['qk_g'], n_ctx, tb)
    o = natten_attention(qr, qn, kr, vb, p['rpb'], n_ctx)
    return out_matmul([o, gs], p['w_out'], x, gate, n_ctx, tb)


_LAYER_KEYS = (
    ('norm_g', 'ada_w', 'ada_b', 'w_in', 'mu', 'lora_b0', 'lora_down', 'lora_up', 'k_ka', 'r_k', 'gn', 'w_out'),
    ('norm_g', 'ada_w', 'ada_b', 'w_in', 'conv_w', 'conv_b', 'gate_w', 'gate_b', 'lam', 'w_out'),
    ('norm_g', 'ada_w', 'ada_b', 'w_in', 'qk_g', 'rpb', 'w_out'),
)
_LAYERS = (rwkv_layer, rglru_layer, natten_layer)


def _forward(x, c, ctx, c_ctx, layer_params, tb):
    B, T, D = x.shape
    n_ctx = ctx.shape[1]
    xs = jnp.concatenate([ctx, x], axis=1)
    m_rows = -(-(B + 1) // SUBLANES) * SUBLANES
    c_all = jnp.zeros((m_rows, D), F32).at[:B].set(c).at[B].set(c_ctx)
    for i, p in enumerate(layer_params):
        m = ada_mod(c_all, p['ada_w'], p['ada_b'])
        m_l = m[:B].reshape(B, 3, D)
        m_c = jnp.broadcast_to(m[B].reshape(1, 3, D), (B, 3, D))
        both = jnp.stack([m_c, m_l], axis=1)
        mod = jnp.stack([1.0 + both[:, :, 1], both[:, :, 0]], axis=2)
        gate = both[:, :, 2:3]
        xs = _LAYERS[i % 3](xs, mod, gate, p, n_ctx, tb)
    return xs[:, n_ctx:]


def kernel(x, c, ctx, c_ctx, l0_norm_g, l0_ada_w, l0_ada_b, l0_w_in, l0_mu, l0_lora_b0, l0_lora_down, l0_lora_up, l0_k_ka, l0_r_k, l0_gn, l0_w_out, l1_norm_g, l1_ada_w, l1_ada_b, l1_w_in, l1_conv_w, l1_conv_b, l1_gate_w, l1_gate_b, l1_lam, l1_w_out, l2_norm_g, l2_ada_w, l2_ada_b, l2_w_in, l2_qk_g, l2_rpb, l2_w_out, l3_norm_g, l3_ada_w, l3_ada_b, l3_w_in, l3_mu, l3_lora_b0, l3_lora_down, l3_lora_up, l3_k_ka, l3_r_k, l3_gn, l3_w_out):
    args = (l0_norm_g, l0_ada_w, l0_ada_b, l0_w_in, l0_mu, l0_lora_b0, l0_lora_down, l0_lora_up, l0_k_ka, l0_r_k, l0_gn, l0_w_out, l1_norm_g, l1_ada_w, l1_ada_b, l1_w_in, l1_conv_w, l1_conv_b, l1_gate_w, l1_gate_b, l1_lam, l1_w_out, l2_norm_g, l2_ada_w, l2_ada_b, l2_w_in, l2_qk_g, l2_rpb, l2_w_out, l3_norm_g, l3_ada_w, l3_ada_b, l3_w_in, l3_mu, l3_lora_b0, l3_lora_down, l3_lora_up, l3_k_ka, l3_r_k, l3_gn, l3_w_out)
    layer_params, pos = [], 0
    for i in range(4):
        keys = _LAYER_KEYS[i % 3]
        layer_params.append(dict(zip(keys, args[pos:pos + len(keys)])))
        pos += len(keys)
    return _forward(x, c, ctx, c_ctx, layer_params, tb=256)
```

```python
import functools

import jax
import jax.numpy as jnp
import numpy as np
from jax import lax
from jax.experimental import pallas as pl
from jax.experimental.pallas import tpu as pltpu

F32 = jnp.float32
BF16 = jnp.bfloat16

HEAD_DIM = 64
CHUNK = 64
SCAN_BATCH = 4
GRID_W = 64
WIN_H = 8
WIN_W = 16
ROPE_THETA = 10000.0
RMS_EPS = 1e-6
RW_GN_EPS = 64e-5
LRU_C = 8.0
SUBLANES = 8
LANES = 128
NEG_BIG = -1e30
VMEM_LIMIT = 56 * 1024 * 1024


def _cparams(n_axes):
    return pltpu.CompilerParams(
        dimension_semantics=("arbitrary",) * n_axes, vmem_limit_bytes=VMEM_LIMIT)


def _bf(x):
    return x.astype(BF16)


def _dot(a, b):
    return jnp.dot(_bf(a), _bf(b), preferred_element_type=F32)


def _dot3(a_exact_bf16, b):
    hi = _bf(b)
    r1 = b - hi.astype(F32)
    mid = _bf(r1)
    lo = _bf(r1 - mid.astype(F32))
    d = lambda t: jnp.dot(a_exact_bf16, t, preferred_element_type=F32)
    return d(hi) + d(mid) + d(lo)


def _sigmoid(x):
    return 0.5 * jnp.tanh(0.5 * x) + 0.5


def _silu(x):
    return x * _sigmoid(x)


def _softplus(x):
    return jnp.maximum(x, 0.0) + jnp.log(1.0 + jnp.exp(-jnp.abs(x)))


def _rows(shape):
    return lax.broadcasted_iota(jnp.int32, shape, 0)


def _shift_down(x, s, fill):
    return jnp.where(_rows(x.shape) >= s, pltpu.roll(x, s, 0), fill)


def _shift_up(x, s, fill):
    n = x.shape[0]
    return jnp.where(_rows(x.shape) < n - s, pltpu.roll(x, n - s, 0), fill)


def _norm_mod(xb, g, scale1, shift):
    xf = xb.astype(F32)
    ms = jnp.mean(xf * xf, axis=-1, keepdims=True)
    return xf * lax.rsqrt(ms + RMS_EPS) * g * scale1 + shift


def _ada_kernel(c_ref, w_ref, b_ref, o_ref):
    o_ref[...] = _dot(_silu(c_ref[...]), w_ref[...]) + b_ref[...]


def ada_mod(c_all, ada_w, ada_b):
    m, d = c_all.shape
    n = ada_w.shape[1]
    tn = d
    return pl.pallas_call(
        _ada_kernel,
        grid=(n // tn,),
        in_specs=[pl.BlockSpec((m, d), lambda j: (0, 0)),
                  pl.BlockSpec((d, tn), lambda j: (0, j)),
                  pl.BlockSpec((1, tn), lambda j: (0, j))],
        out_specs=pl.BlockSpec((m, tn), lambda j: (0, j)),
        out_shape=jax.ShapeDtypeStruct((m, n), F32),
        compiler_params=_cparams(1),
        name="ada_mod",
    )(c_all, _bf(ada_w), ada_b.reshape(1, n))


def _seg_of_block(t, n_ctx_blocks):
    return (t >= n_ctx_blocks).astype(jnp.int32)


def _scan_order(c, n_ctx_blocks, n_blocks, reverse):
    if not reverse:
        return c
    return jnp.where(c < n_ctx_blocks, n_ctx_blocks - 1 - c, n_blocks - 1 + n_ctx_blocks - c)


def _in_mm_kernel(x_ref, mod_ref, g_ref, w_ref, *o_refs, splits, acts):
    h = _norm_mod(x_ref[0], g_ref[...], mod_ref[0, 0, 0:1], mod_ref[0, 0, 1:2])
    hb = _bf(h)
    off = 0
    for o_ref, n, act in zip(o_refs, splits, acts):
        z = jnp.dot(hb, w_ref[:, off:off + n], preferred_element_type=F32)
        if act == "silu":
            z = _silu(z)
        o_ref[0] = z.astype(o_ref.dtype)
        off += n


def in_matmul(x, mod, norm_g, w, splits, acts, n_ctx, tb):
    B, Tt, D = x.shape
    ncb = n_ctx // tb
    n = w.shape[1]
    assert sum(splits) == n
    kern = functools.partial(_in_mm_kernel, splits=tuple(splits), acts=tuple(acts))
    return pl.pallas_call(
        kern,
        grid=(B, Tt // tb),
        in_specs=[pl.BlockSpec((1, tb, D), lambda b, t: (b, t, 0)),
                  pl.BlockSpec((1, 1, 2, D), lambda b, t: (b, _seg_of_block(t, ncb), 0, 0)),
                  pl.BlockSpec((1, D), lambda b, t: (0, 0)),
                  pl.BlockSpec((D, n), lambda b, t: (0, 0))],
        out_specs=[pl.BlockSpec((1, tb, s), lambda b, t: (b, t, 0)) for s in splits],
        out_shape=[jax.ShapeDtypeStruct((B, Tt, s), F32) for s in splits],
        compiler_params=_cparams(2),
        name="in_matmul",
    )(x, mod, norm_g.reshape(1, D), _bf(w))


def _out_mm_kernel(*refs, n_a):
    a_refs = refs[:n_a]
    w_ref, x_ref, gate_ref, o_ref = refs[n_a:]
    a = a_refs[0][0]
    for r in a_refs[1:]:
        a = a * r[0]
    o_ref[0] = x_ref[0] + gate_ref[0, 0] * _dot(a, w_ref[...])


def out_matmul(a_list, w, x, gate, n_ctx, tb, latent_only=False):
    B, Tt, D = x.shape
    K = w.shape[0]
    ncb = n_ctx // tb
    skip = ncb if latent_only else 0
    kern = functools.partial(_out_mm_kernel, n_a=len(a_list))
    return pl.pallas_call(
        kern,
        grid=(B, Tt // tb - skip),
        in_specs=[pl.BlockSpec((1, tb, K), lambda b, t: (b, t + skip, 0)) for _ in a_list] + [
            pl.BlockSpec((K, D), lambda b, t: (0, 0)),
            pl.BlockSpec((1, tb, D), lambda b, t: (b, t + skip, 0)),
            pl.BlockSpec((1, 1, 1, D), lambda b, t: (b, _seg_of_block(t + skip, ncb), 0, 0))],
        out_specs=pl.BlockSpec((1, tb, D), lambda b, t: (b, t, 0)),
        out_shape=jax.ShapeDtypeStruct((B, Tt - skip * tb, D), F32),
        compiler_params=_cparams(2),
        name="out_matmul",
    )(*a_list, _bf(w), x, gate)


_V_NORM_G, _V_MU, _V_KK, _V_KA, _V_B0, _V_RK = 0, 1, 7, 8, 9, 13
_V_ROWS = 16


def _head_sum(x, sel_ref, selt_ref):
    mm = lambda a, w: jnp.dot(a, w, preferred_element_type=F32)
    hi = _bf(x)
    lo = _bf(x - hi.astype(F32))
    s = mm(jnp.concatenate([hi, lo], axis=1), sel_ref[...])
    s_hi = _bf(s)
    s_lo = _bf(s - s_hi.astype(F32))
    return mm(jnp.concatenate([s_hi, s_lo], axis=1), selt_ref[...])


def _head_selectors(D):
    hid = jnp.arange(D) // HEAD_DIM
    sel = (hid[:, None] == jnp.arange(LANES)[None, :]).astype(BF16)
    return jnp.concatenate([sel, sel], axis=0), jnp.concatenate([sel.T, sel.T], axis=0)


def _rwkv_proj_kernel(x_ref, xp_ref, xn_ref, mod_ref, vec_ref, win_ref, dw_ref, da_ref, upw_ref,
                      upa_ref, sel_ref, selt_ref, r_ref, v_ref, g_ref, kk_ref, bonus_ref, lw0_ref,
                      a0_ref, k0_ref, lw1_ref, a1_ref, k1_ref, *, n_ctx_blocks):
    t = pl.program_id(1)
    nb = pl.num_programs(1)
    vec = lambda i: vec_ref[i:i + 1, :]
    scale1, shift = mod_ref[0, 0, 0:1], mod_ref[0, 0, 1:2]
    g = vec(_V_NORM_G)
    h = _norm_mod(x_ref[0], g, scale1, shift)
    first = jnp.logical_or(t == 0, t == n_ctx_blocks)
    last = jnp.logical_or(t == n_ctx_blocks - 1, t == nb - 1)
    hp = _norm_mod(xp_ref[0], g, scale1, shift)[SUBLANES - 1:SUBLANES]
    hn = _norm_mod(xn_ref[0], g, scale1, shift)[0:1]
    hp = jnp.where(first, 0.0, hp)
    hn = jnp.where(last, 0.0, hn)
    h_prev = _shift_down(h, 1, jnp.broadcast_to(hp, h.shape))
    h_next = _shift_up(h, 1, jnp.broadcast_to(hn, h.shape))
    xx = 0.5 * (h_prev + h_next) - h
    mix = lambda j: _bf(h + xx * vec(_V_MU + j))
    mm = lambda a, w: jnp.dot(a, w, preferred_element_type=F32)
    xr, xw, xk, xv, xa, xg = [mix(j) for j in range(6)]
    r = mm(xr, win_ref[0])
    k = mm(xk, win_ref[1])
    v = mm(xv, win_ref[2])
    r_ref[0] = r
    v_ref[0] = v
    g_ref[0] = _silu(mm(xg, win_ref[3]))
    kk_raw = k * vec(_V_KK)
    kk_ref[0] = kk_raw * jnp.minimum(lax.rsqrt(_head_sum(kk_raw * kk_raw, sel_ref, selt_ref)), 1e12)
    tw = _bf(jnp.tanh(mm(xw, dw_ref[...])))
    ta = _bf(mm(xa, da_ref[...]))
    k_a = vec(_V_KA)
    k_sum = None
    for d, (lw_ref, a_ref, kd_ref) in enumerate(((lw0_ref, a0_ref, k0_ref), (lw1_ref, a1_ref, k1_ref))):
        zw = vec(_V_B0 + 2 * d) + mm(tw, upw_ref[d])
        w_log = -_softplus(-zw) - 0.5
        lw_ref[0] = -jnp.exp(w_log)
        a_d = _sigmoid(vec(_V_B0 + 2 * d + 1) + mm(ta, upa_ref[d]))
        a_ref[0] = a_d
        k_d = k * (1.0 + (a_d - 1.0) * k_a)
        kd_ref[0] = k_d
        k_sum = k_d if k_sum is None else k_sum + k_d
    bonus_ref[0] = _head_sum(r * k_sum * vec(_V_RK), sel_ref, selt_ref) * v


def rwkv_proj(x, mod, p, n_ctx, tb):
    B, Tt, D = x.shape
    lora = p['lora_down'].shape[-1]
    ncb = n_ctx // tb
    nb8 = Tt // SUBLANES
    r8 = tb // SUBLANES
    vec = jnp.zeros((_V_ROWS, D), F32)
    vec = vec.at[_V_NORM_G].set(p['norm_g']).at[_V_MU:_V_MU + 6].set(p['mu'])
    vec = vec.at[_V_KK].set(p['k_ka'][0]).at[_V_KA].set(p['k_ka'][1])
    vec = vec.at[_V_B0:_V_B0 + 4].set(p['lora_b0'].reshape(4, D)).at[_V_RK].set(p['r_k'].reshape(D))
    sel, selt = _head_selectors(D)
    down, up = p['lora_down'], p['lora_up']
    dw = _bf(jnp.concatenate([down[0, 0], down[1, 0]], axis=1))
    da = _bf(jnp.concatenate([down[0, 1], down[1, 1]], axis=1))
    z = jnp.zeros((lora, D), F32)
    upw = _bf(jnp.stack([jnp.concatenate([up[0, 0], z], 0), jnp.concatenate([z, up[1, 0]], 0)]))
    upa = _bf(jnp.stack([jnp.concatenate([up[0, 1], z], 0), jnp.concatenate([z, up[1, 1]], 0)]))
    full = lambda shape: pl.BlockSpec(shape, lambda b, t: (0,) * len(shape))
    tok = pl.BlockSpec((1, tb, D), lambda b, t: (b, t, 0))
    kern = functools.partial(_rwkv_proj_kernel, n_ctx_blocks=ncb)
    outs = pl.pallas_call(
        kern,
        grid=(B, Tt // tb),
        in_specs=[tok,
                  pl.BlockSpec((1, SUBLANES, D), lambda b, t: (b, jnp.maximum(t * r8 - 1, 0), 0)),
                  pl.BlockSpec((1, SUBLANES, D), lambda b, t: (b, jnp.minimum((t + 1) * r8, nb8 - 1), 0)),
                  pl.BlockSpec((1, 1, 2, D), lambda b, t: (b, _seg_of_block(t, ncb), 0, 0)),
                  full((_V_ROWS, D)), full((4, D, D)), full((D, 2 * lora)), full((D, 2 * lora)),
                  full((2, 2 * lora, D)), full((2, 2 * lora, D)), full((2 * D, LANES)), full((2 * LANES, D))],
        out_specs=[tok] * 11,
        out_shape=[jax.ShapeDtypeStruct((B, Tt, D), F32)] * 11,
        compiler_params=_cparams(2),
        name="rwkv_proj",
    )(x, x, x, mod, vec, _bf(p['w_in']), dw, da, upw, upa, sel, selt)
    names = ('r', 'v', 'gate', 'kk', 'bonus', 'lw0', 'a0', 'k0', 'lw1', 'a1', 'k1')
    return dict(zip(names, outs))


def _level_masks(L, reverse):
    ri = lax.broadcasted_iota(jnp.int32, (L, 2 * L), 0)
    ci = lax.broadcasted_iota(jnp.int32, (L, 2 * L), 1) & (L - 1)
    if reverse:
        ri, ci = ci, ri
    masks = []
    for j in range(int(np.log2(L))):
        same = (ri >> (j + 1)) == (ci >> (j + 1))
        masks.append(same & (((ri >> j) & 1) == 1) & (((ci >> j) & 1) == 0))
    return masks


def _rwkv_chunk_pairs(rt, kap, bt, kt, v, wl, st, tri_strict, tri_incl, eye, masks, first):
    ps = range(len(rt))
    L = v[0].shape[0]
    mm = lambda a, b: jnp.dot(a, b, preferred_element_type=F32)
    mm_nt = lambda a, b: lax.dot_general(a, b, (((1,), (1,)), ((), ())), preferred_element_type=F32)
    mm_tn = lambda a, b: lax.dot_general(a, b, (((0,), (0,)), ((), ())), preferred_element_type=F32)

    def bd(x):
        z = jnp.zeros_like(x)
        return jnp.concatenate([jnp.where(first, x, z), jnp.where(first, z, x)], axis=0)

    def diag_blocks(x):
        return jnp.where(first, x[:L], x[L:])

    v_b = [_bf(v[p]) for p in ps]
    kap_b = [_bf(kap[p]) for p in ps]
    v_bd = [bd(v_b[p]) for p in ps]
    lhs = [jnp.concatenate([kap_b[p], _bf(rt[p])], axis=0) for p in ps]
    sb = [mm_nt(lhs[p], bd(_bf(bt[p]))) for p in ps]
    sk = [mm_nt(lhs[p], bd(_bf(kt[p]))) for p in ps]
    n_b = [jnp.where(tri_strict, sb[p][:L], 0.0) for p in ps]
    a_rb = [_bf(jnp.where(tri_incl, sb[p][L:], 0.0)) for p in ps]
    n_k = [_bf(jnp.where(tri_strict, sk[p][:L], 0.0)) for p in ps]
    a_rk = [_bf(jnp.where(tri_incl, sk[p][L:], 0.0)) for p in ps]
    t = [jnp.where(eye, 1.0, 0.0) - jnp.where(masks[0], n_b[p], 0.0) for p in ps]
    for m in masks[1:]:
        t_b = [_bf(t[p]) for p in ps]
        ct = [_bf(mm(_bf(jnp.where(m, n_b[p], 0.0)), bd(t_b[p]))) for p in ps]
        t = [t[p] - mm(t_b[p], bd(ct[p])) for p in ps]
    t_b = [_bf(t[p]) for p in ps]
    nkv = [_bf(mm(n_k[p], v_bd[p])) for p in ps]
    e0 = [_bf(-mm(t_b[p], bd(nkv[p]))) for p in ps]
    g = [_bf(-mm(t_b[p], bd(kap_b[p]))) for p in ps]
    y0 = [mm(jnp.concatenate([a_rb[p], a_rk[p]], axis=1), jnp.concatenate([bd(e0[p]), v_bd[p]], axis=0))
          for p in ps]
    qp = [rt[p] + mm(a_rb[p], bd(g[p])) for p in ps]
    btw = [_bf(bt[p] * wl[p]) for p in ps]
    ktw = [_bf(kt[p] * wl[p]) for p in ps]
    ut = [diag_blocks(mm_tn(jnp.concatenate([btw[p], ktw[p]], axis=0),
                            jnp.concatenate([e0[p], v_b[p]], axis=0))) for p in ps]
    pt = [jnp.where(eye, wl[p], 0.0) + diag_blocks(mm_tn(btw[p], g[p])) for p in ps]
    ys = [mm(_bf(jnp.concatenate([qp[p], pt[p]], axis=0)), bd(_bf(st[p]))) for p in ps]
    return [ys[p][:L] + y0[p] for p in ps], [ys[p][L:] + ut[p] for p in ps]


def _rwkv_scan_kernel(*refs, reverse, finish, n_heads, bb):
    if finish:
        (r_ref, lw_ref, k_ref, kk_ref, a_ref, v_ref, yf_ref, bonus_ref, gate_ref, vec_ref,
         o_ref, st_ref) = refs
    else:
        r_ref, lw_ref, k_ref, kk_ref, a_ref, v_ref, o_ref, st_ref = refs

    @pl.when(pl.program_id(1) == 0)
    def _():
        st_ref[...] = jnp.zeros_like(st_ref)

    L = r_ref.shape[1]
    hd = HEAD_DIM
    assert L == hd and 2 * hd == LANES
    n_pairs = n_heads // 2
    ri = lax.broadcasted_iota(jnp.int32, (L, L), 0)
    ci = lax.broadcasted_iota(jnp.int32, (L, L), 1)
    tri_ones = jnp.where((ci >= ri) if reverse else (ci <= ri), 1.0, 0.0).astype(BF16)
    ri2 = lax.broadcasted_iota(jnp.int32, (L, LANES), 0)
    lane = lax.broadcasted_iota(jnp.int32, (L, LANES), 1)
    ci2 = lane & (L - 1)
    if reverse:
        tri_incl, tri_strict = ci2 >= ri2, ci2 > ri2
    else:
        tri_incl, tri_strict = ci2 <= ri2, ci2 < ri2
    eye = ri2 == ci2
    first = lane < hd
    masks = _level_masks(L, reverse)
    end = 0 if reverse else L - 1

    sls = [slice(p * LANES, (p + 1) * LANES) for p in range(n_pairs)]
    split = lambda x: [x[:, sl] for sl in sls]
    rt, kap, bt, kt, vs, wl, st = [], [], [], [], [], [], []
    for i in range(bb):
        lw = lw_ref[i]
        cum = _dot3(tri_ones, lw)
        w_t = jnp.exp(cum)
        w_i = jnp.exp(-cum)
        w_p = jnp.exp(cum - lw)
        kk = kk_ref[i]
        rt += split(r_ref[i] * w_t)
        kap += split(kk * w_p)
        bt += split(kk * a_ref[i] * w_i)
        kt += split(k_ref[i] * w_i)
        vs += split(v_ref[i])
        wl += split(w_t[end:end + 1])
        st += split(st_ref[i])
    outs, states = _rwkv_chunk_pairs(rt, kap, bt, kt, vs, wl, st, tri_strict, tri_incl, eye, masks, first)
    lrow = lax.broadcasted_iota(jnp.int32, (LANES, LANES), 0)
    lcol = lax.broadcasted_iota(jnp.int32, (LANES, LANES), 1)
    ones = jnp.where((lrow < hd) == (lcol < hd), 1.0, 0.0).astype(BF16)
    lane_sum = lambda x: jnp.dot(x, ones, preferred_element_type=F32)
    for i in range(bb):
        mine = slice(i * n_pairs, (i + 1) * n_pairs)
        st_ref[i] = jnp.concatenate(states[mine], axis=1)
        out = jnp.concatenate(outs[mine], axis=1)
        if finish:
            ys = jnp.concatenate(split(out + yf_ref[i]), axis=0)
            ys_hi = _bf(ys)
            mean = (lane_sum(ys_hi) + lane_sum(_bf(ys - ys_hi.astype(F32)))) * (1.0 / hd)
            cen = ys - mean
            c2 = cen * cen
            c2_hi = _bf(c2)
            var = (lane_sum(c2_hi) + lane_sum(_bf(c2 - c2_hi.astype(F32)))) * (1.0 / hd)
            yn = cen * lax.rsqrt(var + RW_GN_EPS)
            out = jnp.concatenate([yn[p * L:(p + 1) * L] for p in range(n_pairs)], axis=1)
            out = (out * vec_ref[0:1] + vec_ref[1:2] + bonus_ref[i]) * gate_ref[i]
        o_ref[i] = out.astype(o_ref.dtype)


def rwkv_scan(pr, d, n_ctx, reverse, finish=None):
    r = pr['r']
    B, Tt, D = r.shape
    L = CHUNK
    nc, ncc = Tt // L, n_ctx // L
    bb = next(n for n in (SCAN_BATCH, 2, 1) if B % n == 0)
    tok = pl.BlockSpec((bb, L, D), lambda b, c: (b, _scan_order(c, ncc, nc, reverse), 0))
    ins = [r, pr['lw%d' % d], pr['k%d' % d], pr['kk'], pr['a%d' % d], pr['v']]
    specs = [tok] * 6
    if finish is not None:
        y_fwd, gn = finish
        vec = jnp.zeros((SUBLANES, D), F32).at[0].set(gn[0]).at[1].set(gn[1])
        ins += [y_fwd, pr['bonus'], pr['gate'], vec]
        specs += [tok, tok, tok, pl.BlockSpec((SUBLANES, D), lambda b, c: (0, 0))]
    kern = functools.partial(_rwkv_scan_kernel, reverse=reverse, finish=finish is not None,
                             n_heads=D // HEAD_DIM, bb=bb)
    return pl.pallas_call(
        kern,
        grid=(B // bb, nc),
        in_specs=specs,
        out_specs=tok,
        out_shape=jax.ShapeDtypeStruct((B, Tt, D), F32 if finish is None else BF16),
        scratch_shapes=[pltpu.VMEM((bb, HEAD_DIM, D), F32)],
        compiler_params=_cparams(2),
        name="rwkv_scan_rev" if reverse else "rwkv_scan_fwd",
    )(*ins)


def rwkv_layer(x, mod, gate, p, n_ctx, tb, latent_only=False):
    pr = rwkv_proj(x, mod, p, n_ctx, tb)
    y_fwd = rwkv_scan(pr, 0, n_ctx, reverse=False)
    z = rwkv_scan(pr, 1, n_ctx, reverse=True, finish=(y_fwd, p['gn']))
    return out_matmul([z], p['w_out'], x, gate, n_ctx, tb, latent_only)


GATE_COLS = 256


def _gate_tiles(W, bd):
    assert bd <= LANES and W % LANES == 0
    win = min(W, GATE_COLS + 2 * LANES)
    tiles = []
    for c0 in range(0, W, GATE_COLS):
        lo = min(max(c0 - LANES, 0), W - win)
        tiles.append((c0, min(GATE_COLS, W - c0), lo))
    return win, tiles


def _rglru_scan_kernel(*refs, reverse, finish, n_ctx_blocks, block_dim):
    if finish:
        (x_ref, xp_ref, xn_ref, cw_ref, vec_ref, wg_ref, hf_ref, gs_ref, o_ref,
         a_scr, b_scr, carry_ref) = refs
    else:
        x_ref, xp_ref, xn_ref, cw_ref, vec_ref, wg_ref, o_ref, a_scr, b_scr, carry_ref = refs
    c = pl.program_id(1)
    nb = pl.num_programs(1)
    t = _scan_order(c, n_ctx_blocks, nb, reverse)

    @pl.when(c == 0)
    def _():
        carry_ref[...] = jnp.zeros_like(carry_ref)

    x = x_ref[0]
    tb, W = x.shape
    first = jnp.logical_or(t == 0, t == n_ctx_blocks)
    last = jnp.logical_or(t == n_ctx_blocks - 1, t == nb - 1)
    xp = jnp.where(first, 0.0, xp_ref[0])
    xn = jnp.where(last, 0.0, xn_ref[0])
    row = lambda a, i: jnp.broadcast_to(a[i:i + 1], x.shape)
    rows = _rows(x.shape)
    x_m1 = _shift_down(x, 1, row(xp, SUBLANES - 1))
    x_m2 = _shift_down(x, 2, jnp.where(rows == 0, row(xp, SUBLANES - 2), row(xp, SUBLANES - 1)))
    x_p1 = _shift_up(x, 1, row(xn, 0))
    cw = lambda i: cw_ref[i:i + 1, :]
    xc = cw(4) + x_m2 * cw(0) + x_m1 * cw(1) + x * cw(2) + x_p1 * cw(3)
    win, tiles = _gate_tiles(W, block_dim)
    xc_b = _bf(xc)
    z = [jnp.dot(xc_b[:, lo:lo + win], wg_ref[j], preferred_element_type=F32)
         for j, (_, _, lo) in enumerate(tiles)]
    gate_r = jnp.concatenate([z[j][:, :cw_] for j, (_, cw_, _) in enumerate(tiles)], axis=1)
    gate_i = jnp.concatenate([z[j][:, GATE_COLS:GATE_COLS + cw_] for j, (_, cw_, _) in enumerate(tiles)],
                             axis=1)
    r = _sigmoid(gate_r + vec_ref[0:1])
    i = _sigmoid(gate_i + vec_ref[1:2])
    log_a = -LRU_C * r * _softplus(-vec_ref[2:3])
    a = jnp.exp(log_a)
    a_scr[...] = a
    b_scr[...] = jnp.sqrt(1.0 - a * a) * (i * xc)

    S = SUBLANES
    ng = tb // S
    sub = _rows((S, W))

    def group(gi, carry):
        g = (ng - 1 - gi) if reverse else gi
        a = a_scr[pl.ds(pl.multiple_of(g * S, S), S), :]
        b = b_scr[pl.ds(pl.multiple_of(g * S, S), S), :]
        for s in (1, 2, 4):
            if reverse:
                ok = sub < S - s
                a_s = jnp.where(ok, pltpu.roll(a, S - s, 0), 1.0)
                b_s = jnp.where(ok, pltpu.roll(b, S - s, 0), 0.0)
            else:
                ok = sub >= s
                a_s = jnp.where(ok, pltpu.roll(a, s, 0), 1.0)
                b_s = jnp.where(ok, pltpu.roll(b, s, 0), 0.0)
            b = a * b_s + b
            a = a * a_s
        h = b + a * carry
        b_scr[pl.ds(pl.multiple_of(g * S, S), S), :] = h
        e = 0 if reverse else S - 1
        return jnp.broadcast_to(h[e:e + 1], (S, W))

    carry_ref[...] = lax.fori_loop(0, ng, group, carry_ref[...])
    h_all = b_scr[...]
    if finish:
        h_all = (h_all + hf_ref[0]) * gs_ref[0]
    o_ref[0] = h_all.astype(o_ref.dtype)


def rglru_scan(xr, p, d, n_ctx, tb, reverse, finish=None):
    B, Tt, W = xr.shape
    nblk = p['gate_w'].shape[2]
    bd = W // nblk
    ncb = n_ctx // tb
    nb = Tt // tb
    nb8 = Tt // SUBLANES
    r8 = tb // SUBLANES

    def dense(wb):
        eye = jnp.eye(nblk, dtype=wb.dtype)
        return jnp.einsum('ncd,nm->ncmd', wb, eye).reshape(W, W)

    win, tiles = _gate_tiles(W, bd)
    w_r, w_i = dense(p['gate_w'][d, 0]), dense(p['gate_w'][d, 1])
    pad = lambda m: jnp.pad(m, ((0, 0), (0, GATE_COLS - m.shape[1])))
    wg = _bf(jnp.stack([jnp.concatenate([pad(w_r[lo:lo + win, c0:c0 + cw_]), pad(w_i[lo:lo + win, c0:c0 + cw_])],
                                        axis=1) for c0, cw_, lo in tiles]))
    cw = jnp.zeros((SUBLANES, W), F32).at[0:4].set(p['conv_w']).at[4].set(p['conv_b'])
    vec = jnp.zeros((SUBLANES, W), F32).at[0:2].set(p['gate_b'][d]).at[2].set(p['lam'][d])
    order = lambda c: _scan_order(c, ncb, nb, reverse)
    tok = pl.BlockSpec((1, tb, W), lambda b, c: (b, order(c), 0))
    full = lambda shape: pl.BlockSpec(shape, lambda b, c: (0,) * len(shape))
    ins = [xr, xr, xr, cw, vec, wg]
    specs = [tok,
             pl.BlockSpec((1, SUBLANES, W), lambda b, c: (b, jnp.maximum(order(c) * r8 - 1, 0), 0)),
             pl.BlockSpec((1, SUBLANES, W), lambda b, c: (b, jnp.minimum((order(c) + 1) * r8, nb8 - 1), 0)),
             full((SUBLANES, W)), full((SUBLANES, W)), full((len(tiles), win, 2 * GATE_COLS))]
    if finish is not None:
        ins += list(finish)
        specs += [tok, tok]
    kern = functools.partial(_rglru_scan_kernel, reverse=reverse, finish=finish is not None,
                             n_ctx_blocks=ncb, block_dim=bd)
    return pl.pallas_call(
        kern,
        grid=(B, nb),
        in_specs=specs,
        out_specs=tok,
        out_shape=jax.ShapeDtypeStruct((B, Tt, W), F32 if finish is None else BF16),
        scratch_shapes=[pltpu.VMEM((tb, W), F32), pltpu.VMEM((tb, W), F32), pltpu.VMEM((SUBLANES, W), F32)],
        compiler_params=_cparams(2),
        name="rglru_scan_rev" if reverse else "rglru_scan_fwd",
    )(*ins)


def rglru_layer(x, mod, gate, p, n_ctx, tb, latent_only=False):
    W = p['conv_w'].shape[1]
    xr, gs = in_matmul(x, mod, p['norm_g'], p['w_in'], (W, W), (None, "silu"), n_ctx, tb)
    h_fwd = rglru_scan(xr, p, 0, n_ctx, tb, reverse=False)
    z = rglru_scan(xr, p, 1, n_ctx, tb, reverse=True, finish=(h_fwd, gs))
    return out_matmul([z], p['w_out'], x, gate, n_ctx, tb, latent_only)


def _natten_in_kernel(x_ref, mod_ref, ng_ref, w_ref, cs_ref, g_ref, sel_ref, selt_ref,
                      qn_ref, qr_ref, kr_ref, vb_ref, gs_ref):
    D = x_ref.shape[2]
    hb = _bf(_norm_mod(x_ref[0], ng_ref[...], mod_ref[0, 0, 0:1], mod_ref[0, 0, 1:2]))
    proj = lambda j: jnp.dot(hb, w_ref[:, j * D:(j + 1) * D], preferred_element_type=F32)
    reps = D // cs_ref.shape[2]
    cos = jnp.concatenate([cs_ref[0]] * reps, axis=1)
    sin = jnp.concatenate([cs_ref[1]] * reps, axis=1)
    lane = lax.broadcasted_iota(jnp.int32, cos.shape, 1)
    quarter = HEAD_DIM // 4
    low = (lane % (2 * quarter)) < quarter

    def prep(x, g):
        ms = _head_sum(x * x, sel_ref, selt_ref) * (1.0 / HEAD_DIM)
        xn = x * lax.rsqrt(ms + RMS_EPS) * g
        partner = jnp.where(low, pltpu.roll(xn, D - quarter, 1), pltpu.roll(xn, quarter, 1))
        return xn, xn * cos + partner * sin

    qn, qr = prep(proj(0), g_ref[0:1])
    _, kr = prep(proj(1), g_ref[1:2])
    scale = HEAD_DIM ** -0.5
    qn_ref[0] = _bf(qn * scale)
    qr_ref[0] = _bf(qr * scale)
    kr_ref[0] = _bf(kr)
    vb_ref[0] = _bf(proj(2))
    gs_ref[0] = _silu(proj(3))


def natten_in(x, mod, norm_g, w_in, qk_g, n_ctx, tb):
    B, Tt, D = x.shape
    H = D // HEAD_DIM
    T = Tt - n_ctx
    ncb = n_ctx // tb
    nfreq = HEAD_DIM // 4
    pos = jnp.arange(T)
    inv = ROPE_THETA ** (-jnp.arange(nfreq, dtype=F32) / nfreq)
    ang_r = (pos // GRID_W).astype(F32)[:, None] * inv
    ang_c = (pos % GRID_W).astype(F32)[:, None] * inv
    cos = jnp.concatenate([jnp.cos(ang_r)] * 2 + [jnp.cos(ang_c)] * 2, axis=1)
    sin = jnp.concatenate([-jnp.sin(ang_r), jnp.sin(ang_r), -jnp.sin(ang_c), jnp.sin(ang_c)], axis=1)
    cs = jnp.stack([jnp.concatenate([jnp.ones((n_ctx, HEAD_DIM), F32), cos], 0),
                    jnp.concatenate([jnp.zeros((n_ctx, HEAD_DIM), F32), sin], 0)])
    cs = jnp.concatenate([cs, cs], axis=2)
    g = jnp.zeros((SUBLANES, D), F32).at[0].set(jnp.tile(qk_g[0], H)).at[1].set(jnp.tile(qk_g[1], H))
    sel, selt = _head_selectors(D)
    tok = pl.BlockSpec((1, tb, D), lambda b, t: (b, t, 0))
    full = lambda shape: pl.BlockSpec(shape, lambda b, t: (0,) * len(shape))
    return pl.pallas_call(
        _natten_in_kernel,
        grid=(B, Tt // tb),
        in_specs=[tok,
                  pl.BlockSpec((1, 1, 2, D), lambda b, t: (b, _seg_of_block(t, ncb), 0, 0)),
                  full((1, D)), full((D, 4 * D)),
                  pl.BlockSpec((2, tb, 2 * HEAD_DIM), lambda b, t: (0, t, 0)),
                  full((SUBLANES, D)), full((2 * D, LANES)), full((2 * LANES, D))],
        out_specs=[tok] * 5,
        out_shape=[jax.ShapeDtypeStruct((B, Tt, D), BF16)] * 4 + [jax.ShapeDtypeStruct((B, Tt, D), F32)],
        compiler_params=_cparams(2),
        name="natten_in",
    )(x, mod, norm_g.reshape(1, D), _bf(w_in), cs, g, sel, selt)


def _natten_kernel(qr_ref, qn_ref, kr_ref, v_ref, bias_ref, o_ref, *, rows, kh, rb, n_ctx):
    gw = GRID_W
    lane = lax.broadcasted_iota(jnp.int32, (gw, 2 * HEAD_DIM), 1)
    head_lanes = [lane < HEAD_DIM, lane >= HEAD_DIM]
    kc = kr_ref[0, 0:n_ctx, :]
    vc = v_ref[0, 0:n_ctx, :]
    nt = lambda a, b: lax.dot_general(a, b, (((1,), (1,)), ((), ())), preferred_element_type=F32)
    mm = lambda a, b: jnp.dot(a, b, preferred_element_type=F32)

    q_c = qn_ref[0, 0:n_ctx, :]
    lane_c = lax.broadcasted_iota(jnp.int32, q_c.shape, 1)
    o_c = []
    for h in range(2):
        mine = (lane_c >= HEAD_DIM) if h else (lane_c < HEAD_DIM)
        s = nt(jnp.where(mine, q_c, jnp.zeros_like(q_c)), kc)
        p = jnp.exp(s - s.max(axis=-1, keepdims=True))
        o_c.append(mm(_bf(p), vc) / p.sum(axis=-1, keepdims=True))
    o_ref[0, 0:n_ctx, :] = jnp.where(lane_c < HEAD_DIM, o_c[0], o_c[1])

    def row_group(g, carry):
        q0s, qrs, qns, kbs, vbs, biases = [], [], [], [], [], []
        for j in range(rb):
            r = g * rb + j
            start = jnp.clip(r - kh // 2, 0, rows - kh)
            d0 = start - r + kh - 1
            q0 = pl.multiple_of(n_ctx + r * gw, gw)
            k0 = pl.multiple_of(n_ctx + start * gw, gw)
            qr = qr_ref[0, pl.ds(q0, gw), :]
            qn = qn_ref[0, pl.ds(q0, gw), :]
            zero = jnp.zeros_like(qr)
            by_head = lambda q: jnp.concatenate([jnp.where(m, q, zero) for m in head_lanes], axis=0)
            q0s.append(q0)
            qrs.append(by_head(qr))
            qns.append(by_head(qn))
            kbs.append(kr_ref[0, pl.ds(k0, kh * gw), :])
            vbs.append(v_ref[0, pl.ds(k0, kh * gw), :])
            biases.append(jnp.concatenate([bias_ref[0, d0], bias_ref[1, d0]], axis=0))
        n = range(rb)
        s_band = [nt(qrs[i], kbs[i]) + biases[i] for i in n]
        s_ctx = [nt(qns[i], kc) for i in n]
        m = [jnp.maximum(s_band[i].max(axis=-1, keepdims=True), s_ctx[i].max(axis=-1, keepdims=True))
             for i in n]
        p_band = [jnp.exp(s_band[i] - m[i]) for i in n]
        p_ctx = [jnp.exp(s_ctx[i] - m[i]) for i in n]
        den = [p_band[i].sum(axis=-1, keepdims=True) + p_ctx[i].sum(axis=-1, keepdims=True) for i in n]
        o = [(mm(_bf(p_band[i]), vbs[i]) + mm(_bf(p_ctx[i]), vc)) / den[i] for i in n]
        for i in n:
            o_ref[0, pl.ds(q0s[i], gw), :] = jnp.where(head_lanes[0], o[i][:gw], o[i][gw:])
        return carry

    lax.fori_loop(0, rows // rb, row_group, 0)


def _natten_bias(rpb, rows, kh):
    cols = np.arange(GRID_W)
    c_start = np.clip(cols - WIN_W // 2, 0, GRID_W - WIN_W)
    col_ok = (cols[None, :] >= c_start[:, None]) & (cols[None, :] < c_start[:, None] + WIN_W)
    dc_idx = np.clip(cols[None, :] - cols[:, None] + WIN_W - 1, 0, 2 * WIN_W - 2)
    by_dr = jnp.where(jnp.asarray(col_ok), rpb[:, :, dc_idx].astype(F32), NEG_BIG)
    base = WIN_H - kh
    per_v = [jnp.concatenate([by_dr[:, base + v + i] for i in range(kh)], axis=-1) for v in range(kh)]
    return jnp.stack(per_v, axis=1)


def natten_attention(qr, qn, kr, vb, rpb, n_ctx):
    B, Tt, D = qr.shape
    T = Tt - n_ctx
    rows = T // GRID_W
    kh = min(WIN_H, rows)
    HP = D // (2 * HEAD_DIM)
    assert n_ctx % GRID_W == 0
    bias = _natten_bias(rpb, rows, kh)
    seq = pl.BlockSpec((1, Tt, 2 * HEAD_DIM), lambda hp, b: (b, 0, hp))
    rb = 4 if rows % 4 == 0 else 1
    kern = functools.partial(_natten_kernel, rows=rows, kh=kh, rb=rb, n_ctx=n_ctx)
    return pl.pallas_call(
        kern,
        grid=(HP, B),
        in_specs=[seq, seq, seq, seq,
                  pl.BlockSpec((2, kh, GRID_W, kh * GRID_W), lambda hp, b: (hp, 0, 0, 0))],
        out_specs=seq,
        out_shape=jax.ShapeDtypeStruct((B, Tt, D), F32),
        compiler_params=_cparams(2),
        name="natten",
    )(qr, qn, kr, vb, bias)


def natten_layer(x, mod, gate, p, n_ctx, tb, latent_only=False):
    qn, qr, kr, vb, gs = natten_in(x, mod, p['norm_g'], p['w_in'], p['qk_g'], n_ctx, tb)
    o = natten_attention(qr, qn, kr, vb, p['rpb'], n_ctx)
    return out_matmul([o, gs], p['w_out'], x, gate, n_ctx, tb, latent_only)


_LAYER_KEYS = (
    ('norm_g', 'ada_w', 'ada_b', 'w_in', 'mu', 'lora_b0', 'lora_down', 'lora_up', 'k_ka', 'r_k', 'gn', 'w_out'),
    ('norm_g', 'ada_w', 'ada_b', 'w_in', 'conv_w', 'conv_b', 'gate_w', 'gate_b', 'lam', 'w_out'),
    ('norm_g', 'ada_w', 'ada_b', 'w_in', 'qk_g', 'rpb', 'w_out'),
)
_LAYERS = (rwkv_layer, rglru_layer, natten_layer)


def _forward(x, c, ctx, c_ctx, layer_params, tb):
    B, T, D = x.shape
    n_ctx = ctx.shape[1]
    xs = jnp.concatenate([ctx, x], axis=1)
    m_rows = -(-(B + 1) // SUBLANES) * SUBLANES
    c_all = jnp.zeros((m_rows, D), F32).at[:B].set(c).at[B].set(c_ctx)
    for i, p in enumerate(layer_params):
        m = ada_mod(c_all, p['ada_w'], p['ada_b'])
        m_l = m[:B].reshape(B, 3, D)
        m_c = jnp.broadcast_to(m[B].reshape(1, 3, D), (B, 3, D))
        both = jnp.stack([m_c, m_l], axis=1)
        mod = jnp.stack([1.0 + both[:, :, 1], both[:, :, 0]], axis=2)
        gate = both[:, :, 2:3]
        xs = _LAYERS[i % 3](xs, mod, gate, p, n_ctx, tb, latent_only=i == len(layer_params) - 1)
    return xs


def kernel(x, c, ctx, c_ctx, l0_norm_g, l0_ada_w, l0_ada_b, l0_w_in, l0_mu, l0_lora_b0, l0_lora_down, l0_lora_up, l0_k_ka, l0_r_k, l0_gn, l0_w_out, l1_norm_g, l1_ada_w, l1_ada_b, l1_w_in, l1_conv_w, l1_conv_b, l1_gate_w, l1_gate_b, l1_lam, l1_w_out, l2_norm_g, l2_ada_w, l2_ada_b, l2_w_in, l2_qk_g, l2_rpb, l2_w_out, l3_norm_g, l3_ada_w, l3_ada_b, l3_w_in, l3_mu, l3_lora_b0, l3_lora_down, l3_lora_up, l3_k_ka, l3_r_k, l3_gn, l3_w_out):
    args = (l0_norm_g, l0_ada_w, l0_ada_b, l0_w_in, l0_mu, l0_lora_b0, l0_lora_down, l0_lora_up, l0_k_ka, l0_r_k, l0_gn, l0_w_out, l1_norm_g, l1_ada_w, l1_ada_b, l1_w_in, l1_conv_w, l1_conv_b, l1_gate_w, l1_gate_b, l1_lam, l1_w_out, l2_norm_g, l2_ada_w, l2_ada_b, l2_w_in, l2_qk_g, l2_rpb, l2_w_out, l3_norm_g, l3_ada_w, l3_ada_b, l3_w_in, l3_mu, l3_lora_b0, l3_lora_down, l3_lora_up, l3_k_ka, l3_r_k, l3_gn, l3_w_out)
    layer_params, pos = [], 0
    for i in range(4):
        keys = _LAYER_KEYS[i % 3]
        layer_params.append(dict(zip(keys, args[pos:pos + len(keys)])))
        pos += len(keys)
    return _forward(x, c, ctx, c_ctx, layer_params, tb=256)
```

```python
import functools

import jax
import jax.numpy as jnp
import numpy as np
from jax import lax
from jax.experimental import pallas as pl
from jax.experimental.pallas import tpu as pltpu

F32 = jnp.float32
BF16 = jnp.bfloat16

HEAD_DIM = 64
CHUNK = 64
SCAN_BATCH = 4
GRID_W = 64
WIN_H = 8
WIN_W = 16
ROPE_THETA = 10000.0
RMS_EPS = 1e-6
RW_GN_EPS = 64e-5
LRU_C = 8.0
DECAY_SCALE = float(np.exp(-0.5))
SUBLANES = 8
LANES = 128
NEG_BIG = -1e30
VMEM_LIMIT = 56 * 1024 * 1024


def _cparams(n_axes):
    return pltpu.CompilerParams(
        dimension_semantics=("arbitrary",) * n_axes, vmem_limit_bytes=VMEM_LIMIT)


def _bf(x):
    return x.astype(BF16)


def _dot(a, b):
    return jnp.dot(_bf(a), _bf(b), preferred_element_type=F32)


def _cumsum_rows(x, reverse):
    n = x.shape[0]
    rows = _rows(x.shape)
    s = 1
    while s < n:
        if reverse:
            x = x + jnp.where(rows < n - s, pltpu.roll(x, n - s, 0), 0.0)
        else:
            x = x + jnp.where(rows >= s, pltpu.roll(x, s, 0), 0.0)
        s *= 2
    return x


def _sigmoid(x):
    return 0.5 * jnp.tanh(0.5 * x) + 0.5


def _silu(x):
    return x * _sigmoid(x)


def _softplus(x):
    return jnp.maximum(x, 0.0) + jnp.log(1.0 + jnp.exp(-jnp.abs(x)))


def _rows(shape):
    return lax.broadcasted_iota(jnp.int32, shape, 0)


def _shift_down(x, s, fill):
    return jnp.where(_rows(x.shape) >= s, pltpu.roll(x, s, 0), fill)


def _shift_up(x, s, fill):
    n = x.shape[0]
    return jnp.where(_rows(x.shape) < n - s, pltpu.roll(x, n - s, 0), fill)


def _norm_mod(xb, g, scale1, shift):
    xf = xb.astype(F32)
    ms = jnp.mean(xf * xf, axis=-1, keepdims=True)
    return xf * lax.rsqrt(ms + RMS_EPS) * g * scale1 + shift


def _ada_kernel(c_ref, w_ref, b_ref, o_ref):
    o_ref[...] = _dot(_silu(c_ref[...]), w_ref[...]) + b_ref[...]


def ada_mod(c_all, ada_w, ada_b):
    m, d = c_all.shape
    n = ada_w.shape[1]
    tn = d
    return pl.pallas_call(
        _ada_kernel,
        grid=(n // tn,),
        in_specs=[pl.BlockSpec((m, d), lambda j: (0, 0)),
                  pl.BlockSpec((d, tn), lambda j: (0, j)),
                  pl.BlockSpec((1, tn), lambda j: (0, j))],
        out_specs=pl.BlockSpec((m, tn), lambda j: (0, j)),
        out_shape=jax.ShapeDtypeStruct((m, n), F32),
        compiler_params=_cparams(1),
        name="ada_mod",
    )(c_all, _bf(ada_w), ada_b.reshape(1, n))


def _seg_of_block(t, n_ctx_blocks):
    return (t >= n_ctx_blocks).astype(jnp.int32)


def _scan_order(c, n_ctx_blocks, n_blocks, reverse):
    if not reverse:
        return c
    return jnp.where(c < n_ctx_blocks, n_ctx_blocks - 1 - c, n_blocks - 1 + n_ctx_blocks - c)


def _in_mm_kernel(x_ref, mod_ref, g_ref, w_ref, *o_refs, splits, acts):
    h = _norm_mod(x_ref[0], g_ref[...], mod_ref[0, 0, 0:1], mod_ref[0, 0, 1:2])
    hb = _bf(h)
    off = 0
    for o_ref, n, act in zip(o_refs, splits, acts):
        z = jnp.dot(hb, w_ref[:, off:off + n], preferred_element_type=F32)
        if act == "silu":
            z = _silu(z)
        o_ref[0] = z.astype(o_ref.dtype)
        off += n


def in_matmul(x, mod, norm_g, w, splits, acts, n_ctx, tb):
    B, Tt, D = x.shape
    ncb = n_ctx // tb
    n = w.shape[1]
    assert sum(splits) == n
    kern = functools.partial(_in_mm_kernel, splits=tuple(splits), acts=tuple(acts))
    return pl.pallas_call(
        kern,
        grid=(B, Tt // tb),
        in_specs=[pl.BlockSpec((1, tb, D), lambda b, t: (b, t, 0)),
                  pl.BlockSpec((1, 1, 2, D), lambda b, t: (b, _seg_of_block(t, ncb), 0, 0)),
                  pl.BlockSpec((1, D), lambda b, t: (0, 0)),
                  pl.BlockSpec((D, n), lambda b, t: (0, 0))],
        out_specs=[pl.BlockSpec((1, tb, s), lambda b, t: (b, t, 0)) for s in splits],
        out_shape=[jax.ShapeDtypeStruct((B, Tt, s), F32) for s in splits],
        compiler_params=_cparams(2),
        name="in_matmul",
    )(x, mod, norm_g.reshape(1, D), _bf(w))


def _out_mm_kernel(*refs, n_a):
    a_refs = refs[:n_a]
    w_ref, x_ref, gate_ref, o_ref = refs[n_a:]
    a = a_refs[0][0]
    for r in a_refs[1:]:
        a = a * r[0]
    o_ref[0] = x_ref[0] + gate_ref[0, 0] * _dot(a, w_ref[...])


def out_matmul(a_list, w, x, gate, n_ctx, tb, latent_only=False):
    B, Tt, D = x.shape
    K = w.shape[0]
    ncb = n_ctx // tb
    skip = ncb if latent_only else 0
    kern = functools.partial(_out_mm_kernel, n_a=len(a_list))
    return pl.pallas_call(
        kern,
        grid=(B, Tt // tb - skip),
        in_specs=[pl.BlockSpec((1, tb, K), lambda b, t: (b, t + skip, 0)) for _ in a_list] + [
            pl.BlockSpec((K, D), lambda b, t: (0, 0)),
            pl.BlockSpec((1, tb, D), lambda b, t: (b, t + skip, 0)),
            pl.BlockSpec((1, 1, 1, D), lambda b, t: (b, _seg_of_block(t + skip, ncb), 0, 0))],
        out_specs=pl.BlockSpec((1, tb, D), lambda b, t: (b, t, 0)),
        out_shape=jax.ShapeDtypeStruct((B, Tt - skip * tb, D), F32),
        compiler_params=_cparams(2),
        name="out_matmul",
    )(*a_list, _bf(w), x, gate)


_V_NORM_G, _V_MU, _V_KK, _V_KA, _V_B0, _V_RK = 0, 1, 7, 8, 9, 13
_V_ROWS = 16


def _head_sum(x, sel_ref, selt_ref):
    mm = lambda a, w: jnp.dot(a, w, preferred_element_type=F32)
    hi = _bf(x)
    lo = _bf(x - hi.astype(F32))
    s = mm(jnp.concatenate([hi, lo], axis=1), sel_ref[...])
    s_hi = _bf(s)
    s_lo = _bf(s - s_hi.astype(F32))
    return mm(jnp.concatenate([s_hi, s_lo], axis=1), selt_ref[...])


def _head_selectors(D):
    hid = jnp.arange(D) // HEAD_DIM
    sel = (hid[:, None] == jnp.arange(LANES)[None, :]).astype(BF16)
    return jnp.concatenate([sel, sel], axis=0), jnp.concatenate([sel.T, sel.T], axis=0)


def _rwkv_proj_kernel(x_ref, xp_ref, xn_ref, mod_ref, vec_ref, win_ref, dw_ref, da_ref, upw_ref,
                      upa_ref, sel_ref, selt_ref, r_ref, v_ref, g_ref, kk_ref, bonus_ref, lw0_ref,
                      a0_ref, k0_ref, lw1_ref, a1_ref, k1_ref, *, n_ctx_blocks):
    t = pl.program_id(1)
    nb = pl.num_programs(1)
    vec = lambda i: vec_ref[i:i + 1, :]
    scale1, shift = mod_ref[0, 0, 0:1], mod_ref[0, 0, 1:2]
    g = vec(_V_NORM_G)
    h = _norm_mod(x_ref[0], g, scale1, shift)
    first = jnp.logical_or(t == 0, t == n_ctx_blocks)
    last = jnp.logical_or(t == n_ctx_blocks - 1, t == nb - 1)
    hp = _norm_mod(xp_ref[0], g, scale1, shift)[SUBLANES - 1:SUBLANES]
    hn = _norm_mod(xn_ref[0], g, scale1, shift)[0:1]
    hp = jnp.where(first, 0.0, hp)
    hn = jnp.where(last, 0.0, hn)
    h_prev = _shift_down(h, 1, jnp.broadcast_to(hp, h.shape))
    h_next = _shift_up(h, 1, jnp.broadcast_to(hn, h.shape))
    xx = 0.5 * (h_prev + h_next) - h
    mix = lambda j: _bf(h + xx * vec(_V_MU + j))
    mm = lambda a, w: jnp.dot(a, w, preferred_element_type=F32)
    xr, xw, xk, xv, xa, xg = [mix(j) for j in range(6)]
    r = mm(xr, win_ref[0])
    k = mm(xk, win_ref[1])
    v = mm(xv, win_ref[2])
    r_ref[0] = r
    v_ref[0] = v
    g_ref[0] = _silu(mm(xg, win_ref[3]))
    kk_raw = k * vec(_V_KK)
    kk_ref[0] = kk_raw * jnp.minimum(lax.rsqrt(_head_sum(kk_raw * kk_raw, sel_ref, selt_ref)), 1e12)
    tw = _bf(jnp.tanh(mm(xw, dw_ref[...])))
    ta = _bf(mm(xa, da_ref[...]))
    kka = k * vec(_V_KA)
    k_rest = k - kka
    k_sum = None
    for d, (lw_ref, a_ref, kd_ref) in enumerate(((lw0_ref, a0_ref, k0_ref), (lw1_ref, a1_ref, k1_ref))):
        zw = vec(_V_B0 + 2 * d) + mm(tw, upw_ref[d])
        lw_ref[0] = -DECAY_SCALE * _sigmoid(zw)
        a_d = _sigmoid(vec(_V_B0 + 2 * d + 1) + mm(ta, upa_ref[d]))
        a_ref[0] = a_d
        k_d = k_rest + kka * a_d
        kd_ref[0] = k_d
        k_sum = k_d if k_sum is None else k_sum + k_d
    bonus_ref[0] = _head_sum(r * k_sum * vec(_V_RK), sel_ref, selt_ref) * v


def rwkv_proj(x, mod, p, n_ctx, tb):
    B, Tt, D = x.shape
    lora = p['lora_down'].shape[-1]
    ncb = n_ctx // tb
    nb8 = Tt // SUBLANES
    r8 = tb // SUBLANES
    vec = jnp.zeros((_V_ROWS, D), F32)
    vec = vec.at[_V_NORM_G].set(p['norm_g']).at[_V_MU:_V_MU + 6].set(p['mu'])
    vec = vec.at[_V_KK].set(p['k_ka'][0]).at[_V_KA].set(p['k_ka'][1])
    vec = vec.at[_V_B0:_V_B0 + 4].set(p['lora_b0'].reshape(4, D)).at[_V_RK].set(p['r_k'].reshape(D))
    sel, selt = _head_selectors(D)
    down, up = p['lora_down'], p['lora_up']
    dw = _bf(jnp.concatenate([down[0, 0], down[1, 0]], axis=1))
    da = _bf(jnp.concatenate([down[0, 1], down[1, 1]], axis=1))
    z = jnp.zeros((lora, D), F32)
    upw = _bf(jnp.stack([jnp.concatenate([up[0, 0], z], 0), jnp.concatenate([z, up[1, 0]], 0)]))
    upa = _bf(jnp.stack([jnp.concatenate([up[0, 1], z], 0), jnp.concatenate([z, up[1, 1]], 0)]))
    full = lambda shape: pl.BlockSpec(shape, lambda b, t: (0,) * len(shape))
    tok = pl.BlockSpec((1, tb, D), lambda b, t: (b, t, 0))
    kern = functools.partial(_rwkv_proj_kernel, n_ctx_blocks=ncb)
    outs = pl.pallas_call(
        kern,
        grid=(B, Tt // tb),
        in_specs=[tok,
                  pl.BlockSpec((1, SUBLANES, D), lambda b, t: (b, jnp.maximum(t * r8 - 1, 0), 0)),
                  pl.BlockSpec((1, SUBLANES, D), lambda b, t: (b, jnp.minimum((t + 1) * r8, nb8 - 1), 0)),
                  pl.BlockSpec((1, 1, 2, D), lambda b, t: (b, _seg_of_block(t, ncb), 0, 0)),
                  full((_V_ROWS, D)), full((4, D, D)), full((D, 2 * lora)), full((D, 2 * lora)),
                  full((2, 2 * lora, D)), full((2, 2 * lora, D)), full((2 * D, LANES)), full((2 * LANES, D))],
        out_specs=[tok] * 11,
        out_shape=[jax.ShapeDtypeStruct((B, Tt, D), F32)] * 11,
        compiler_params=_cparams(2),
        name="rwkv_proj",
    )(x, x, x, mod, vec, _bf(p['w_in']), dw, da, upw, upa, sel, selt)
    names = ('r', 'v', 'gate', 'kk', 'bonus', 'lw0', 'a0', 'k0', 'lw1', 'a1', 'k1')
    return dict(zip(names, outs))


def _level_masks(L, reverse):
    ri = lax.broadcasted_iota(jnp.int32, (L, 2 * L), 0)
    ci = lax.broadcasted_iota(jnp.int32, (L, 2 * L), 1) & (L - 1)
    if reverse:
        ri, ci = ci, ri
    masks = []
    for j in range(int(np.log2(L))):
        same = (ri >> (j + 1)) == (ci >> (j + 1))
        masks.append(same & (((ri >> j) & 1) == 1) & (((ci >> j) & 1) == 0))
    return masks


def _rwkv_chunk_pairs(rt, kap, bt, kt, v, wl, st, tri_strict, tri_incl, eye, masks, first):
    ps = range(len(rt))
    L = v[0].shape[0]
    mm = lambda a, b: jnp.dot(a, b, preferred_element_type=F32)
    mm_nt = lambda a, b: lax.dot_general(a, b, (((1,), (1,)), ((), ())), preferred_element_type=F32)
    mm_tn = lambda a, b: lax.dot_general(a, b, (((0,), (0,)), ((), ())), preferred_element_type=F32)

    def bd(x):
        z = jnp.zeros_like(x)
        return jnp.concatenate([jnp.where(first, x, z), jnp.where(first, z, x)], axis=0)

    def diag_blocks(x):
        return jnp.where(first, x[:L], x[L:])

    v_b = [_bf(v[p]) for p in ps]
    kap_b = [_bf(kap[p]) for p in ps]
    v_bd = [bd(v_b[p]) for p in ps]
    lhs = [jnp.concatenate([kap_b[p], _bf(rt[p])], axis=0) for p in ps]
    sb = [mm_nt(lhs[p], bd(_bf(bt[p]))) for p in ps]
    sk = [mm_nt(lhs[p], bd(_bf(kt[p]))) for p in ps]
    n_b = [jnp.where(tri_strict, sb[p][:L], 0.0) for p in ps]
    a_rb = [_bf(jnp.where(tri_incl, sb[p][L:], 0.0)) for p in ps]
    n_k = [_bf(jnp.where(tri_strict, sk[p][:L], 0.0)) for p in ps]
    a_rk = [_bf(jnp.where(tri_incl, sk[p][L:], 0.0)) for p in ps]
    t = [jnp.where(eye, 1.0, 0.0) - jnp.where(masks[0], n_b[p], 0.0) for p in ps]
    for m in masks[1:]:
        t_b = [_bf(t[p]) for p in ps]
        ct = [_bf(mm(_bf(jnp.where(m, n_b[p], 0.0)), bd(t_b[p]))) for p in ps]
        t = [t[p] - mm(t_b[p], bd(ct[p])) for p in ps]
    t_b = [_bf(t[p]) for p in ps]
    nkv = [_bf(mm(n_k[p], v_bd[p])) for p in ps]
    e0 = [_bf(-mm(t_b[p], bd(nkv[p]))) for p in ps]
    g = [_bf(-mm(t_b[p], bd(kap_b[p]))) for p in ps]
    y0 = [mm(jnp.concatenate([a_rb[p], a_rk[p]], axis=1), jnp.concatenate([bd(e0[p]), v_bd[p]], axis=0))
          for p in ps]
    qp = [rt[p] + mm(a_rb[p], bd(g[p])) for p in ps]
    btw = [_bf(bt[p] * wl[p]) for p in ps]
    ktw = [_bf(kt[p] * wl[p]) for p in ps]
    ut = [diag_blocks(mm_tn(jnp.concatenate([btw[p], ktw[p]], axis=0),
                            jnp.concatenate([e0[p], v_b[p]], axis=0))) for p in ps]
    pt = [jnp.where(eye, wl[p], 0.0) + diag_blocks(mm_tn(btw[p], g[p])) for p in ps]
    ys = [mm(_bf(jnp.concatenate([qp[p], pt[p]], axis=0)), bd(_bf(st[p]))) for p in ps]
    return [ys[p][:L] + y0[p] for p in ps], [ys[p][L:] + ut[p] for p in ps]


def _rwkv_scan_kernel(*refs, reverse, finish, n_heads, bb):
    if finish:
        (r_ref, lw_ref, k_ref, kk_ref, a_ref, v_ref, yf_ref, bonus_ref, gate_ref, vec_ref,
         o_ref, st_ref) = refs
    else:
        r_ref, lw_ref, k_ref, kk_ref, a_ref, v_ref, o_ref, st_ref = refs

    @pl.when(pl.program_id(1) == 0)
    def _():
        st_ref[...] = jnp.zeros_like(st_ref)

    L = r_ref.shape[1]
    hd = HEAD_DIM
    assert L == hd and 2 * hd == LANES
    n_pairs = n_heads // 2
    ri2 = lax.broadcasted_iota(jnp.int32, (L, LANES), 0)
    lane = lax.broadcasted_iota(jnp.int32, (L, LANES), 1)
    ci2 = lane & (L - 1)
    if reverse:
        tri_incl, tri_strict = ci2 >= ri2, ci2 > ri2
    else:
        tri_incl, tri_strict = ci2 <= ri2, ci2 < ri2
    eye = ri2 == ci2
    first = lane < hd
    masks = _level_masks(L, reverse)
    end = 0 if reverse else L - 1

    sls = [slice(p * LANES, (p + 1) * LANES) for p in range(n_pairs)]
    split = lambda x: [x[:, sl] for sl in sls]
    rt, kap, bt, kt, vs, wl, st = [], [], [], [], [], [], []
    for i in range(bb):
        lw = lw_ref[i]
        cum = _cumsum_rows(lw, reverse)
        w_t = jnp.exp(cum)
        w_i = jnp.exp(-cum)
        w_p = jnp.exp(cum - lw)
        kk = kk_ref[i]
        rt += split(r_ref[i] * w_t)
        kap += split(kk * w_p)
        bt += split(kk * a_ref[i] * w_i)
        kt += split(k_ref[i] * w_i)
        vs += split(v_ref[i])
        wl += split(w_t[end:end + 1])
        st += split(st_ref[i])
    outs, states = _rwkv_chunk_pairs(rt, kap, bt, kt, vs, wl, st, tri_strict, tri_incl, eye, masks, first)
    lrow = lax.broadcasted_iota(jnp.int32, (LANES, LANES), 0)
    lcol = lax.broadcasted_iota(jnp.int32, (LANES, LANES), 1)
    ones = jnp.where((lrow < hd) == (lcol < hd), 1.0, 0.0).astype(BF16)
    lane_sum = lambda x: jnp.dot(x, ones, preferred_element_type=F32)
    for i in range(bb):
        mine = slice(i * n_pairs, (i + 1) * n_pairs)
        st_ref[i] = jnp.concatenate(states[mine], axis=1)
        out = jnp.concatenate(outs[mine], axis=1)
        if finish:
            ys = jnp.concatenate(split(out + yf_ref[i]), axis=0)
            ys_hi = _bf(ys)
            mean = (lane_sum(ys_hi) + lane_sum(_bf(ys - ys_hi.astype(F32)))) * (1.0 / hd)
            cen = ys - mean
            c2 = cen * cen
            c2_hi = _bf(c2)
            var = (lane_sum(c2_hi) + lane_sum(_bf(c2 - c2_hi.astype(F32)))) * (1.0 / hd)
            yn = cen * lax.rsqrt(var + RW_GN_EPS)
            out = jnp.concatenate([yn[p * L:(p + 1) * L] for p in range(n_pairs)], axis=1)
            out = (out * vec_ref[0:1] + vec_ref[1:2] + bonus_ref[i]) * gate_ref[i]
        o_ref[i] = out.astype(o_ref.dtype)


def rwkv_scan(pr, d, n_ctx, reverse, finish=None):
    r = pr['r']
    B, Tt, D = r.shape
    L = CHUNK
    nc, ncc = Tt // L, n_ctx // L
    bb = next(n for n in (SCAN_BATCH, 2, 1) if B % n == 0)
    tok = pl.BlockSpec((bb, L, D), lambda b, c: (b, _scan_order(c, ncc, nc, reverse), 0))
    ins = [r, pr['lw%d' % d], pr['k%d' % d], pr['kk'], pr['a%d' % d], pr['v']]
    specs = [tok] * 6
    if finish is not None:
        y_fwd, gn = finish
        vec = jnp.zeros((SUBLANES, D), F32).at[0].set(gn[0]).at[1].set(gn[1])
        ins += [y_fwd, pr['bonus'], pr['gate'], vec]
        specs += [tok, tok, tok, pl.BlockSpec((SUBLANES, D), lambda b, c: (0, 0))]
    kern = functools.partial(_rwkv_scan_kernel, reverse=reverse, finish=finish is not None,
                             n_heads=D // HEAD_DIM, bb=bb)
    return pl.pallas_call(
        kern,
        grid=(B // bb, nc),
        in_specs=specs,
        out_specs=tok,
        out_shape=jax.ShapeDtypeStruct((B, Tt, D), F32 if finish is None else BF16),
        scratch_shapes=[pltpu.VMEM((bb, HEAD_DIM, D), F32)],
        compiler_params=_cparams(2),
        name="rwkv_scan_rev" if reverse else "rwkv_scan_fwd",
    )(*ins)


def rwkv_layer(x, mod, gate, p, n_ctx, tb, latent_only=False):
    pr = rwkv_proj(x, mod, p, n_ctx, tb)
    y_fwd = rwkv_scan(pr, 0, n_ctx, reverse=False)
    z = rwkv_scan(pr, 1, n_ctx, reverse=True, finish=(y_fwd, p['gn']))
    return out_matmul([z], p['w_out'], x, gate, n_ctx, tb, latent_only)


GATE_COLS = 256


def _gate_tiles(W, bd):
    assert bd <= LANES and W % LANES == 0
    win = min(W, GATE_COLS + 2 * LANES)
    tiles = []
    for c0 in range(0, W, GATE_COLS):
        lo = min(max(c0 - LANES, 0), W - win)
        tiles.append((c0, min(GATE_COLS, W - c0), lo))
    return win, tiles


def _rglru_scan_kernel(*refs, reverse, finish, n_ctx_blocks, block_dim):
    if finish:
        (x_ref, xp_ref, xn_ref, cw_ref, vec_ref, wg_ref, hf_ref, gs_ref, o_ref,
         a_scr, b_scr, carry_ref) = refs
    else:
        x_ref, xp_ref, xn_ref, cw_ref, vec_ref, wg_ref, o_ref, a_scr, b_scr, carry_ref = refs
    c = pl.program_id(1)
    nb = pl.num_programs(1)
    t = _scan_order(c, n_ctx_blocks, nb, reverse)

    @pl.when(c == 0)
    def _():
        carry_ref[...] = jnp.zeros_like(carry_ref)

    x = x_ref[0]
    tb, W = x.shape
    first = jnp.logical_or(t == 0, t == n_ctx_blocks)
    last = jnp.logical_or(t == n_ctx_blocks - 1, t == nb - 1)
    xp = jnp.where(first, 0.0, xp_ref[0])
    xn = jnp.where(last, 0.0, xn_ref[0])
    row = lambda a, i: jnp.broadcast_to(a[i:i + 1], x.shape)
    rows = _rows(x.shape)
    x_m1 = _shift_down(x, 1, row(xp, SUBLANES - 1))
    x_m2 = _shift_down(x, 2, jnp.where(rows == 0, row(xp, SUBLANES - 2), row(xp, SUBLANES - 1)))
    x_p1 = _shift_up(x, 1, row(xn, 0))
    cw = lambda i: cw_ref[i:i + 1, :]
    xc = cw(4) + x_m2 * cw(0) + x_m1 * cw(1) + x * cw(2) + x_p1 * cw(3)
    win, tiles = _gate_tiles(W, block_dim)
    xc_b = _bf(xc)
    z = [jnp.dot(xc_b[:, lo:lo + win], wg_ref[j], preferred_element_type=F32)
         for j, (_, _, lo) in enumerate(tiles)]
    gate_r = jnp.concatenate([z[j][:, :cw_] for j, (_, cw_, _) in enumerate(tiles)], axis=1)
    gate_i = jnp.concatenate([z[j][:, GATE_COLS:GATE_COLS + cw_] for j, (_, cw_, _) in enumerate(tiles)],
                             axis=1)
    r = _sigmoid(gate_r + vec_ref[0:1])
    i = _sigmoid(gate_i + vec_ref[1:2])
    log_a = -LRU_C * r * _softplus(-vec_ref[2:3])
    a = jnp.exp(log_a)
    a_scr[...] = a
    b_scr[...] = jnp.sqrt(1.0 - a * a) * (i * xc)

    S = SUBLANES
    ng = tb // S
    sub = _rows((S, W))

    def group(gi, carry):
        g = (ng - 1 - gi) if reverse else gi
        a = a_scr[pl.ds(pl.multiple_of(g * S, S), S), :]
        b = b_scr[pl.ds(pl.multiple_of(g * S, S), S), :]
        for s in (1, 2, 4):
            if reverse:
                ok = sub < S - s
                a_s = jnp.where(ok, pltpu.roll(a, S - s, 0), 1.0)
                b_s = jnp.where(ok, pltpu.roll(b, S - s, 0), 0.0)
            else:
                ok = sub >= s
                a_s = jnp.where(ok, pltpu.roll(a, s, 0), 1.0)
                b_s = jnp.where(ok, pltpu.roll(b, s, 0), 0.0)
            b = a * b_s + b
            a = a * a_s
        h = b + a * carry
        b_scr[pl.ds(pl.multiple_of(g * S, S), S), :] = h
        e = 0 if reverse else S - 1
        return jnp.broadcast_to(h[e:e + 1], (S, W))

    carry_ref[...] = lax.fori_loop(0, ng, group, carry_ref[...])
    h_all = b_scr[...]
    if finish:
        h_all = (h_all + hf_ref[0]) * gs_ref[0]
    o_ref[0] = h_all.astype(o_ref.dtype)


def rglru_scan(xr, p, d, n_ctx, tb, reverse, finish=None):
    B, Tt, W = xr.shape
    nblk = p['gate_w'].shape[2]
    bd = W // nblk
    ncb = n_ctx // tb
    nb = Tt // tb
    nb8 = Tt // SUBLANES
    r8 = tb // SUBLANES

    def dense(wb):
        eye = jnp.eye(nblk, dtype=wb.dtype)
        return jnp.einsum('ncd,nm->ncmd', wb, eye).reshape(W, W)

    win, tiles = _gate_tiles(W, bd)
    w_r, w_i = dense(p['gate_w'][d, 0]), dense(p['gate_w'][d, 1])
    pad = lambda m: jnp.pad(m, ((0, 0), (0, GATE_COLS - m.shape[1])))
    wg = _bf(jnp.stack([jnp.concatenate([pad(w_r[lo:lo + win, c0:c0 + cw_]), pad(w_i[lo:lo + win, c0:c0 + cw_])],
                                        axis=1) for c0, cw_, lo in tiles]))
    cw = jnp.zeros((SUBLANES, W), F32).at[0:4].set(p['conv_w']).at[4].set(p['conv_b'])
    vec = jnp.zeros((SUBLANES, W), F32).at[0:2].set(p['gate_b'][d]).at[2].set(p['lam'][d])
    order = lambda c: _scan_order(c, ncb, nb, reverse)
    tok = pl.BlockSpec((1, tb, W), lambda b, c: (b, order(c), 0))
    full = lambda shape: pl.BlockSpec(shape, lambda b, c: (0,) * len(shape))
    ins = [xr, xr, xr, cw, vec, wg]
    specs = [tok,
             pl.BlockSpec((1, SUBLANES, W), lambda b, c: (b, jnp.maximum(order(c) * r8 - 1, 0), 0)),
             pl.BlockSpec((1, SUBLANES, W), lambda b, c: (b, jnp.minimum((order(c) + 1) * r8, nb8 - 1), 0)),
             full((SUBLANES, W)), full((SUBLANES, W)), full((len(tiles), win, 2 * GATE_COLS))]
    if finish is not None:
        ins += list(finish)
        specs += [tok, tok]
    kern = functools.partial(_rglru_scan_kernel, reverse=reverse, finish=finish is not None,
                             n_ctx_blocks=ncb, block_dim=bd)
    return pl.pallas_call(
        kern,
        grid=(B, nb),
        in_specs=specs,
        out_specs=tok,
        out_shape=jax.ShapeDtypeStruct((B, Tt, W), F32 if finish is None else BF16),
        scratch_shapes=[pltpu.VMEM((tb, W), F32), pltpu.VMEM((tb, W), F32), pltpu.VMEM((SUBLANES, W), F32)],
        compiler_params=_cparams(2),
        name="rglru_scan_rev" if reverse else "rglru_scan_fwd",
    )(*ins)


def rglru_layer(x, mod, gate, p, n_ctx, tb, latent_only=False):
    W = p['conv_w'].shape[1]
    xr, gs = in_matmul(x, mod, p['norm_g'], p['w_in'], (W, W), (None, "silu"), n_ctx, tb)
    h_fwd = rglru_scan(xr, p, 0, n_ctx, tb, reverse=False)
    z = rglru_scan(xr, p, 1, n_ctx, tb, reverse=True, finish=(h_fwd, gs))
    return out_matmul([z], p['w_out'], x, gate, n_ctx, tb, latent_only)


def _natten_in_kernel(x_ref, mod_ref, ng_ref, w_ref, cs_ref, g_ref, sel_ref, selt_ref,
                      qn_ref, qr_ref, kr_ref, vb_ref, gs_ref):
    D = x_ref.shape[2]
    hb = _bf(_norm_mod(x_ref[0], ng_ref[...], mod_ref[0, 0, 0:1], mod_ref[0, 0, 1:2]))
    proj = lambda j: jnp.dot(hb, w_ref[:, j * D:(j + 1) * D], preferred_element_type=F32)
    reps = D // cs_ref.shape[2]
    cos = jnp.concatenate([cs_ref[0]] * reps, axis=1)
    sin = jnp.concatenate([cs_ref[1]] * reps, axis=1)
    lane = lax.broadcasted_iota(jnp.int32, cos.shape, 1)
    quarter = HEAD_DIM // 4
    low = (lane % (2 * quarter)) < quarter

    def prep(x, g):
        ms = _head_sum(x * x, sel_ref, selt_ref) * (1.0 / HEAD_DIM)
        xn = x * lax.rsqrt(ms + RMS_EPS) * g
        partner = jnp.where(low, pltpu.roll(xn, D - quarter, 1), pltpu.roll(xn, quarter, 1))
        return xn, xn * cos + partner * sin

    qn, qr = prep(proj(0), g_ref[0:1])
    _, kr = prep(proj(1), g_ref[1:2])
    scale = HEAD_DIM ** -0.5
    qn_ref[0] = _bf(qn * scale)
    qr_ref[0] = _bf(qr * scale)
    kr_ref[0] = _bf(kr)
    vb_ref[0] = _bf(proj(2))
    gs_ref[0] = _silu(proj(3))


def natten_in(x, mod, norm_g, w_in, qk_g, n_ctx, tb):
    B, Tt, D = x.shape
    H = D // HEAD_DIM
    T = Tt - n_ctx
    ncb = n_ctx // tb
    nfreq = HEAD_DIM // 4
    pos = jnp.arange(T)
    inv = ROPE_THETA ** (-jnp.arange(nfreq, dtype=F32) / nfreq)
    ang_r = (pos // GRID_W).astype(F32)[:, None] * inv
    ang_c = (pos % GRID_W).astype(F32)[:, None] * inv
    cos = jnp.concatenate([jnp.cos(ang_r)] * 2 + [jnp.cos(ang_c)] * 2, axis=1)
    sin = jnp.concatenate([-jnp.sin(ang_r), jnp.sin(ang_r), -jnp.sin(ang_c), jnp.sin(ang_c)], axis=1)
    cs = jnp.stack([jnp.concatenate([jnp.ones((n_ctx, HEAD_DIM), F32), cos], 0),
                    jnp.concatenate([jnp.zeros((n_ctx, HEAD_DIM), F32), sin], 0)])
    cs = jnp.concatenate([cs, cs], axis=2)
    g = jnp.zeros((SUBLANES, D), F32).at[0].set(jnp.tile(qk_g[0], H)).at[1].set(jnp.tile(qk_g[1], H))
    sel, selt = _head_selectors(D)
    tok = pl.BlockSpec((1, tb, D), lambda b, t: (b, t, 0))
    full = lambda shape: pl.BlockSpec(shape, lambda b, t: (0,) * len(shape))
    return pl.pallas_call(
        _natten_in_kernel,
        grid=(B, Tt // tb),
        in_specs=[tok,
                  pl.BlockSpec((1, 1, 2, D), lambda b, t: (b, _seg_of_block(t, ncb), 0, 0)),
                  full((1, D)), full((D, 4 * D)),
                  pl.BlockSpec((2, tb, 2 * HEAD_DIM), lambda b, t: (0, t, 0)),
                  full((SUBLANES, D)), full((2 * D, LANES)), full((2 * LANES, D))],
        out_specs=[tok] * 5,
        out_shape=[jax.ShapeDtypeStruct((B, Tt, D), BF16)] * 4 + [jax.ShapeDtypeStruct((B, Tt, D), F32)],
        compiler_params=_cparams(2),
        name="natten_in",
    )(x, mod, norm_g.reshape(1, D), _bf(w_in), cs, g, sel, selt)


def _natten_kernel(qr_ref, qn_ref, kr_ref, v_ref, bias_ref, o_ref, *, rows, kh, rb, n_ctx):
    gw = GRID_W
    lane = lax.broadcasted_iota(jnp.int32, (gw, 2 * HEAD_DIM), 1)
    head_lanes = [lane < HEAD_DIM, lane >= HEAD_DIM]
    kc = kr_ref[0, 0:n_ctx, :]
    vc = v_ref[0, 0:n_ctx, :]
    nt = lambda a, b: lax.dot_general(a, b, (((1,), (1,)), ((), ())), preferred_element_type=F32)
    mm = lambda a, b: jnp.dot(a, b, preferred_element_type=F32)

    q_c = qn_ref[0, 0:n_ctx, :]
    lane_c = lax.broadcasted_iota(jnp.int32, q_c.shape, 1)
    o_c = []
    for h in range(2):
        mine = (lane_c >= HEAD_DIM) if h else (lane_c < HEAD_DIM)
        s = nt(jnp.where(mine, q_c, jnp.zeros_like(q_c)), kc)
        p = jnp.exp(s - s.max(axis=-1, keepdims=True))
        o_c.append(mm(_bf(p), vc) / p.sum(axis=-1, keepdims=True))
    o_ref[0, 0:n_ctx, :] = jnp.where(lane_c < HEAD_DIM, o_c[0], o_c[1])

    def row_group(g, carry):
        q0s, qrs, qns, kbs, vbs, biases = [], [], [], [], [], []
        for j in range(rb):
            r = g * rb + j
            start = jnp.clip(r - kh // 2, 0, rows - kh)
            d0 = start - r + kh - 1
            q0 = pl.multiple_of(n_ctx + r * gw, gw)
            k0 = pl.multiple_of(n_ctx + start * gw, gw)
            qr = qr_ref[0, pl.ds(q0, gw), :]
            qn = qn_ref[0, pl.ds(q0, gw), :]
            zero = jnp.zeros_like(qr)
            by_head = lambda q: jnp.concatenate([jnp.where(m, q, zero) for m in head_lanes], axis=0)
            q0s.append(q0)
            qrs.append(by_head(qr))
            qns.append(by_head(qn))
            kbs.append(kr_ref[0, pl.ds(k0, kh * gw), :])
            vbs.append(v_ref[0, pl.ds(k0, kh * gw), :])
            biases.append(jnp.concatenate([bias_ref[0, d0], bias_ref[1, d0]], axis=0))
        n = range(rb)
        s_band = [nt(qrs[i], kbs[i]) + biases[i] for i in n]
        s_ctx_all = nt(jnp.concatenate(qns, axis=0), kc)
        s_ctx = [s_ctx_all[i * 2 * gw:(i + 1) * 2 * gw] for i in n]
        m = [jnp.maximum(s_band[i].max(axis=-1, keepdims=True), s_ctx[i].max(axis=-1, keepdims=True))
             for i in n]
        p_band = [jnp.exp(s_band[i] - m[i]) for i in n]
        p_ctx = [jnp.exp(s_ctx[i] - m[i]) for i in n]
        den = [p_band[i].sum(axis=-1, keepdims=True) + p_ctx[i].sum(axis=-1, keepdims=True) for i in n]
        o_ctx_all = mm(_bf(jnp.concatenate(p_ctx, axis=0)), vc)
        o = [(mm(_bf(p_band[i]), vbs[i]) + o_ctx_all[i * 2 * gw:(i + 1) * 2 * gw]) / den[i] for i in n]
        for i in n:
            o_ref[0, pl.ds(q0s[i], gw), :] = jnp.where(head_lanes[0], o[i][:gw], o[i][gw:])
        return carry

    lax.fori_loop(0, rows // rb, row_group, 0)


def _natten_bias(rpb, rows, kh):
    cols = np.arange(GRID_W)
    c_start = np.clip(cols - WIN_W // 2, 0, GRID_W - WIN_W)
    col_ok = (cols[None, :] >= c_start[:, None]) & (cols[None, :] < c_start[:, None] + WIN_W)
    dc_idx = np.clip(cols[None, :] - cols[:, None] + WIN_W - 1, 0, 2 * WIN_W - 2)
    by_dr = jnp.where(jnp.asarray(col_ok), rpb[:, :, dc_idx].astype(F32), NEG_BIG)
    base = WIN_H - kh
    per_v = [jnp.concatenate([by_dr[:, base + v + i] for i in range(kh)], axis=-1) for v in range(kh)]
    return jnp.stack(per_v, axis=1)


def natten_attention(qr, qn, kr, vb, rpb, n_ctx):
    B, Tt, D = qr.shape
    T = Tt - n_ctx
    rows = T // GRID_W
    kh = min(WIN_H, rows)
    HP = D // (2 * HEAD_DIM)
    assert n_ctx % GRID_W == 0
    bias = _natten_bias(rpb, rows, kh)
    seq = pl.BlockSpec((1, Tt, 2 * HEAD_DIM), lambda hp, b: (b, 0, hp))
    rb = 4 if rows % 4 == 0 else 1
    kern = functools.partial(_natten_kernel, rows=rows, kh=kh, rb=rb, n_ctx=n_ctx)
    return pl.pallas_call(
        kern,
        grid=(HP, B),
        in_specs=[seq, seq, seq, seq,
                  pl.BlockSpec((2, kh, GRID_W, kh * GRID_W), lambda hp, b: (hp, 0, 0, 0))],
        out_specs=seq,
        out_shape=jax.ShapeDtypeStruct((B, Tt, D), F32),
        compiler_params=_cparams(2),
        name="natten",
    )(qr, qn, kr, vb, bias)


def natten_layer(x, mod, gate, p, n_ctx, tb, latent_only=False):
    qn, qr, kr, vb, gs = natten_in(x, mod, p['norm_g'], p['w_in'], p['qk_g'], n_ctx, tb)
    o = natten_attention(qr, qn, kr, vb, p['rpb'], n_ctx)
    return out_matmul([o, gs], p['w_out'], x, gate, n_ctx, tb, latent_only)


_LAYER_KEYS = (
    ('norm_g', 'ada_w', 'ada_b', 'w_in', 'mu', 'lora_b0', 'lora_down', 'lora_up', 'k_ka', 'r_k', 'gn', 'w_out'),
    ('norm_g', 'ada_w', 'ada_b', 'w_in', 'conv_w', 'conv_b', 'gate_w', 'gate_b', 'lam', 'w_out'),
    ('norm_g', 'ada_w', 'ada_b', 'w_in', 'qk_g', 'rpb', 'w_out'),
)
_LAYERS = (rwkv_layer, rglru_layer, natten_layer)


def _forward(x, c, ctx, c_ctx, layer_params, tb):
    B, T, D = x.shape
    n_ctx = ctx.shape[1]
    xs = jnp.concatenate([ctx, x], axis=1)
    m_rows = -(-(B + 1) // SUBLANES) * SUBLANES
    c_all = jnp.zeros((m_rows, D), F32).at[:B].set(c).at[B].set(c_ctx)
    for i, p in enumerate(layer_params):
        m = ada_mod(c_all, p['ada_w'], p['ada_b'])
        m_l = m[:B].reshape(B, 3, D)
        m_c = jnp.broadcast_to(m[B].reshape(1, 3, D), (B, 3, D))
        both = jnp.stack([m_c, m_l], axis=1)
        mod = jnp.stack([1.0 + both[:, :, 1], both[:, :, 0]], axis=2)
        gate = both[:, :, 2:3]
        xs = _LAYERS[i % 3](xs, mod, gate, p, n_ctx, tb, latent_only=i == len(layer_params) - 1)
    return xs


def kernel(x, c, ctx, c_ctx, l0_norm_g, l0_ada_w, l0_ada_b, l0_w_in, l0_mu, l0_lora_b0, l0_lora_down, l0_lora_up, l0_k_ka, l0_r_k, l0_gn, l0_w_out, l1_norm_g, l1_ada_w, l1_ada_b, l1_w_in, l1_conv_w, l1_conv_b, l1_gate_w, l1_gate_b, l1_lam, l1_w_out, l2_norm_g, l2_ada_w, l2_ada_b, l2_w_in, l2_qk_g, l2_rpb, l2_w_out, l3_norm_g, l3_ada_w, l3_ada_b, l3_w_in, l3_mu, l3_lora_b0, l3_lora_down, l3_lora_up, l3_k_ka, l3_r_k, l3_gn, l3_w_out):
    args = (l0_norm_g, l0_ada_w, l0_ada_b, l0_w_in, l0_mu, l0_lora_b0, l0_lora_down, l0_lora_up, l0_k_ka, l0_r_k, l0_gn, l0_w_out, l1_norm_g, l1_ada_w, l1_ada_b, l1_w_in, l1_conv_w, l1_conv_b, l1_gate_w, l1_gate_b, l1_lam, l1_w_out, l2_norm_g, l2_ada_w, l2_ada_b, l2_w_in, l2_qk_g, l2_rpb, l2_w_out, l3_norm_g, l3_ada_w, l3_ada_b, l3_w_in, l3_mu, l3_lora_b0, l3_lora_down, l3_lora_up, l3_k_ka, l3_r_k, l3_gn, l3_w_out)
    layer_params, pos = [], 0
    for i in range(4):
        keys = _LAYER_KEYS[i % 3]
        layer_params.append(dict(zip(keys, args[pos:pos + len(keys)])))
        pos += len(keys)
    return _forward(x, c, ctx, c_ctx, layer_params, tb=256)
```

```python
import functools

import jax
import jax.numpy as jnp
import numpy as np
from jax import lax
from jax.experimental import pallas as pl
from jax.experimental.pallas import tpu as pltpu

F32 = jnp.float32
BF16 = jnp.bfloat16

HEAD_DIM = 64
CHUNK = 64
SCAN_BATCH = 4
GRID_W = 64
WIN_H = 8
WIN_W = 16
ROPE_THETA = 10000.0
RMS_EPS = 1e-6
RW_GN_EPS = 64e-5
LRU_C = 8.0
DECAY_SCALE = float(np.exp(-0.5))
SUBLANES = 8
LANES = 128
NEG_BIG = -1e30
VMEM_LIMIT = 56 * 1024 * 1024


def _cparams(n_axes):
    return pltpu.CompilerParams(
        dimension_semantics=("arbitrary",) * n_axes, vmem_limit_bytes=VMEM_LIMIT)


def _bf(x):
    return x.astype(BF16)


def _dot(a, b):
    return jnp.dot(_bf(a), _bf(b), preferred_element_type=F32)


def _cumsum_rows(x, reverse):
    n = x.shape[0]
    rows = _rows(x.shape)
    s = 1
    while s < n:
        if reverse:
            x = x + jnp.where(rows < n - s, pltpu.roll(x, n - s, 0), 0.0)
        else:
            x = x + jnp.where(rows >= s, pltpu.roll(x, s, 0), 0.0)
        s *= 2
    return x


def _sigmoid(x):
    return 0.5 * jnp.tanh(0.5 * x) + 0.5


def _silu(x):
    return x * _sigmoid(x)


def _softplus(x):
    return jnp.maximum(x, 0.0) + jnp.log(1.0 + jnp.exp(-jnp.abs(x)))


def _rows(shape):
    return lax.broadcasted_iota(jnp.int32, shape, 0)


def _shift_down(x, s, fill):
    return jnp.where(_rows(x.shape) >= s, pltpu.roll(x, s, 0), fill)


def _shift_up(x, s, fill):
    n = x.shape[0]
    return jnp.where(_rows(x.shape) < n - s, pltpu.roll(x, n - s, 0), fill)


def _norm_mod(xb, g, scale1, shift):
    xf = xb.astype(F32)
    ms = jnp.mean(xf * xf, axis=-1, keepdims=True)
    return xf * lax.rsqrt(ms + RMS_EPS) * g * scale1 + shift


def _ada_kernel(c_ref, w_ref, b_ref, o_ref):
    o_ref[...] = _dot(_silu(c_ref[...]), w_ref[...]) + b_ref[...]


def ada_mod(c_all, ada_w, ada_b):
    m, d = c_all.shape
    n = ada_w.shape[1]
    tn = d
    return pl.pallas_call(
        _ada_kernel,
        grid=(n // tn,),
        in_specs=[pl.BlockSpec((m, d), lambda j: (0, 0)),
                  pl.BlockSpec((d, tn), lambda j: (0, j)),
                  pl.BlockSpec((1, tn), lambda j: (0, j))],
        out_specs=pl.BlockSpec((m, tn), lambda j: (0, j)),
        out_shape=jax.ShapeDtypeStruct((m, n), F32),
        compiler_params=_cparams(1),
        name="ada_mod",
    )(c_all, _bf(ada_w), ada_b.reshape(1, n))


def _seg_of_block(t, n_ctx_blocks):
    return (t >= n_ctx_blocks).astype(jnp.int32)


def _scan_order(c, n_ctx_blocks, n_blocks, reverse):
    if not reverse:
        return c
    return jnp.where(c < n_ctx_blocks, n_ctx_blocks - 1 - c, n_blocks - 1 + n_ctx_blocks - c)


def _in_mm_kernel(x_ref, mod_ref, g_ref, w_ref, *o_refs, splits, acts):
    h = _norm_mod(x_ref[0], g_ref[...], mod_ref[0, 0, 0:1], mod_ref[0, 0, 1:2])
    hb = _bf(h)
    off = 0
    for o_ref, n, act in zip(o_refs, splits, acts):
        z = jnp.dot(hb, w_ref[:, off:off + n], preferred_element_type=F32)
        if act == "silu":
            z = _silu(z)
        o_ref[0] = z.astype(o_ref.dtype)
        off += n


def in_matmul(x, mod, norm_g, w, splits, acts, n_ctx, tb):
    B, Tt, D = x.shape
    ncb = n_ctx // tb
    n = w.shape[1]
    assert sum(splits) == n
    kern = functools.partial(_in_mm_kernel, splits=tuple(splits), acts=tuple(acts))
    return pl.pallas_call(
        kern,
        grid=(B, Tt // tb),
        in_specs=[pl.BlockSpec((1, tb, D), lambda b, t: (b, t, 0)),
                  pl.BlockSpec((1, 1, 2, D), lambda b, t: (b, _seg_of_block(t, ncb), 0, 0)),
                  pl.BlockSpec((1, D), lambda b, t: (0, 0)),
                  pl.BlockSpec((D, n), lambda b, t: (0, 0))],
        out_specs=[pl.BlockSpec((1, tb, s), lambda b, t: (b, t, 0)) for s in splits],
        out_shape=[jax.ShapeDtypeStruct((B, Tt, s), F32) for s in splits],
        compiler_params=_cparams(2),
        name="in_matmul",
    )(x, mod, norm_g.reshape(1, D), _bf(w))


def _out_mm_kernel(*refs, n_a):
    a_refs = refs[:n_a]
    w_ref, x_ref, gate_ref, o_ref = refs[n_a:]
    a = a_refs[0][0]
    for r in a_refs[1:]:
        a = a * r[0]
    o_ref[0] = x_ref[0] + gate_ref[0, 0] * _dot(a, w_ref[...])


def out_matmul(a_list, w, x, gate, n_ctx, tb, latent_only=False):
    B, Tt, D = x.shape
    K = w.shape[0]
    ncb = n_ctx // tb
    skip = ncb if latent_only else 0
    kern = functools.partial(_out_mm_kernel, n_a=len(a_list))
    return pl.pallas_call(
        kern,
        grid=(B, Tt // tb - skip),
        in_specs=[pl.BlockSpec((1, tb, K), lambda b, t: (b, t + skip, 0)) for _ in a_list] + [
            pl.BlockSpec((K, D), lambda b, t: (0, 0)),
            pl.BlockSpec((1, tb, D), lambda b, t: (b, t + skip, 0)),
            pl.BlockSpec((1, 1, 1, D), lambda b, t: (b, _seg_of_block(t + skip, ncb), 0, 0))],
        out_specs=pl.BlockSpec((1, tb, D), lambda b, t: (b, t, 0)),
        out_shape=jax.ShapeDtypeStruct((B, Tt - skip * tb, D), F32),
        compiler_params=_cparams(2),
        name="out_matmul",
    )(*a_list, _bf(w), x, gate)


_V_NORM_G, _V_MU, _V_KK, _V_KA, _V_B0, _V_RK = 0, 1, 7, 8, 9, 13
_V_ROWS = 16


def _head_sum(x, sel_ref, selt_ref):
    mm = lambda a, w: jnp.dot(a, w, preferred_element_type=F32)
    hi = _bf(x)
    lo = _bf(x - hi.astype(F32))
    s = mm(jnp.concatenate([hi, lo], axis=1), sel_ref[...])
    s_hi = _bf(s)
    s_lo = _bf(s - s_hi.astype(F32))
    return mm(jnp.concatenate([s_hi, s_lo], axis=1), selt_ref[...])


def _head_selectors(D):
    hid = jnp.arange(D) // HEAD_DIM
    sel = (hid[:, None] == jnp.arange(LANES)[None, :]).astype(BF16)
    return jnp.concatenate([sel, sel], axis=0), jnp.concatenate([sel.T, sel.T], axis=0)


def _rwkv_proj_kernel(x_ref, xp_ref, xn_ref, mod_ref, vec_ref, win_ref, dw_ref, da_ref, upw_ref,
                      upa_ref, sel_ref, selt_ref, r_ref, v_ref, g_ref, kk_ref, bonus_ref, lw0_ref,
                      a0_ref, k0_ref, lw1_ref, a1_ref, k1_ref, *, n_ctx_blocks):
    t = pl.program_id(1)
    nb = pl.num_programs(1)
    vec = lambda i: vec_ref[i:i + 1, :]
    scale1, shift = mod_ref[0, 0, 0:1], mod_ref[0, 0, 1:2]
    g = vec(_V_NORM_G)
    h = _norm_mod(x_ref[0], g, scale1, shift)
    first = jnp.logical_or(t == 0, t == n_ctx_blocks)
    last = jnp.logical_or(t == n_ctx_blocks - 1, t == nb - 1)
    hp = _norm_mod(xp_ref[0], g, scale1, shift)[SUBLANES - 1:SUBLANES]
    hn = _norm_mod(xn_ref[0], g, scale1, shift)[0:1]
    hp = jnp.where(first, 0.0, hp)
    hn = jnp.where(last, 0.0, hn)
    h_prev = _shift_down(h, 1, jnp.broadcast_to(hp, h.shape))
    h_next = _shift_up(h, 1, jnp.broadcast_to(hn, h.shape))
    xx = 0.5 * (h_prev + h_next) - h
    mix = lambda j: _bf(h + xx * vec(_V_MU + j))
    mm = lambda a, w: jnp.dot(a, w, preferred_element_type=F32)
    xr, xw, xk, xv, xa, xg = [mix(j) for j in range(6)]
    r = mm(xr, win_ref[0])
    k = mm(xk, win_ref[1])
    v = mm(xv, win_ref[2])
    r_ref[0] = r
    v_ref[0] = v
    g_ref[0] = _silu(mm(xg, win_ref[3]))
    kk_raw = k * vec(_V_KK)
    kk_ref[0] = kk_raw * jnp.minimum(lax.rsqrt(_head_sum(kk_raw * kk_raw, sel_ref, selt_ref)), 1e12)
    tw = _bf(jnp.tanh(mm(xw, dw_ref[...])))
    ta = _bf(mm(xa, da_ref[...]))
    kka = k * vec(_V_KA)
    k_rest = k - kka
    k_sum = None
    for d, (lw_ref, a_ref, kd_ref) in enumerate(((lw0_ref, a0_ref, k0_ref), (lw1_ref, a1_ref, k1_ref))):
        zw = vec(_V_B0 + 2 * d) + mm(tw, upw_ref[d])
        lw_ref[0] = -DECAY_SCALE * _sigmoid(zw)
        a_d = _sigmoid(vec(_V_B0 + 2 * d + 1) + mm(ta, upa_ref[d]))
        a_ref[0] = a_d
        k_d = k_rest + kka * a_d
        kd_ref[0] = k_d
        k_sum = k_d if k_sum is None else k_sum + k_d
    bonus_ref[0] = _head_sum(r * k_sum * vec(_V_RK), sel_ref, selt_ref) * v


def rwkv_proj(x, mod, p, n_ctx, tb):
    B, Tt, D = x.shape
    lora = p['lora_down'].shape[-1]
    ncb = n_ctx // tb
    nb8 = Tt // SUBLANES
    r8 = tb // SUBLANES
    vec = jnp.zeros((_V_ROWS, D), F32)
    vec = vec.at[_V_NORM_G].set(p['norm_g']).at[_V_MU:_V_MU + 6].set(p['mu'])
    vec = vec.at[_V_KK].set(p['k_ka'][0]).at[_V_KA].set(p['k_ka'][1])
    vec = vec.at[_V_B0:_V_B0 + 4].set(p['lora_b0'].reshape(4, D)).at[_V_RK].set(p['r_k'].reshape(D))
    sel, selt = _head_selectors(D)
    down, up = p['lora_down'], p['lora_up']
    dw = _bf(jnp.concatenate([down[0, 0], down[1, 0]], axis=1))
    da = _bf(jnp.concatenate([down[0, 1], down[1, 1]], axis=1))
    z = jnp.zeros((lora, D), F32)
    upw = _bf(jnp.stack([jnp.concatenate([up[0, 0], z], 0), jnp.concatenate([z, up[1, 0]], 0)]))
    upa = _bf(jnp.stack([jnp.concatenate([up[0, 1], z], 0), jnp.concatenate([z, up[1, 1]], 0)]))
    full = lambda shape: pl.BlockSpec(shape, lambda b, t: (0,) * len(shape))
    tok = pl.BlockSpec((1, tb, D), lambda b, t: (b, t, 0))
    kern = functools.partial(_rwkv_proj_kernel, n_ctx_blocks=ncb)
    outs = pl.pallas_call(
        kern,
        grid=(B, Tt // tb),
        in_specs=[tok,
                  pl.BlockSpec((1, SUBLANES, D), lambda b, t: (b, jnp.maximum(t * r8 - 1, 0), 0)),
                  pl.BlockSpec((1, SUBLANES, D), lambda b, t: (b, jnp.minimum((t + 1) * r8, nb8 - 1), 0)),
                  pl.BlockSpec((1, 1, 2, D), lambda b, t: (b, _seg_of_block(t, ncb), 0, 0)),
                  full((_V_ROWS, D)), full((4, D, D)), full((D, 2 * lora)), full((D, 2 * lora)),
                  full((2, 2 * lora, D)), full((2, 2 * lora, D)), full((2 * D, LANES)), full((2 * LANES, D))],
        out_specs=[tok] * 11,
        out_shape=[jax.ShapeDtypeStruct((B, Tt, D), F32)] * 11,
        compiler_params=_cparams(2),
        name="rwkv_proj",
    )(x, x, x, mod, vec, _bf(p['w_in']), dw, da, upw, upa, sel, selt)
    names = ('r', 'v', 'gate', 'kk', 'bonus', 'lw0', 'a0', 'k0', 'lw1', 'a1', 'k1')
    return dict(zip(names, outs))


def _level_masks(L, reverse):
    ri = lax.broadcasted_iota(jnp.int32, (L, 2 * L), 0)
    ci = lax.broadcasted_iota(jnp.int32, (L, 2 * L), 1) & (L - 1)
    if reverse:
        ri, ci = ci, ri
    masks = []
    for j in range(int(np.log2(L))):
        same = (ri >> (j + 1)) == (ci >> (j + 1))
        masks.append(same & (((ri >> j) & 1) == 1) & (((ci >> j) & 1) == 0))
    return masks


def _rwkv_chunk_pairs(rt, kap, bt, kt, v, wl, st, tri_strict, tri_incl, eye, masks, first):
    ps = range(len(rt))
    L = v[0].shape[0]
    mm = lambda a, b: jnp.dot(a, b, preferred_element_type=F32)
    mm_nt = lambda a, b: lax.dot_general(a, b, (((1,), (1,)), ((), ())), preferred_element_type=F32)
    mm_tn = lambda a, b: lax.dot_general(a, b, (((0,), (0,)), ((), ())), preferred_element_type=F32)

    def bd(x):
        z = jnp.zeros_like(x)
        return jnp.concatenate([jnp.where(first, x, z), jnp.where(first, z, x)], axis=0)

    def diag_blocks(x):
        return jnp.where(first, x[:L], x[L:])

    v_b = [_bf(v[p]) for p in ps]
    kap_b = [_bf(kap[p]) for p in ps]
    v_bd = [bd(v_b[p]) for p in ps]
    lhs = [jnp.concatenate([kap_b[p], _bf(rt[p])], axis=0) for p in ps]
    s = [mm_nt(lhs[p], jnp.concatenate([bd(_bf(bt[p])), bd(_bf(kt[p]))], axis=0)) for p in ps]
    n_b = [jnp.where(tri_strict, s[p][:L, :LANES], 0.0) for p in ps]
    a_rb = [_bf(jnp.where(tri_incl, s[p][L:, :LANES], 0.0)) for p in ps]
    n_k = [_bf(jnp.where(tri_strict, s[p][:L, LANES:], 0.0)) for p in ps]
    a_rk = [_bf(jnp.where(tri_incl, s[p][L:, LANES:], 0.0)) for p in ps]
    t = [jnp.where(eye, 1.0, 0.0) - jnp.where(masks[0], n_b[p], 0.0) for p in ps]
    for m in masks[1:]:
        t_b = [_bf(t[p]) for p in ps]
        ct = [_bf(mm(_bf(jnp.where(m, n_b[p], 0.0)), bd(t_b[p]))) for p in ps]
        t = [t[p] - mm(t_b[p], bd(ct[p])) for p in ps]
    t_b = [_bf(t[p]) for p in ps]
    nkv = [_bf(mm(n_k[p], v_bd[p])) for p in ps]
    eg = [_bf(-mm(t_b[p], jnp.concatenate([bd(nkv[p]), bd(kap_b[p])], axis=1))) for p in ps]
    e0 = [eg[p][:, :LANES] for p in ps]
    g = [eg[p][:, LANES:] for p in ps]
    y0 = [mm(jnp.concatenate([a_rb[p], a_rk[p]], axis=1), jnp.concatenate([bd(e0[p]), v_bd[p]], axis=0))
          for p in ps]
    qp = [rt[p] + mm(a_rb[p], bd(g[p])) for p in ps]
    btw = [_bf(bt[p] * wl[p]) for p in ps]
    ktw = [_bf(kt[p] * wl[p]) for p in ps]
    zeros = jnp.zeros_like(v_b[0])
    up = [mm_tn(jnp.concatenate([btw[p], ktw[p]], axis=0),
                jnp.concatenate([eg[p], jnp.concatenate([v_b[p], zeros], axis=1)], axis=0)) for p in ps]
    ut = [diag_blocks(up[p][:, :LANES]) for p in ps]
    pt = [jnp.where(eye, wl[p], 0.0) + diag_blocks(up[p][:, LANES:]) for p in ps]
    ys = [mm(_bf(jnp.concatenate([qp[p], pt[p]], axis=0)), bd(_bf(st[p]))) for p in ps]
    return [ys[p][:L] + y0[p] for p in ps], [ys[p][L:] + ut[p] for p in ps]


def _rwkv_scan_kernel(*refs, reverse, finish, n_heads, bb):
    if finish:
        (r_ref, lw_ref, k_ref, kk_ref, a_ref, v_ref, yf_ref, bonus_ref, gate_ref, vec_ref,
         o_ref, st_ref) = refs
    else:
        r_ref, lw_ref, k_ref, kk_ref, a_ref, v_ref, o_ref, st_ref = refs

    @pl.when(pl.program_id(1) == 0)
    def _():
        st_ref[...] = jnp.zeros_like(st_ref)

    L = r_ref.shape[1]
    hd = HEAD_DIM
    assert L == hd and 2 * hd == LANES
    n_pairs = n_heads // 2
    ri2 = lax.broadcasted_iota(jnp.int32, (L, LANES), 0)
    lane = lax.broadcasted_iota(jnp.int32, (L, LANES), 1)
    ci2 = lane & (L - 1)
    if reverse:
        tri_incl, tri_strict = ci2 >= ri2, ci2 > ri2
    else:
        tri_incl, tri_strict = ci2 <= ri2, ci2 < ri2
    eye = ri2 == ci2
    first = lane < hd
    masks = _level_masks(L, reverse)
    end = 0 if reverse else L - 1

    sls = [slice(p * LANES, (p + 1) * LANES) for p in range(n_pairs)]
    split = lambda x: [x[:, sl] for sl in sls]
    rt, kap, bt, kt, vs, wl, st = [], [], [], [], [], [], []
    for i in range(bb):
        lw = lw_ref[i]
        cum = _cumsum_rows(lw, reverse)
        w_t = jnp.exp(cum)
        w_i = jnp.exp(-cum)
        w_p = jnp.exp(cum - lw)
        kk = kk_ref[i]
        rt += split(r_ref[i] * w_t)
        kap += split(kk * w_p)
        bt += split(kk * a_ref[i] * w_i)
        kt += split(k_ref[i] * w_i)
        vs += split(v_ref[i])
        wl += split(w_t[end:end + 1])
        st += split(st_ref[i])
    outs, states = _rwkv_chunk_pairs(rt, kap, bt, kt, vs, wl, st, tri_strict, tri_incl, eye, masks, first)
    lrow = lax.broadcasted_iota(jnp.int32, (LANES, LANES), 0)
    lcol = lax.broadcasted_iota(jnp.int32, (LANES, LANES), 1)
    ones = jnp.where((lrow < hd) == (lcol < hd), 1.0, 0.0).astype(BF16)
    lane_sum = lambda x: jnp.dot(x, ones, preferred_element_type=F32)
    for i in range(bb):
        mine = slice(i * n_pairs, (i + 1) * n_pairs)
        st_ref[i] = jnp.concatenate(states[mine], axis=1)
        out = jnp.concatenate(outs[mine], axis=1)
        if finish:
            ys = jnp.concatenate(split(out + yf_ref[i]), axis=0)
            ys_hi = _bf(ys)
            mean = (lane_sum(ys_hi) + lane_sum(_bf(ys - ys_hi.astype(F32)))) * (1.0 / hd)
            cen = ys - mean
            var = lane_sum(_bf(cen * cen)) * (1.0 / hd)
            yn = cen * lax.rsqrt(var + RW_GN_EPS)
            out = jnp.concatenate([yn[p * L:(p + 1) * L] for p in range(n_pairs)], axis=1)
            out = (out * vec_ref[0:1] + vec_ref[1:2] + bonus_ref[i]) * gate_ref[i]
        o_ref[i] = out.astype(o_ref.dtype)


def rwkv_scan(pr, d, n_ctx, reverse, finish=None):
    r = pr['r']
    B, Tt, D = r.shape
    L = CHUNK
    nc, ncc = Tt // L, n_ctx // L
    bb = next(n for n in (SCAN_BATCH, 2, 1) if B % n == 0)
    tok = pl.BlockSpec((bb, L, D), lambda b, c: (b, _scan_order(c, ncc, nc, reverse), 0))
    ins = [r, pr['lw%d' % d], pr['k%d' % d], pr['kk'], pr['a%d' % d], pr['v']]
    specs = [tok] * 6
    if finish is not None:
        y_fwd, gn = finish
        vec = jnp.zeros((SUBLANES, D), F32).at[0].set(gn[0]).at[1].set(gn[1])
        ins += [y_fwd, pr['bonus'], pr['gate'], vec]
        specs += [tok, tok, tok, pl.BlockSpec((SUBLANES, D), lambda b, c: (0, 0))]
    kern = functools.partial(_rwkv_scan_kernel, reverse=reverse, finish=finish is not None,
                             n_heads=D // HEAD_DIM, bb=bb)
    return pl.pallas_call(
        kern,
        grid=(B // bb, nc),
        in_specs=specs,
        out_specs=tok,
        out_shape=jax.ShapeDtypeStruct((B, Tt, D), F32 if finish is None else BF16),
        scratch_shapes=[pltpu.VMEM((bb, HEAD_DIM, D), F32)],
        compiler_params=_cparams(2),
        name="rwkv_scan_rev" if reverse else "rwkv_scan_fwd",
    )(*ins)


def rwkv_layer(x, mod, gate, p, n_ctx, tb, latent_only=False):
    pr = rwkv_proj(x, mod, p, n_ctx, tb)
    y_fwd = rwkv_scan(pr, 0, n_ctx, reverse=False)
    z = rwkv_scan(pr, 1, n_ctx, reverse=True, finish=(y_fwd, p['gn']))
    return out_matmul([z], p['w_out'], x, gate, n_ctx, tb, latent_only)


GATE_COLS = 256


def _gate_tiles(W, bd):
    assert bd <= LANES and W % LANES == 0
    win = min(W, GATE_COLS + 2 * LANES)
    tiles = []
    for c0 in range(0, W, GATE_COLS):
        lo = min(max(c0 - LANES, 0), W - win)
        tiles.append((c0, min(GATE_COLS, W - c0), lo))
    return win, tiles


def _rglru_scan_kernel(*refs, reverse, finish, n_ctx_blocks, block_dim):
    if finish:
        (x_ref, xp_ref, xn_ref, cw_ref, vec_ref, wg_ref, hf_ref, gs_ref, o_ref,
         a_scr, b_scr, carry_ref) = refs
    else:
        x_ref, xp_ref, xn_ref, cw_ref, vec_ref, wg_ref, o_ref, a_scr, b_scr, carry_ref = refs
    c = pl.program_id(1)
    nb = pl.num_programs(1)
    t = _scan_order(c, n_ctx_blocks, nb, reverse)

    @pl.when(c == 0)
    def _():
        carry_ref[...] = jnp.zeros_like(carry_ref)

    x = x_ref[0]
    tb, W = x.shape
    first = jnp.logical_or(t == 0, t == n_ctx_blocks)
    last = jnp.logical_or(t == n_ctx_blocks - 1, t == nb - 1)
    xp = jnp.where(first, 0.0, xp_ref[0])
    xn = jnp.where(last, 0.0, xn_ref[0])
    row = lambda a, i: jnp.broadcast_to(a[i:i + 1], x.shape)
    rows = _rows(x.shape)
    x_m1 = _shift_down(x, 1, row(xp, SUBLANES - 1))
    x_m2 = _shift_down(x, 2, jnp.where(rows == 0, row(xp, SUBLANES - 2), row(xp, SUBLANES - 1)))
    x_p1 = _shift_up(x, 1, row(xn, 0))
    cw = lambda i: cw_ref[i:i + 1, :]
    xc = cw(4) + x_m2 * cw(0) + x_m1 * cw(1) + x * cw(2) + x_p1 * cw(3)
    win, tiles = _gate_tiles(W, block_dim)
    xc_b = _bf(xc)
    z = [jnp.dot(xc_b[:, lo:lo + win], wg_ref[j], preferred_element_type=F32)
         for j, (_, _, lo) in enumerate(tiles)]
    gate_r = jnp.concatenate([z[j][:, :cw_] for j, (_, cw_, _) in enumerate(tiles)], axis=1)
    gate_i = jnp.concatenate([z[j][:, GATE_COLS:GATE_COLS + cw_] for j, (_, cw_, _) in enumerate(tiles)],
                             axis=1)
    r = _sigmoid(gate_r + vec_ref[0:1])
    i = _sigmoid(gate_i + vec_ref[1:2])
    log_a = -LRU_C * r * _softplus(-vec_ref[2:3])
    a = jnp.exp(log_a)
    a_scr[...] = a
    b_scr[...] = jnp.sqrt(1.0 - a * a) * (i * xc)

    S = SUBLANES
    ng = tb // S
    sub = _rows((S, W))

    def group(gi, carry):
        g = (ng - 1 - gi) if reverse else gi
        a = a_scr[pl.ds(pl.multiple_of(g * S, S), S), :]
        b = b_scr[pl.ds(pl.multiple_of(g * S, S), S), :]
        for s in (1, 2, 4):
            if reverse:
                ok = sub < S - s
                a_s = jnp.where(ok, pltpu.roll(a, S - s, 0), 1.0)
                b_s = jnp.where(ok, pltpu.roll(b, S - s, 0), 0.0)
            else:
                ok = sub >= s
                a_s = jnp.where(ok, pltpu.roll(a, s, 0), 1.0)
                b_s = jnp.where(ok, pltpu.roll(b, s, 0), 0.0)
            b = a * b_s + b
            a = a * a_s
        h = b + a * carry
        b_scr[pl.ds(pl.multiple_of(g * S, S), S), :] = h
        e = 0 if reverse else S - 1
        return jnp.broadcast_to(h[e:e + 1], (S, W))

    carry_ref[...] = lax.fori_loop(0, ng, group, carry_ref[...])
    h_all = b_scr[...]
    if finish:
        h_all = (h_all + hf_ref[0]) * gs_ref[0]
    o_ref[0] = h_all.astype(o_ref.dtype)


def rglru_scan(xr, p, d, n_ctx, tb, reverse, finish=None):
    B, Tt, W = xr.shape
    nblk = p['gate_w'].shape[2]
    bd = W // nblk
    ncb = n_ctx // tb
    nb = Tt // tb
    nb8 = Tt // SUBLANES
    r8 = tb // SUBLANES

    def dense(wb):
        eye = jnp.eye(nblk, dtype=wb.dtype)
        return jnp.einsum('ncd,nm->ncmd', wb, eye).reshape(W, W)

    win, tiles = _gate_tiles(W, bd)
    w_r, w_i = dense(p['gate_w'][d, 0]), dense(p['gate_w'][d, 1])
    pad = lambda m: jnp.pad(m, ((0, 0), (0, GATE_COLS - m.shape[1])))
    wg = _bf(jnp.stack([jnp.concatenate([pad(w_r[lo:lo + win, c0:c0 + cw_]), pad(w_i[lo:lo + win, c0:c0 + cw_])],
                                        axis=1) for c0, cw_, lo in tiles]))
    cw = jnp.zeros((SUBLANES, W), F32).at[0:4].set(p['conv_w']).at[4].set(p['conv_b'])
    vec = jnp.zeros((SUBLANES, W), F32).at[0:2].set(p['gate_b'][d]).at[2].set(p['lam'][d])
    order = lambda c: _scan_order(c, ncb, nb, reverse)
    tok = pl.BlockSpec((1, tb, W), lambda b, c: (b, order(c), 0))
    full = lambda shape: pl.BlockSpec(shape, lambda b, c: (0,) * len(shape))
    ins = [xr, xr, xr, cw, vec, wg]
    specs = [tok,
             pl.BlockSpec((1, SUBLANES, W), lambda b, c: (b, jnp.maximum(order(c) * r8 - 1, 0), 0)),
             pl.BlockSpec((1, SUBLANES, W), lambda b, c: (b, jnp.minimum((order(c) + 1) * r8, nb8 - 1), 0)),
             full((SUBLANES, W)), full((SUBLANES, W)), full((len(tiles), win, 2 * GATE_COLS))]
    if finish is not None:
        ins += list(finish)
        specs += [tok, tok]
    kern = functools.partial(_rglru_scan_kernel, reverse=reverse, finish=finish is not None,
                             n_ctx_blocks=ncb, block_dim=bd)
    return pl.pallas_call(
        kern,
        grid=(B, nb),
        in_specs=specs,
        out_specs=tok,
        out_shape=jax.ShapeDtypeStruct((B, Tt, W), F32 if finish is None else BF16),
        scratch_shapes=[pltpu.VMEM((tb, W), F32), pltpu.VMEM((tb, W), F32), pltpu.VMEM((SUBLANES, W), F32)],
        compiler_params=_cparams(2),
        name="rglru_scan_rev" if reverse else "rglru_scan_fwd",
    )(*ins)


def rglru_layer(x, mod, gate, p, n_ctx, tb, latent_only=False):
    W = p['conv_w'].shape[1]
    xr, gs = in_matmul(x, mod, p['norm_g'], p['w_in'], (W, W), (None, "silu"), n_ctx, tb)
    h_fwd = rglru_scan(xr, p, 0, n_ctx, tb, reverse=False)
    z = rglru_scan(xr, p, 1, n_ctx, tb, reverse=True, finish=(h_fwd, gs))
    return out_matmul([z], p['w_out'], x, gate, n_ctx, tb, latent_only)


def _natten_in_kernel(x_ref, mod_ref, ng_ref, w_ref, cs_ref, g_ref, sel_ref, selt_ref,
                      qn_ref, qr_ref, kr_ref, vb_ref, gs_ref):
    D = x_ref.shape[2]
    hb = _bf(_norm_mod(x_ref[0], ng_ref[...], mod_ref[0, 0, 0:1], mod_ref[0, 0, 1:2]))
    proj = lambda j: jnp.dot(hb, w_ref[:, j * D:(j + 1) * D], preferred_element_type=F32)
    reps = D // cs_ref.shape[2]
    cos = jnp.concatenate([cs_ref[0]] * reps, axis=1)
    sin = jnp.concatenate([cs_ref[1]] * reps, axis=1)
    lane = lax.broadcasted_iota(jnp.int32, cos.shape, 1)
    quarter = HEAD_DIM // 4
    low = (lane % (2 * quarter)) < quarter

    def prep(x, g):
        ms = _head_sum(x * x, sel_ref, selt_ref) * (1.0 / HEAD_DIM)
        xn = x * lax.rsqrt(ms + RMS_EPS) * g
        partner = jnp.where(low, pltpu.roll(xn, D - quarter, 1), pltpu.roll(xn, quarter, 1))
        return xn, xn * cos + partner * sin

    qn, qr = prep(proj(0), g_ref[0:1])
    _, kr = prep(proj(1), g_ref[1:2])
    scale = HEAD_DIM ** -0.5
    qn_ref[0] = _bf(qn * scale)
    qr_ref[0] = _bf(qr * scale)
    kr_ref[0] = _bf(kr)
    vb_ref[0] = _bf(proj(2))
    gs_ref[0] = _silu(proj(3))


def natten_in(x, mod, norm_g, w_in, qk_g, n_ctx, tb):
    B, Tt, D = x.shape
    H = D // HEAD_DIM
    T = Tt - n_ctx
    ncb = n_ctx // tb
    nfreq = HEAD_DIM // 4
    pos = jnp.arange(T)
    inv = ROPE_THETA ** (-jnp.arange(nfreq, dtype=F32) / nfreq)
    ang_r = (pos // GRID_W).astype(F32)[:, None] * inv
    ang_c = (pos % GRID_W).astype(F32)[:, None] * inv
    cos = jnp.concatenate([jnp.cos(ang_r)] * 2 + [jnp.cos(ang_c)] * 2, axis=1)
    sin = jnp.concatenate([-jnp.sin(ang_r), jnp.sin(ang_r), -jnp.sin(ang_c), jnp.sin(ang_c)], axis=1)
    cs = jnp.stack([jnp.concatenate([jnp.ones((n_ctx, HEAD_DIM), F32), cos], 0),
                    jnp.concatenate([jnp.zeros((n_ctx, HEAD_DIM), F32), sin], 0)])
    cs = jnp.concatenate([cs, cs], axis=2)
    g = jnp.zeros((SUBLANES, D), F32).at[0].set(jnp.tile(qk_g[0], H)).at[1].set(jnp.tile(qk_g[1], H))
    sel, selt = _head_selectors(D)
    tok = pl.BlockSpec((1, tb, D), lambda b, t: (b, t, 0))
    full = lambda shape: pl.BlockSpec(shape, lambda b, t: (0,) * len(shape))
    return pl.pallas_call(
        _natten_in_kernel,
        grid=(B, Tt // tb),
        in_specs=[tok,
                  pl.BlockSpec((1, 1, 2, D), lambda b, t: (b, _seg_of_block(t, ncb), 0, 0)),
                  full((1, D)), full((D, 4 * D)),
                  pl.BlockSpec((2, tb, 2 * HEAD_DIM), lambda b, t: (0, t, 0)),
                  full((SUBLANES, D)), full((2 * D, LANES)), full((2 * LANES, D))],
        out_specs=[tok] * 5,
        out_shape=[jax.ShapeDtypeStruct((B, Tt, D), BF16)] * 4 + [jax.ShapeDtypeStruct((B, Tt, D), F32)],
        compiler_params=_cparams(2),
        name="natten_in",
    )(x, mod, norm_g.reshape(1, D), _bf(w_in), cs, g, sel, selt)


def _natten_kernel(qr_ref, qn_ref, kr_ref, v_ref, bias_ref, o_ref, *, rows, kh, rb, n_ctx):
    gw = GRID_W
    lane = lax.broadcasted_iota(jnp.int32, (gw, 2 * HEAD_DIM), 1)
    head_lanes = [lane < HEAD_DIM, lane >= HEAD_DIM]
    kc = kr_ref[0, 0:n_ctx, :]
    vc = v_ref[0, 0:n_ctx, :]
    nt = lambda a, b: lax.dot_general(a, b, (((1,), (1,)), ((), ())), preferred_element_type=F32)
    mm = lambda a, b: jnp.dot(a, b, preferred_element_type=F32)

    q_c = qn_ref[0, 0:n_ctx, :]
    lane_c = lax.broadcasted_iota(jnp.int32, q_c.shape, 1)
    o_c = []
    for h in range(2):
        mine = (lane_c >= HEAD_DIM) if h else (lane_c < HEAD_DIM)
        s = nt(jnp.where(mine, q_c, jnp.zeros_like(q_c)), kc)
        p = jnp.exp(s - s.max(axis=-1, keepdims=True))
        o_c.append(mm(_bf(p), vc) / p.sum(axis=-1, keepdims=True))
    o_ref[0, 0:n_ctx, :] = jnp.where(lane_c < HEAD_DIM, o_c[0], o_c[1])

    def row_group(g, carry):
        q0s, qrs, qns, kbs, vbs, biases = [], [], [], [], [], []
        for j in range(rb):
            r = g * rb + j
            start = jnp.clip(r - kh // 2, 0, rows - kh)
            d0 = start - r + kh - 1
            q0 = pl.multiple_of(n_ctx + r * gw, gw)
            k0 = pl.multiple_of(n_ctx + start * gw, gw)
            qr = qr_ref[0, pl.ds(q0, gw), :]
            qn = qn_ref[0, pl.ds(q0, gw), :]
            zero = jnp.zeros_like(qr)
            by_head = lambda q: jnp.concatenate([jnp.where(m, q, zero) for m in head_lanes], axis=0)
            q0s.append(q0)
            qrs.append(by_head(qr))
            qns.append(by_head(qn))
            kbs.append(kr_ref[0, pl.ds(k0, kh * gw), :])
            vbs.append(v_ref[0, pl.ds(k0, kh * gw), :])
            biases.append(jnp.concatenate([bias_ref[0, d0], bias_ref[1, d0]], axis=0))
        n = range(rb)
        s_band = [nt(qrs[i], kbs[i]) + biases[i] for i in n]
        s_ctx = [nt(qns[i], kc) for i in n]
        m = [jnp.maximum(s_band[i].max(axis=-1, keepdims=True), s_ctx[i].max(axis=-1, keepdims=True))
             for i in n]
        p_band = [jnp.exp(s_band[i] - m[i]) for i in n]
        p_ctx = [jnp.exp(s_ctx[i] - m[i]) for i in n]
        den = [p_band[i].sum(axis=-1, keepdims=True) + p_ctx[i].sum(axis=-1, keepdims=True) for i in n]
        o = [(mm(_bf(p_band[i]), vbs[i]) + mm(_bf(p_ctx[i]), vc)) / den[i] for i in n]
        for i in n:
            o_ref[0, pl.ds(q0s[i], gw), :] = jnp.where(head_lanes[0], o[i][:gw], o[i][gw:])
        return carry

    lax.fori_loop(0, rows // rb, row_group, 0)


def _natten_bias(rpb, rows, kh):
    cols = np.arange(GRID_W)
    c_start = np.clip(cols - WIN_W // 2, 0, GRID_W - WIN_W)
    col_ok = (cols[None, :] >= c_start[:, None]) & (cols[None, :] < c_start[:, None] + WIN_W)
    dc_idx = np.clip(cols[None, :] - cols[:, None] + WIN_W - 1, 0, 2 * WIN_W - 2)
    by_dr = jnp.where(jnp.asarray(col_ok), rpb[:, :, dc_idx].astype(F32), NEG_BIG)
    base = WIN_H - kh
    per_v = [jnp.concatenate([by_dr[:, base + v + i] for i in range(kh)], axis=-1) for v in range(kh)]
    return jnp.stack(per_v, axis=1)


def natten_attention(qr, qn, kr, vb, rpb, n_ctx):
    B, Tt, D = qr.shape
    T = Tt - n_ctx
    rows = T // GRID_W
    kh = min(WIN_H, rows)
    HP = D // (2 * HEAD_DIM)
    assert n_ctx % GRID_W == 0
    bias = _natten_bias(rpb, rows, kh)
    seq = pl.BlockSpec((1, Tt, 2 * HEAD_DIM), lambda hp, b: (b, 0, hp))
    rb = 8 if rows % 8 == 0 else 1
    kern = functools.partial(_natten_kernel, rows=rows, kh=kh, rb=rb, n_ctx=n_ctx)
    return pl.pallas_call(
        kern,
        grid=(HP, B),
        in_specs=[seq, seq, seq, seq,
                  pl.BlockSpec((2, kh, GRID_W, kh * GRID_W), lambda hp, b: (hp, 0, 0, 0))],
        out_specs=seq,
        out_shape=jax.ShapeDtypeStruct((B, Tt, D), F32),
        compiler_params=_cparams(2),
        name="natten",
    )(qr, qn, kr, vb, bias)


def natten_layer(x, mod, gate, p, n_ctx, tb, latent_only=False):
    qn, qr, kr, vb, gs = natten_in(x, mod, p['norm_g'], p['w_in'], p['qk_g'], n_ctx, tb)
    o = natten_attention(qr, qn, kr, vb, p['rpb'], n_ctx)
    return out_matmul([o, gs], p['w_out'], x, gate, n_ctx, tb, latent_only)


_LAYER_KEYS = (
    ('norm_g', 'ada_w', 'ada_b', 'w_in', 'mu', 'lora_b0', 'lora_down', 'lora_up', 'k_ka', 'r_k', 'gn', 'w_out'),
    ('norm_g', 'ada_w', 'ada_b', 'w_in', 'conv_w', 'conv_b', 'gate_w', 'gate_b', 'lam', 'w_out'),
    ('norm_g', 'ada_w', 'ada_b', 'w_in', 'qk_g', 'rpb', 'w_out'),
)
_LAYERS = (rwkv_layer, rglru_layer, natten_layer)


def _forward(x, c, ctx, c_ctx, layer_params, tb):
    B, T, D = x.shape
    n_ctx = ctx.shape[1]
    xs = jnp.concatenate([ctx, x], axis=1)
    m_rows = -(-(B + 1) // SUBLANES) * SUBLANES
    c_all = jnp.zeros((m_rows, D), F32).at[:B].set(c).at[B].set(c_ctx)
    for i, p in enumerate(layer_params):
        m = ada_mod(c_all, p['ada_w'], p['ada_b'])
        m_l = m[:B].reshape(B, 3, D)
        m_c = jnp.broadcast_to(m[B].reshape(1, 3, D), (B, 3, D))
        both = jnp.stack([m_c, m_l], axis=1)
        mod = jnp.stack([1.0 + both[:, :, 1], both[:, :, 0]], axis=2)
        gate = both[:, :, 2:3]
        xs = _LAYERS[i % 3](xs, mod, gate, p, n_ctx, tb, latent_only=i == len(layer_params) - 1)
    return xs


def kernel(x, c, ctx, c_ctx, l0_norm_g, l0_ada_w, l0_ada_b, l0_w_in, l0_mu, l0_lora_b0, l0_lora_down, l0_lora_up, l0_k_ka, l0_r_k, l0_gn, l0_w_out, l1_norm_g, l1_ada_w, l1_ada_b, l1_w_in, l1_conv_w, l1_conv_b, l1_gate_w, l1_gate_b, l1_lam, l1_w_out, l2_norm_g, l2_ada_w, l2_ada_b, l2_w_in, l2_qk_g, l2_rpb, l2_w_out, l3_norm_g, l3_ada_w, l3_ada_b, l3_w_in, l3_mu, l3_lora_b0, l3_lora_down, l3_lora_up, l3_k_ka, l3_r_k, l3_gn, l3_w_out):
    args = (l0_norm_g, l0_ada_w, l0_ada_b, l0_w_in, l0_mu, l0_lora_b0, l0_lora_down, l0_lora_up, l0_k_ka, l0_r_k, l0_gn, l0_w_out, l1_norm_g, l1_ada_w, l1_ada_b, l1_w_in, l1_conv_w, l1_conv_b, l1_gate_w, l1_gate_b, l1_lam, l1_w_out, l2_norm_g, l2_ada_w, l2_ada_b, l2_w_in, l2_qk_g, l2_rpb, l2_w_out, l3_norm_g, l3_ada_w, l3_ada_b, l3_w_in, l3_mu, l3_lora_b0, l3_lora_down, l3_lora_up, l3_k_ka, l3_r_k, l3_gn, l3_w_out)
    layer_params, pos = [], 0
    for i in range(4):
        keys = _LAYER_KEYS[i % 3]
        layer_params.append(dict(zip(keys, args[pos:pos + len(keys)])))
        pos += len(keys)
    return _forward(x, c, ctx, c_ctx, layer_params, tb=256)
```

```python
import functools

import jax
import jax.numpy as jnp
import numpy as np
from jax import lax
from jax.experimental import pallas as pl
from jax.experimental.pallas import tpu as pltpu

F32 = jnp.float32
BF16 = jnp.bfloat16

HEAD_DIM = 64
CHUNK = 64
SCAN_BATCH = 4
GRID_W = 64
WIN_H = 8
WIN_W = 16
ROPE_THETA = 10000.0
RMS_EPS = 1e-6
RW_GN_EPS = 64e-5
LRU_C = 8.0
DECAY_SCALE = float(np.exp(-0.5))
SUBLANES = 8
LANES = 128
NEG_BIG = -1e30
VMEM_LIMIT = 56 * 1024 * 1024
MM_ROWS_MAX = 544


def _cparams(n_axes):
    return pltpu.CompilerParams(
        dimension_semantics=("arbitrary",) * n_axes, vmem_limit_bytes=VMEM_LIMIT)


def _bf(x):
    return x.astype(BF16)


def _dot(a, b):
    return jnp.dot(_bf(a), _bf(b), preferred_element_type=F32)


def _cumsum_rows(x, reverse):
    n = x.shape[0]
    rows = _rows(x.shape)
    s = 1
    while s < n:
        if reverse:
            x = x + jnp.where(rows < n - s, pltpu.roll(x, n - s, 0), 0.0)
        else:
            x = x + jnp.where(rows >= s, pltpu.roll(x, s, 0), 0.0)
        s *= 2
    return x


def _sigmoid(x):
    return 0.5 * jnp.tanh(0.5 * x) + 0.5


def _silu(x):
    return x * _sigmoid(x)


def _softplus(x):
    return jnp.maximum(x, 0.0) + jnp.log(1.0 + jnp.exp(-jnp.abs(x)))


def _rows(shape):
    return lax.broadcasted_iota(jnp.int32, shape, 0)


def _shift_down(x, s, fill):
    return jnp.where(_rows(x.shape) >= s, pltpu.roll(x, s, 0), fill)


def _shift_up(x, s, fill):
    n = x.shape[0]
    return jnp.where(_rows(x.shape) < n - s, pltpu.roll(x, n - s, 0), fill)


def _norm_mod(xb, g, scale1, shift):
    xf = xb.astype(F32)
    ms = jnp.mean(xf * xf, axis=-1, keepdims=True)
    return xf * lax.rsqrt(ms + RMS_EPS) * g * scale1 + shift


def _ada_kernel(c_ref, w_ref, b_ref, o_ref):
    o_ref[...] = _dot(_silu(c_ref[...]), w_ref[...]) + b_ref[...]


def ada_mod(c_all, ada_w, ada_b):
    m, d = c_all.shape
    n = ada_w.shape[1]
    tn = d
    return pl.pallas_call(
        _ada_kernel,
        grid=(n // tn,),
        in_specs=[pl.BlockSpec((m, d), lambda j: (0, 0)),
                  pl.BlockSpec((d, tn), lambda j: (0, j)),
                  pl.BlockSpec((1, tn), lambda j: (0, j))],
        out_specs=pl.BlockSpec((m, tn), lambda j: (0, j)),
        out_shape=jax.ShapeDtypeStruct((m, n), F32),
        compiler_params=_cparams(1),
        name="ada_mod",
    )(c_all, _bf(ada_w), ada_b.reshape(1, n))


def _seg_of_block(t, n_ctx_blocks):
    return (t >= n_ctx_blocks).astype(jnp.int32)


def _scan_order(c, n_ctx_blocks, n_blocks, reverse):
    if not reverse:
        return c
    return jnp.where(c < n_ctx_blocks, n_ctx_blocks - 1 - c, n_blocks - 1 + n_ctx_blocks - c)


def _mm_rows(Tt):
    return max(d for d in range(2 * SUBLANES, MM_ROWS_MAX + 1, 2 * SUBLANES) if Tt % d == 0)


def _seg_rows(mod_ref, t, tb, n_ctx, i):
    latent = (t * tb + lax.broadcasted_iota(jnp.int32, (tb, 1), 0)) >= n_ctx
    return jnp.where(latent, mod_ref[0, 1, i:i + 1], mod_ref[0, 0, i:i + 1])


def _in_mm_kernel(x_ref, mod_ref, g_ref, w_ref, *o_refs, splits, acts, n_ctx):
    t, tb = pl.program_id(1), x_ref.shape[1]
    h = _norm_mod(x_ref[0], g_ref[...], _seg_rows(mod_ref, t, tb, n_ctx, 0), _seg_rows(mod_ref, t, tb, n_ctx, 1))
    hb = _bf(h)
    off = 0
    for o_ref, n, act in zip(o_refs, splits, acts):
        z = jnp.dot(hb, w_ref[:, off:off + n], preferred_element_type=F32)
        if act == "silu":
            z = _silu(z)
        o_ref[0] = z.astype(o_ref.dtype)
        off += n


def in_matmul(x, mod, norm_g, w, splits, acts, n_ctx):
    B, Tt, D = x.shape
    tb = _mm_rows(Tt)
    n = w.shape[1]
    assert sum(splits) == n
    kern = functools.partial(_in_mm_kernel, splits=tuple(splits), acts=tuple(acts), n_ctx=n_ctx)
    return pl.pallas_call(
        kern,
        grid=(B, Tt // tb),
        in_specs=[pl.BlockSpec((1, tb, D), lambda b, t: (b, t, 0)),
                  pl.BlockSpec((1, 2, 2, D), lambda b, t: (b, 0, 0, 0)),
                  pl.BlockSpec((1, D), lambda b, t: (0, 0)),
                  pl.BlockSpec((D, n), lambda b, t: (0, 0))],
        out_specs=[pl.BlockSpec((1, tb, s), lambda b, t: (b, t, 0)) for s in splits],
        out_shape=[jax.ShapeDtypeStruct((B, Tt, s), F32) for s in splits],
        compiler_params=_cparams(2),
        name="in_matmul",
    )(x, mod, norm_g.reshape(1, D), _bf(w))


def _out_mm_kernel(*refs, n_a, n_ctx, skip):
    a_refs = refs[:n_a]
    w_ref, x_ref, gate_ref, o_ref = refs[n_a:]
    a = a_refs[0][0]
    for r in a_refs[1:]:
        a = a * r[0]
    gate = _seg_rows(gate_ref, pl.program_id(1) + skip, x_ref.shape[1], n_ctx, 0)
    o_ref[0] = x_ref[0] + gate * _dot(a, w_ref[...])


def out_matmul(a_list, w, x, gate, n_ctx, tb, latent_only=False):
    B, Tt, D = x.shape
    K = w.shape[0]
    if latent_only:
        skip = n_ctx // tb
    else:
        tb, skip = _mm_rows(Tt), 0
    kern = functools.partial(_out_mm_kernel, n_a=len(a_list), n_ctx=n_ctx, skip=skip)
    return pl.pallas_call(
        kern,
        grid=(B, Tt // tb - skip),
        in_specs=[pl.BlockSpec((1, tb, K), lambda b, t: (b, t + skip, 0)) for _ in a_list] + [
            pl.BlockSpec((K, D), lambda b, t: (0, 0)),
            pl.BlockSpec((1, tb, D), lambda b, t: (b, t + skip, 0)),
            pl.BlockSpec((1, 2, 1, D), lambda b, t: (b, 0, 0, 0))],
        out_specs=pl.BlockSpec((1, tb, D), lambda b, t: (b, t, 0)),
        out_shape=jax.ShapeDtypeStruct((B, Tt - skip * tb, D), F32),
        compiler_params=_cparams(2),
        name="out_matmul",
    )(*a_list, _bf(w), x, gate)


_V_NORM_G, _V_MU, _V_KK, _V_KA, _V_B0, _V_RK = 0, 1, 7, 8, 9, 13
_V_ROWS = 16


def _head_sum(x, sel_ref, selt_ref):
    mm = lambda a, w: jnp.dot(a, w, preferred_element_type=F32)
    hi = _bf(x)
    lo = _bf(x - hi.astype(F32))
    s = mm(jnp.concatenate([hi, lo], axis=1), sel_ref[...])
    s_hi = _bf(s)
    s_lo = _bf(s - s_hi.astype(F32))
    return mm(jnp.concatenate([s_hi, s_lo], axis=1), selt_ref[...])


def _head_selectors(D):
    hid = jnp.arange(D) // HEAD_DIM
    sel = (hid[:, None] == jnp.arange(LANES)[None, :]).astype(BF16)
    return jnp.concatenate([sel, sel], axis=0), jnp.concatenate([sel.T, sel.T], axis=0)


def _rwkv_proj_kernel(x_ref, xp_ref, xn_ref, mod_ref, vec_ref, win_ref, dw_ref, da_ref, upw_ref,
                      upa_ref, sel_ref, selt_ref, r_ref, v_ref, g_ref, kk_ref, bonus_ref, lw0_ref,
                      a0_ref, k0_ref, lw1_ref, a1_ref, k1_ref, *, n_ctx_blocks):
    t = pl.program_id(1)
    nb = pl.num_programs(1)
    vec = lambda i: vec_ref[i:i + 1, :]
    scale1, shift = mod_ref[0, 0, 0:1], mod_ref[0, 0, 1:2]
    g = vec(_V_NORM_G)
    h = _norm_mod(x_ref[0], g, scale1, shift)
    first = jnp.logical_or(t == 0, t == n_ctx_blocks)
    last = jnp.logical_or(t == n_ctx_blocks - 1, t == nb - 1)
    hp = _norm_mod(xp_ref[0], g, scale1, shift)[SUBLANES - 1:SUBLANES]
    hn = _norm_mod(xn_ref[0], g, scale1, shift)[0:1]
    hp = jnp.where(first, 0.0, hp)
    hn = jnp.where(last, 0.0, hn)
    h_prev = _shift_down(h, 1, jnp.broadcast_to(hp, h.shape))
    h_next = _shift_up(h, 1, jnp.broadcast_to(hn, h.shape))
    xx = 0.5 * (h_prev + h_next) - h
    mix = lambda j: _bf(h + xx * vec(_V_MU + j))
    mm = lambda a, w: jnp.dot(a, w, preferred_element_type=F32)
    xr, xw, xk, xv, xa, xg = [mix(j) for j in range(6)]
    r = mm(xr, win_ref[0])
    k = mm(xk, win_ref[1])
    v = mm(xv, win_ref[2])
    r_ref[0] = r
    v_ref[0] = v
    g_ref[0] = _silu(mm(xg, win_ref[3]))
    kk_raw = k * vec(_V_KK)
    kk_ref[0] = kk_raw * jnp.minimum(lax.rsqrt(_head_sum(kk_raw * kk_raw, sel_ref, selt_ref)), 1e12)
    tw = _bf(jnp.tanh(mm(xw, dw_ref[...])))
    ta = _bf(mm(xa, da_ref[...]))
    kka = k * vec(_V_KA)
    k_rest = k - kka
    k_sum = None
    for d, (lw_ref, a_ref, kd_ref) in enumerate(((lw0_ref, a0_ref, k0_ref), (lw1_ref, a1_ref, k1_ref))):
        zw = vec(_V_B0 + 2 * d) + mm(tw, upw_ref[d])
        lw_ref[0] = -DECAY_SCALE * _sigmoid(zw)
        a_d = _sigmoid(vec(_V_B0 + 2 * d + 1) + mm(ta, upa_ref[d]))
        a_ref[0] = a_d
        k_d = k_rest + kka * a_d
        kd_ref[0] = k_d
        k_sum = k_d if k_sum is None else k_sum + k_d
    bonus_ref[0] = _head_sum(r * k_sum * vec(_V_RK), sel_ref, selt_ref) * v


def rwkv_proj(x, mod, p, n_ctx, tb):
    B, Tt, D = x.shape
    lora = p['lora_down'].shape[-1]
    ncb = n_ctx // tb
    nb8 = Tt // SUBLANES
    r8 = tb // SUBLANES
    vec = jnp.zeros((_V_ROWS, D), F32)
    vec = vec.at[_V_NORM_G].set(p['norm_g']).at[_V_MU:_V_MU + 6].set(p['mu'])
    vec = vec.at[_V_KK].set(p['k_ka'][0]).at[_V_KA].set(p['k_ka'][1])
    vec = vec.at[_V_B0:_V_B0 + 4].set(p['lora_b0'].reshape(4, D)).at[_V_RK].set(p['r_k'].reshape(D))
    sel, selt = _head_selectors(D)
    down, up = p['lora_down'], p['lora_up']
    dw = _bf(jnp.concatenate([down[0, 0], down[1, 0]], axis=1))
    da = _bf(jnp.concatenate([down[0, 1], down[1, 1]], axis=1))
    z = jnp.zeros((lora, D), F32)
    upw = _bf(jnp.stack([jnp.concatenate([up[0, 0], z], 0), jnp.concatenate([z, up[1, 0]], 0)]))
    upa = _bf(jnp.stack([jnp.concatenate([up[0, 1], z], 0), jnp.concatenate([z, up[1, 1]], 0)]))
    full = lambda shape: pl.BlockSpec(shape, lambda b, t: (0,) * len(shape))
    tok = pl.BlockSpec((1, tb, D), lambda b, t: (b, t, 0))
    kern = functools.partial(_rwkv_proj_kernel, n_ctx_blocks=ncb)
    outs = pl.pallas_call(
        kern,
        grid=(B, Tt // tb),
        in_specs=[tok,
                  pl.BlockSpec((1, SUBLANES, D), lambda b, t: (b, jnp.maximum(t * r8 - 1, 0), 0)),
                  pl.BlockSpec((1, SUBLANES, D), lambda b, t: (b, jnp.minimum((t + 1) * r8, nb8 - 1), 0)),
                  pl.BlockSpec((1, 1, 2, D), lambda b, t: (b, _seg_of_block(t, ncb), 0, 0)),
                  full((_V_ROWS, D)), full((4, D, D)), full((D, 2 * lora)), full((D, 2 * lora)),
                  full((2, 2 * lora, D)), full((2, 2 * lora, D)), full((2 * D, LANES)), full((2 * LANES, D))],
        out_specs=[tok] * 11,
        out_shape=[jax.ShapeDtypeStruct((B, Tt, D), F32)] * 11,
        compiler_params=_cparams(2),
        name="rwkv_proj",
    )(x, x, x, mod, vec, _bf(p['w_in']), dw, da, upw, upa, sel, selt)
    names = ('r', 'v', 'gate', 'kk', 'bonus', 'lw0', 'a0', 'k0', 'lw1', 'a1', 'k1')
    return dict(zip(names, outs))


def _level_masks(L, reverse):
    ri = lax.broadcasted_iota(jnp.int32, (L, 2 * L), 0)
    ci = lax.broadcasted_iota(jnp.int32, (L, 2 * L), 1) & (L - 1)
    if reverse:
        ri, ci = ci, ri
    masks = []
    for j in range(int(np.log2(L))):
        same = (ri >> (j + 1)) == (ci >> (j + 1))
        masks.append(same & (((ri >> j) & 1) == 1) & (((ci >> j) & 1) == 0))
    return masks


def _rwkv_chunk_pairs(rt, kap, bt, kt, v, wl, st, tri_strict, tri_incl, eye, masks, first):
    ps = range(len(rt))
    L = v[0].shape[0]
    mm = lambda a, b: jnp.dot(a, b, preferred_element_type=F32)
    mm_nt = lambda a, b: lax.dot_general(a, b, (((1,), (1,)), ((), ())), preferred_element_type=F32)
    mm_tn = lambda a, b: lax.dot_general(a, b, (((0,), (0,)), ((), ())), preferred_element_type=F32)

    def bd(x):
        z = jnp.zeros_like(x)
        return jnp.concatenate([jnp.where(first, x, z), jnp.where(first, z, x)], axis=0)

    def diag_blocks(x):
        return jnp.where(first, x[:L], x[L:])

    v_b = [_bf(v[p]) for p in ps]
    kap_b = [_bf(kap[p]) for p in ps]
    v_bd = [bd(v_b[p]) for p in ps]
    lhs = [jnp.concatenate([kap_b[p], _bf(rt[p])], axis=0) for p in ps]
    s = [mm_nt(lhs[p], jnp.concatenate([bd(_bf(bt[p])), bd(_bf(kt[p]))], axis=0)) for p in ps]
    n_b = [jnp.where(tri_strict, s[p][:L, :LANES], 0.0) for p in ps]
    a_rb = [_bf(jnp.where(tri_incl, s[p][L:, :LANES], 0.0)) for p in ps]
    n_k = [_bf(jnp.where(tri_strict, s[p][:L, LANES:], 0.0)) for p in ps]
    a_rk = [_bf(jnp.where(tri_incl, s[p][L:, LANES:], 0.0)) for p in ps]
    t = [jnp.where(eye, 1.0, 0.0) - jnp.where(masks[0], n_b[p], 0.0) for p in ps]
    for m in masks[1:]:
        t_b = [_bf(t[p]) for p in ps]
        ct = [_bf(mm(_bf(jnp.where(m, n_b[p], 0.0)), bd(t_b[p]))) for p in ps]
        t = [t[p] - mm(t_b[p], bd(ct[p])) for p in ps]
    t_b = [_bf(t[p]) for p in ps]
    nkv = [_bf(mm(n_k[p], v_bd[p])) for p in ps]
    eg = [_bf(-mm(t_b[p], jnp.concatenate([bd(nkv[p]), bd(kap_b[p])], axis=1))) for p in ps]
    e0 = [eg[p][:, :LANES] for p in ps]
    g = [eg[p][:, LANES:] for p in ps]
    y0 = [mm(jnp.concatenate([a_rb[p], a_rk[p]], axis=1), jnp.concatenate([bd(e0[p]), v_bd[p]], axis=0))
          for p in ps]
    qp = [rt[p] + mm(a_rb[p], bd(g[p])) for p in ps]
    btw = [_bf(bt[p] * wl[p]) for p in ps]
    ktw = [_bf(kt[p] * wl[p]) for p in ps]
    zeros = jnp.zeros_like(v_b[0])
    up = [mm_tn(jnp.concatenate([btw[p], ktw[p]], axis=0),
                jnp.concatenate([eg[p], jnp.concatenate([v_b[p], zeros], axis=1)], axis=0)) for p in ps]
    ut = [diag_blocks(up[p][:, :LANES]) for p in ps]
    pt = [jnp.where(eye, wl[p], 0.0) + diag_blocks(up[p][:, LANES:]) for p in ps]
    ys = [mm(_bf(jnp.concatenate([qp[p], pt[p]], axis=0)), bd(_bf(st[p]))) for p in ps]
    return [ys[p][:L] + y0[p] for p in ps], [ys[p][L:] + ut[p] for p in ps]


def _rwkv_scan_kernel(*refs, reverse, finish, n_heads, bb):
    if finish:
        (r_ref, lw_ref, k_ref, kk_ref, a_ref, v_ref, yf_ref, bonus_ref, gate_ref, vec_ref,
         o_ref, st_ref) = refs
    else:
        r_ref, lw_ref, k_ref, kk_ref, a_ref, v_ref, o_ref, st_ref = refs

    @pl.when(pl.program_id(1) == 0)
    def _():
        st_ref[...] = jnp.zeros_like(st_ref)

    L = r_ref.shape[1]
    hd = HEAD_DIM
    assert L == hd and 2 * hd == LANES
    n_pairs = n_heads // 2
    ri2 = lax.broadcasted_iota(jnp.int32, (L, LANES), 0)
    lane = lax.broadcasted_iota(jnp.int32, (L, LANES), 1)
    ci2 = lane & (L - 1)
    if reverse:
        tri_incl, tri_strict = ci2 >= ri2, ci2 > ri2
    else:
        tri_incl, tri_strict = ci2 <= ri2, ci2 < ri2
    eye = ri2 == ci2
    first = lane < hd
    masks = _level_masks(L, reverse)
    end = 0 if reverse else L - 1

    sls = [slice(p * LANES, (p + 1) * LANES) for p in range(n_pairs)]
    split = lambda x: [x[:, sl] for sl in sls]
    rt, kap, bt, kt, vs, wl, st = [], [], [], [], [], [], []
    for i in range(bb):
        lw = lw_ref[i]
        cum = _cumsum_rows(lw, reverse)
        w_t = jnp.exp(cum)
        w_i = jnp.exp(-cum)
        w_p = jnp.exp(cum - lw)
        kk = kk_ref[i]
        rt += split(r_ref[i] * w_t)
        kap += split(kk * w_p)
        bt += split(kk * a_ref[i] * w_i)
        kt += split(k_ref[i] * w_i)
        vs += split(v_ref[i])
        wl += split(w_t[end:end + 1])
        st += split(st_ref[i])
    outs, states = _rwkv_chunk_pairs(rt, kap, bt, kt, vs, wl, st, tri_strict, tri_incl, eye, masks, first)
    lrow = lax.broadcasted_iota(jnp.int32, (LANES, LANES), 0)
    lcol = lax.broadcasted_iota(jnp.int32, (LANES, LANES), 1)
    ones = jnp.where((lrow < hd) == (lcol < hd), 1.0, 0.0).astype(BF16)
    lane_sum = lambda x: jnp.dot(x, ones, preferred_element_type=F32)
    for i in range(bb):
        mine = slice(i * n_pairs, (i + 1) * n_pairs)
        st_ref[i] = jnp.concatenate(states[mine], axis=1)
        out = jnp.concatenate(outs[mine], axis=1)
        if finish:
            ys = jnp.concatenate(split(out + yf_ref[i]), axis=0)
            ys_hi = _bf(ys)
            mean = (lane_sum(ys_hi) + lane_sum(_bf(ys - ys_hi.astype(F32)))) * (1.0 / hd)
            cen = ys - mean
            var = lane_sum(_bf(cen * cen)) * (1.0 / hd)
            yn = cen * lax.rsqrt(var + RW_GN_EPS)
            out = jnp.concatenate([yn[p * L:(p + 1) * L] for p in range(n_pairs)], axis=1)
            out = (out * vec_ref[0:1] + vec_ref[1:2] + bonus_ref[i]) * gate_ref[i]
        o_ref[i] = out.astype(o_ref.dtype)


def rwkv_scan(pr, d, n_ctx, reverse, finish=None):
    r = pr['r']
    B, Tt, D = r.shape
    L = CHUNK
    nc, ncc = Tt // L, n_ctx // L
    bb = next(n for n in (SCAN_BATCH, 2, 1) if B % n == 0)
    tok = pl.BlockSpec((bb, L, D), lambda b, c: (b, _scan_order(c, ncc, nc, reverse), 0))
    ins = [r, pr['lw%d' % d], pr['k%d' % d], pr['kk'], pr['a%d' % d], pr['v']]
    specs = [tok] * 6
    if finish is not None:
        y_fwd, gn = finish
        vec = jnp.zeros((SUBLANES, D), F32).at[0].set(gn[0]).at[1].set(gn[1])
        ins += [y_fwd, pr['bonus'], pr['gate'], vec]
        specs += [tok, tok, tok, pl.BlockSpec((SUBLANES, D), lambda b, c: (0, 0))]
    kern = functools.partial(_rwkv_scan_kernel, reverse=reverse, finish=finish is not None,
                             n_heads=D // HEAD_DIM, bb=bb)
    return pl.pallas_call(
        kern,
        grid=(B // bb, nc),
        in_specs=specs,
        out_specs=tok,
        out_shape=jax.ShapeDtypeStruct((B, Tt, D), F32 if finish is None else BF16),
        scratch_shapes=[pltpu.VMEM((bb, HEAD_DIM, D), F32)],
        compiler_params=_cparams(2),
        name="rwkv_scan_rev" if reverse else "rwkv_scan_fwd",
    )(*ins)


def rwkv_layer(x, mod, gate, p, n_ctx, tb, latent_only=False):
    pr = rwkv_proj(x, mod, p, n_ctx, tb)
    y_fwd = rwkv_scan(pr, 0, n_ctx, reverse=False)
    z = rwkv_scan(pr, 1, n_ctx, reverse=True, finish=(y_fwd, p['gn']))
    return out_matmul([z], p['w_out'], x, gate, n_ctx, tb, latent_only)


GATE_COLS = 256


def _gate_tiles(W, bd):
    assert bd <= LANES and W % LANES == 0
    win = min(W, GATE_COLS + 2 * LANES)
    tiles = []
    for c0 in range(0, W, GATE_COLS):
        lo = min(max(c0 - LANES, 0), W - win)
        tiles.append((c0, min(GATE_COLS, W - c0), lo))
    return win, tiles


def _rglru_scan_kernel(*refs, reverse, finish, n_ctx_blocks, block_dim):
    if finish:
        (x_ref, xp_ref, xn_ref, cw_ref, vec_ref, wg_ref, hf_ref, gs_ref, o_ref,
         a_scr, b_scr, carry_ref) = refs
    else:
        x_ref, xp_ref, xn_ref, cw_ref, vec_ref, wg_ref, o_ref, a_scr, b_scr, carry_ref = refs
    c = pl.program_id(1)
    nb = pl.num_programs(1)
    t = _scan_order(c, n_ctx_blocks, nb, reverse)

    @pl.when(c == 0)
    def _():
        carry_ref[...] = jnp.zeros_like(carry_ref)

    x = x_ref[0]
    tb, W = x.shape
    first = jnp.logical_or(t == 0, t == n_ctx_blocks)
    last = jnp.logical_or(t == n_ctx_blocks - 1, t == nb - 1)
    xp = jnp.where(first, 0.0, xp_ref[0])
    xn = jnp.where(last, 0.0, xn_ref[0])
    row = lambda a, i: jnp.broadcast_to(a[i:i + 1], x.shape)
    rows = _rows(x.shape)
    x_m1 = _shift_down(x, 1, row(xp, SUBLANES - 1))
    x_m2 = _shift_down(x, 2, jnp.where(rows == 0, row(xp, SUBLANES - 2), row(xp, SUBLANES - 1)))
    x_p1 = _shift_up(x, 1, row(xn, 0))
    cw = lambda i: cw_ref[i:i + 1, :]
    xc = cw(4) + x_m2 * cw(0) + x_m1 * cw(1) + x * cw(2) + x_p1 * cw(3)
    win, tiles = _gate_tiles(W, block_dim)
    xc_b = _bf(xc)
    z = [jnp.dot(xc_b[:, lo:lo + win], wg_ref[j], preferred_element_type=F32)
         for j, (_, _, lo) in enumerate(tiles)]
    gate_r = jnp.concatenate([z[j][:, :cw_] for j, (_, cw_, _) in enumerate(tiles)], axis=1)
    gate_i = jnp.concatenate([z[j][:, GATE_COLS:GATE_COLS + cw_] for j, (_, cw_, _) in enumerate(tiles)],
                             axis=1)
    r = _sigmoid(gate_r + vec_ref[0:1])
    i = _sigmoid(gate_i + vec_ref[1:2])
    log_a = -LRU_C * r * _softplus(-vec_ref[2:3])
    a = jnp.exp(log_a)
    a_scr[...] = a
    b_scr[...] = jnp.sqrt(1.0 - a * a) * (i * xc)

    S = SUBLANES
    ng = tb // S
    sub = _rows((S, W))

    def group(gi, carry):
        g = (ng - 1 - gi) if reverse else gi
        a = a_scr[pl.ds(pl.multiple_of(g * S, S), S), :]
        b = b_scr[pl.ds(pl.multiple_of(g * S, S), S), :]
        for s in (1, 2, 4):
            if reverse:
                ok = sub < S - s
                a_s = jnp.where(ok, pltpu.roll(a, S - s, 0), 1.0)
                b_s = jnp.where(ok, pltpu.roll(b, S - s, 0), 0.0)
            else:
                ok = sub >= s
                a_s = jnp.where(ok, pltpu.roll(a, s, 0), 1.0)
                b_s = jnp.where(ok, pltpu.roll(b, s, 0), 0.0)
            b = a * b_s + b
            a = a * a_s
        h = b + a * carry
        b_scr[pl.ds(pl.multiple_of(g * S, S), S), :] = h
        e = 0 if reverse else S - 1
        return jnp.broadcast_to(h[e:e + 1], (S, W))

    carry_ref[...] = lax.fori_loop(0, ng, group, carry_ref[...])
    h_all = b_scr[...]
    if finish:
        h_all = (h_all + hf_ref[0]) * gs_ref[0]
    o_ref[0] = h_all.astype(o_ref.dtype)


def rglru_scan(xr, p, d, n_ctx, tb, reverse, finish=None):
    B, Tt, W = xr.shape
    nblk = p['gate_w'].shape[2]
    bd = W // nblk
    ncb = n_ctx // tb
    nb = Tt // tb
    nb8 = Tt // SUBLANES
    r8 = tb // SUBLANES

    def dense(wb):
        eye = jnp.eye(nblk, dtype=wb.dtype)
        return jnp.einsum('ncd,nm->ncmd', wb, eye).reshape(W, W)

    win, tiles = _gate_tiles(W, bd)
    w_r, w_i = dense(_bf(p['gate_w'][d, 0])), dense(_bf(p['gate_w'][d, 1]))
    pad = lambda m: jnp.pad(m, ((0, 0), (0, GATE_COLS - m.shape[1])))
    wg = _bf(jnp.stack([jnp.concatenate([pad(w_r[lo:lo + win, c0:c0 + cw_]), pad(w_i[lo:lo + win, c0:c0 + cw_])],
                                        axis=1) for c0, cw_, lo in tiles]))
    cw = jnp.zeros((SUBLANES, W), F32).at[0:4].set(p['conv_w']).at[4].set(p['conv_b'])
    vec = jnp.zeros((SUBLANES, W), F32).at[0:2].set(p['gate_b'][d]).at[2].set(p['lam'][d])
    order = lambda c: _scan_order(c, ncb, nb, reverse)
    tok = pl.BlockSpec((1, tb, W), lambda b, c: (b, order(c), 0))
    full = lambda shape: pl.BlockSpec(shape, lambda b, c: (0,) * len(shape))
    ins = [xr, xr, xr, cw, vec, wg]
    specs = [tok,
             pl.BlockSpec((1, SUBLANES, W), lambda b, c: (b, jnp.maximum(order(c) * r8 - 1, 0), 0)),
             pl.BlockSpec((1, SUBLANES, W), lambda b, c: (b, jnp.minimum((order(c) + 1) * r8, nb8 - 1), 0)),
             full((SUBLANES, W)), full((SUBLANES, W)), full((len(tiles), win, 2 * GATE_COLS))]
    if finish is not None:
        ins += list(finish)
        specs += [tok, tok]
    kern = functools.partial(_rglru_scan_kernel, reverse=reverse, finish=finish is not None,
                             n_ctx_blocks=ncb, block_dim=bd)
    return pl.pallas_call(
        kern,
        grid=(B, nb),
        in_specs=specs,
        out_specs=tok,
        out_shape=jax.ShapeDtypeStruct((B, Tt, W), F32 if finish is None else BF16),
        scratch_shapes=[pltpu.VMEM((tb, W), F32), pltpu.VMEM((tb, W), F32), pltpu.VMEM((SUBLANES, W), F32)],
        compiler_params=_cparams(2),
        name="rglru_scan_rev" if reverse else "rglru_scan_fwd",
    )(*ins)


def rglru_layer(x, mod, gate, p, n_ctx, tb, latent_only=False):
    W = p['conv_w'].shape[1]
    xr, gs = in_matmul(x, mod, p['norm_g'], p['w_in'], (W, W), (None, "silu"), n_ctx)
    h_fwd = rglru_scan(xr, p, 0, n_ctx, tb, reverse=False)
    z = rglru_scan(xr, p, 1, n_ctx, tb, reverse=True, finish=(h_fwd, gs))
    return out_matmul([z], p['w_out'], x, gate, n_ctx, tb, latent_only)


def _natten_in_kernel(x_ref, mod_ref, ng_ref, w_ref, cs_ref, g_ref, sel_ref, selt_ref,
                      qn_ref, qr_ref, kr_ref, vb_ref, gs_ref, *, n_ctx):
    t, tb, D = pl.program_id(1), x_ref.shape[1], x_ref.shape[2]
    hb = _bf(_norm_mod(x_ref[0], ng_ref[...], _seg_rows(mod_ref, t, tb, n_ctx, 0),
                       _seg_rows(mod_ref, t, tb, n_ctx, 1)))
    proj = lambda j: jnp.dot(hb, w_ref[:, j * D:(j + 1) * D], preferred_element_type=F32)
    reps = D // cs_ref.shape[2]
    cos = jnp.concatenate([cs_ref[0]] * reps, axis=1)
    sin = jnp.concatenate([cs_ref[1]] * reps, axis=1)
    lane = lax.broadcasted_iota(jnp.int32, cos.shape, 1)
    quarter = HEAD_DIM // 4
    low = (lane % (2 * quarter)) < quarter

    def prep(x, g):
        ms = _head_sum(x * x, sel_ref, selt_ref) * (1.0 / HEAD_DIM)
        xn = x * lax.rsqrt(ms + RMS_EPS) * g
        partner = jnp.where(low, pltpu.roll(xn, D - quarter, 1), pltpu.roll(xn, quarter, 1))
        return xn, xn * cos + partner * sin

    qn, qr = prep(proj(0), g_ref[0:1])
    _, kr = prep(proj(1), g_ref[1:2])
    scale = HEAD_DIM ** -0.5
    qn_ref[0] = _bf(qn * scale)
    qr_ref[0] = _bf(qr * scale)
    kr_ref[0] = _bf(kr)
    vb_ref[0] = _bf(proj(2))
    gs_ref[0] = _silu(proj(3))


def natten_in(x, mod, norm_g, w_in, qk_g, n_ctx, tb):
    B, Tt, D = x.shape
    H = D // HEAD_DIM
    T = Tt - n_ctx
    nfreq = HEAD_DIM // 4
    pos = jnp.arange(T)
    inv = ROPE_THETA ** (-jnp.arange(nfreq, dtype=F32) / nfreq)
    ang_r = (pos // GRID_W).astype(F32)[:, None] * inv
    ang_c = (pos % GRID_W).astype(F32)[:, None] * inv
    cos = jnp.concatenate([jnp.cos(ang_r)] * 2 + [jnp.cos(ang_c)] * 2, axis=1)
    sin = jnp.concatenate([-jnp.sin(ang_r), jnp.sin(ang_r), -jnp.sin(ang_c), jnp.sin(ang_c)], axis=1)
    cs = jnp.stack([jnp.concatenate([jnp.ones((n_ctx, HEAD_DIM), F32), cos], 0),
                    jnp.concatenate([jnp.zeros((n_ctx, HEAD_DIM), F32), sin], 0)])
    cs = jnp.concatenate([cs, cs], axis=2)
    g = jnp.zeros((SUBLANES, D), F32).at[0].set(jnp.tile(qk_g[0], H)).at[1].set(jnp.tile(qk_g[1], H))
    sel, selt = _head_selectors(D)
    tok = pl.BlockSpec((1, tb, D), lambda b, t: (b, t, 0))
    full = lambda shape: pl.BlockSpec(shape, lambda b, t: (0,) * len(shape))
    return pl.pallas_call(
        functools.partial(_natten_in_kernel, n_ctx=n_ctx),
        grid=(B, Tt // tb),
        in_specs=[tok,
                  pl.BlockSpec((1, 2, 2, D), lambda b, t: (b, 0, 0, 0)),
                  full((1, D)), full((D, 4 * D)),
                  pl.BlockSpec((2, tb, 2 * HEAD_DIM), lambda b, t: (0, t, 0)),
                  full((SUBLANES, D)), full((2 * D, LANES)), full((2 * LANES, D))],
        out_specs=[tok] * 5,
        out_shape=[jax.ShapeDtypeStruct((B, Tt, D), BF16)] * 4 + [jax.ShapeDtypeStruct((B, Tt, D), F32)],
        compiler_params=_cparams(2),
        name="natten_in",
    )(x, mod, norm_g.reshape(1, D), _bf(w_in), cs, g, sel, selt)


def _natten_kernel(qr_ref, qn_ref, kr_ref, v_ref, bias_ref, o_ref, *, rows, kh, rb, n_ctx):
    gw = GRID_W
    lane = lax.broadcasted_iota(jnp.int32, (gw, 2 * HEAD_DIM), 1)
    head_lanes = [lane < HEAD_DIM, lane >= HEAD_DIM]
    kc = kr_ref[0, 0:n_ctx, :]
    vc = v_ref[0, 0:n_ctx, :]
    nt = lambda a, b: lax.dot_general(a, b, (((1,), (1,)), ((), ())), preferred_element_type=F32)
    mm = lambda a, b: jnp.dot(a, b, preferred_element_type=F32)

    q_c = qn_ref[0, 0:n_ctx, :]
    lane_c = lax.broadcasted_iota(jnp.int32, q_c.shape, 1)
    o_c = []
    for h in range(2):
        mine = (lane_c >= HEAD_DIM) if h else (lane_c < HEAD_DIM)
        s = nt(jnp.where(mine, q_c, jnp.zeros_like(q_c)), kc)
        p = jnp.exp(s - s.max(axis=-1, keepdims=True))
        o_c.append(mm(_bf(p), vc) / p.sum(axis=-1, keepdims=True))
    o_ref[0, 0:n_ctx, :] = jnp.where(lane_c < HEAD_DIM, o_c[0], o_c[1])

    def row_group(g, carry):
        q0s, qrs, qns, kbs, vbs, biases = [], [], [], [], [], []
        for j in range(rb):
            r = g * rb + j
            start = jnp.clip(r - kh // 2, 0, rows - kh)
            d0 = start - r + kh - 1
            q0 = pl.multiple_of(n_ctx + r * gw, gw)
            k0 = pl.multiple_of(n_ctx + start * gw, gw)
            qr = qr_ref[0, pl.ds(q0, gw), :]
            qn = qn_ref[0, pl.ds(q0, gw), :]
            zero = jnp.zeros_like(qr)
            by_head = lambda q: jnp.concatenate([jnp.where(m, q, zero) for m in head_lanes], axis=0)
            q0s.append(q0)
            qrs.append(by_head(qr))
            qns.append(by_head(qn))
            kbs.append(kr_ref[0, pl.ds(k0, kh * gw), :])
            vbs.append(v_ref[0, pl.ds(k0, kh * gw), :])
            biases.append(jnp.concatenate([bias_ref[0, d0], bias_ref[1, d0]], axis=0))
        n = range(rb)
        s_band = [nt(qrs[i], kbs[i]) + biases[i] for i in n]
        s_ctx = [nt(qns[i], kc) for i in n]
        m = [jnp.maximum(s_band[i].max(axis=-1, keepdims=True), s_ctx[i].max(axis=-1, keepdims=True))
             for i in n]
        p_band = [jnp.exp(s_band[i] - m[i]) for i in n]
        p_ctx = [jnp.exp(s_ctx[i] - m[i]) for i in n]
        den = [p_band[i].sum(axis=-1, keepdims=True) + p_ctx[i].sum(axis=-1, keepdims=True) for i in n]
        o = [(mm(_bf(p_band[i]), vbs[i]) + mm(_bf(p_ctx[i]), vc)) / den[i] for i in n]
        for i in n:
            o_ref[0, pl.ds(q0s[i], gw), :] = jnp.where(head_lanes[0], o[i][:gw], o[i][gw:])
        return carry

    lax.fori_loop(0, rows // rb, row_group, 0)


def _natten_bias(rpb, rows, kh):
    cols = np.arange(GRID_W)
    c_start = np.clip(cols - WIN_W // 2, 0, GRID_W - WIN_W)
    col_ok = (cols[None, :] >= c_start[:, None]) & (cols[None, :] < c_start[:, None] + WIN_W)
    dc_idx = np.clip(cols[None, :] - cols[:, None] + WIN_W - 1, 0, 2 * WIN_W - 2)
    onehot = jnp.asarray(dc_idx[None] == np.arange(2 * WIN_W - 1)[:, None, None], F32)
    picked = jnp.einsum('hrd,dqk->hrqk', rpb.astype(F32), onehot, precision=lax.Precision.HIGHEST)
    by_dr = jnp.where(jnp.asarray(col_ok), picked, NEG_BIG)
    base = WIN_H - kh
    per_v = [jnp.concatenate([by_dr[:, base + v + i] for i in range(kh)], axis=-1) for v in range(kh)]
    return jnp.stack(per_v, axis=1)


def natten_attention(qr, qn, kr, vb, rpb, n_ctx):
    B, Tt, D = qr.shape
    T = Tt - n_ctx
    rows = T // GRID_W
    kh = min(WIN_H, rows)
    HP = D // (2 * HEAD_DIM)
    assert n_ctx % GRID_W == 0
    bias = _natten_bias(rpb, rows, kh)
    seq = pl.BlockSpec((1, Tt, 2 * HEAD_DIM), lambda hp, b: (b, 0, hp))
    rb = 8 if rows % 8 == 0 else 1
    kern = functools.partial(_natten_kernel, rows=rows, kh=kh, rb=rb, n_ctx=n_ctx)
    return pl.pallas_call(
        kern,
        grid=(HP, B),
        in_specs=[seq, seq, seq, seq,
                  pl.BlockSpec((2, kh, GRID_W, kh * GRID_W), lambda hp, b: (hp, 0, 0, 0))],
        out_specs=seq,
        out_shape=jax.ShapeDtypeStruct((B, Tt, D), F32),
        compiler_params=_cparams(2),
        name="natten",
    )(qr, qn, kr, vb, bias)


def natten_layer(x, mod, gate, p, n_ctx, tb, latent_only=False):
    qn, qr, kr, vb, gs = natten_in(x, mod, p['norm_g'], p['w_in'], p['qk_g'], n_ctx, tb)
    o = natten_attention(qr, qn, kr, vb, p['rpb'], n_ctx)
    return out_matmul([o, gs], p['w_out'], x, gate, n_ctx, tb, latent_only)


_LAYER_KEYS = (
    ('norm_g', 'ada_w', 'ada_b', 'w_in', 'mu', 'lora_b0', 'lora_down', 'lora_up', 'k_ka', 'r_k', 'gn', 'w_out'),
    ('norm_g', 'ada_w', 'ada_b', 'w_in', 'conv_w', 'conv_b', 'gate_w', 'gate_b', 'lam', 'w_out'),
    ('norm_g', 'ada_w', 'ada_b', 'w_in', 'qk_g', 'rpb', 'w_out'),
)
_LAYERS = (rwkv_layer, rglru_layer, natten_layer)


def _forward(x, c, ctx, c_ctx, layer_params, tb):
    B, T, D = x.shape
    n_ctx = ctx.shape[1]
    xs = jnp.concatenate([ctx, x], axis=1)
    m_rows = -(-(B + 1) // SUBLANES) * SUBLANES
    c_all = jnp.zeros((m_rows, D), F32).at[:B].set(c).at[B].set(c_ctx)
    for i, p in enumerate(layer_params):
        m = ada_mod(c_all, p['ada_w'], p['ada_b'])
        m_l = m[:B].reshape(B, 3, D)
        m_c = jnp.broadcast_to(m[B].reshape(1, 3, D), (B, 3, D))
        both = jnp.stack([m_c, m_l], axis=1)
        mod = jnp.stack([1.0 + both[:, :, 1], both[:, :, 0]], axis=2)
        gate = both[:, :, 2:3]
        xs = _LAYERS[i % 3](xs, mod, gate, p, n_ctx, tb, latent_only=i == len(layer_params) - 1)
    return xs


def kernel(x, c, ctx, c_ctx, l0_norm_g, l0_ada_w, l0_ada_b, l0_w_in, l0_mu, l0_lora_b0, l0_lora_down, l0_lora_up, l0_k_ka, l0_r_k, l0_gn, l0_w_out, l1_norm_g, l1_ada_w, l1_ada_b, l1_w_in, l1_conv_w, l1_conv_b, l1_gate_w, l1_gate_b, l1_lam, l1_w_out, l2_norm_g, l2_ada_w, l2_ada_b, l2_w_in, l2_qk_g, l2_rpb, l2_w_out, l3_norm_g, l3_ada_w, l3_ada_b, l3_w_in, l3_mu, l3_lora_b0, l3_lora_down, l3_lora_up, l3_k_ka, l3_r_k, l3_gn, l3_w_out):
    args = (l0_norm_g, l0_ada_w, l0_ada_b, l0_w_in, l0_mu, l0_lora_b0, l0_lora_down, l0_lora_up, l0_k_ka, l0_r_k, l0_gn, l0_w_out, l1_norm_g, l1_ada_w, l1_ada_b, l1_w_in, l1_conv_w, l1_conv_b, l1_gate_w, l1_gate_b, l1_lam, l1_w_out, l2_norm_g, l2_ada_w, l2_ada_b, l2_w_in, l2_qk_g, l2_rpb, l2_w_out, l3_norm_g, l3_ada_w, l3_ada_b, l3_w_in, l3_mu, l3_lora_b0, l3_lora_down, l3_lora_up, l3_k_ka, l3_r_k, l3_gn, l3_w_out)
    layer_params, pos = [], 0
    for i in range(4):
        keys = _LAYER_KEYS[i % 3]
        layer_params.append(dict(zip(keys, args[pos:pos + len(keys)])))
        pos += len(keys)
    return _forward(x, c, ctx, c_ctx, layer_params, tb=256)
```

```python
import functools

import jax
import jax.numpy as jnp
import numpy as np
from jax import lax
from jax.experimental import pallas as pl
from jax.experimental.pallas import tpu as pltpu

F32 = jnp.float32
BF16 = jnp.bfloat16

HEAD_DIM = 64
CHUNK = 64
SCAN_BATCH = 4
NATTEN_ROW_GROUP = 16
GRID_W = 64
WIN_H = 8
WIN_W = 16
ROPE_THETA = 10000.0
RMS_EPS = 1e-6
RW_GN_EPS = 64e-5
LRU_C = 8.0
DECAY_SCALE = float(np.exp(-0.5))
SUBLANES = 8
LANES = 128
NEG_BIG = -1e30
VMEM_LIMIT = 56 * 1024 * 1024
MM_ROWS_MAX = 544


def _cparams(n_axes):
    return pltpu.CompilerParams(
        dimension_semantics=("arbitrary",) * n_axes, vmem_limit_bytes=VMEM_LIMIT)


def _bf(x):
    return x.astype(BF16)


def _dot(a, b):
    return jnp.dot(_bf(a), _bf(b), preferred_element_type=F32)


def _cumsum_rows(x, reverse):
    n = x.shape[0]
    rows = _rows(x.shape)
    s = 1
    while s < n:
        if reverse:
            x = x + jnp.where(rows < n - s, pltpu.roll(x, n - s, 0), 0.0)
        else:
            x = x + jnp.where(rows >= s, pltpu.roll(x, s, 0), 0.0)
        s *= 2
    return x


def _sigmoid(x):
    return 0.5 * jnp.tanh(0.5 * x) + 0.5


def _silu(x):
    return x * _sigmoid(x)


def _softplus(x):
    return jnp.maximum(x, 0.0) + jnp.log(1.0 + jnp.exp(-jnp.abs(x)))


def _rows(shape):
    return lax.broadcasted_iota(jnp.int32, shape, 0)


def _shift_down(x, s, fill):
    return jnp.where(_rows(x.shape) >= s, pltpu.roll(x, s, 0), fill)


def _shift_up(x, s, fill):
    n = x.shape[0]
    return jnp.where(_rows(x.shape) < n - s, pltpu.roll(x, n - s, 0), fill)


def _norm_mod(xb, g, scale1, shift):
    xf = xb.astype(F32)
    ms = jnp.mean(xf * xf, axis=-1, keepdims=True)
    return xf * lax.rsqrt(ms + RMS_EPS) * g * scale1 + shift


def _ada_kernel(c_ref, w_ref, b_ref, o_ref):
    o_ref[...] = _dot(_silu(c_ref[...]), w_ref[...]) + b_ref[...]


def ada_mod(c_all, ada_w, ada_b):
    m, d = c_all.shape
    n = ada_w.shape[1]
    tn = d
    return pl.pallas_call(
        _ada_kernel,
        grid=(n // tn,),
        in_specs=[pl.BlockSpec((m, d), lambda j: (0, 0)),
                  pl.BlockSpec((d, tn), lambda j: (0, j)),
                  pl.BlockSpec((1, tn), lambda j: (0, j))],
        out_specs=pl.BlockSpec((m, tn), lambda j: (0, j)),
        out_shape=jax.ShapeDtypeStruct((m, n), F32),
        compiler_params=_cparams(1),
        name="ada_mod",
    )(c_all, _bf(ada_w), ada_b.reshape(1, n))


def _seg_of_block(t, n_ctx_blocks):
    return (t >= n_ctx_blocks).astype(jnp.int32)


def _scan_order(c, n_ctx_blocks, n_blocks, reverse):
    if not reverse:
        return c
    return jnp.where(c < n_ctx_blocks, n_ctx_blocks - 1 - c, n_blocks - 1 + n_ctx_blocks - c)


def _mm_rows(Tt):
    return max(d for d in range(2 * SUBLANES, MM_ROWS_MAX + 1, 2 * SUBLANES) if Tt % d == 0)


def _seg_rows(mod_ref, t, tb, n_ctx, i):
    latent = (t * tb + lax.broadcasted_iota(jnp.int32, (tb, 1), 0)) >= n_ctx
    return jnp.where(latent, mod_ref[0, 1, i:i + 1], mod_ref[0, 0, i:i + 1])


def _in_mm_kernel(x_ref, mod_ref, g_ref, w_ref, *o_refs, splits, acts, n_ctx):
    t, tb = pl.program_id(1), x_ref.shape[1]
    h = _norm_mod(x_ref[0], g_ref[...], _seg_rows(mod_ref, t, tb, n_ctx, 0), _seg_rows(mod_ref, t, tb, n_ctx, 1))
    hb = _bf(h)
    off = 0
    for o_ref, n, act in zip(o_refs, splits, acts):
        z = jnp.dot(hb, w_ref[:, off:off + n], preferred_element_type=F32)
        if act == "silu":
            z = _silu(z)
        o_ref[0] = z.astype(o_ref.dtype)
        off += n


def in_matmul(x, mod, norm_g, w, splits, acts, n_ctx):
    B, Tt, D = x.shape
    tb = _mm_rows(Tt)
    n = w.shape[1]
    assert sum(splits) == n
    kern = functools.partial(_in_mm_kernel, splits=tuple(splits), acts=tuple(acts), n_ctx=n_ctx)
    return pl.pallas_call(
        kern,
        grid=(B, Tt // tb),
        in_specs=[pl.BlockSpec((1, tb, D), lambda b, t: (b, t, 0)),
                  pl.BlockSpec((1, 2, 2, D), lambda b, t: (b, 0, 0, 0)),
                  pl.BlockSpec((1, D), lambda b, t: (0, 0)),
                  pl.BlockSpec((D, n), lambda b, t: (0, 0))],
        out_specs=[pl.BlockSpec((1, tb, s), lambda b, t: (b, t, 0)) for s in splits],
        out_shape=[jax.ShapeDtypeStruct((B, Tt, s), F32) for s in splits],
        compiler_params=_cparams(2),
        name="in_matmul",
    )(x, mod, norm_g.reshape(1, D), _bf(w))


def _out_mm_kernel(*refs, n_a, n_ctx, skip):
    a_refs = refs[:n_a]
    w_ref, x_ref, gate_ref, o_ref = refs[n_a:]
    a = a_refs[0][0]
    for r in a_refs[1:]:
        a = a * r[0]
    gate = _seg_rows(gate_ref, pl.program_id(1) + skip, x_ref.shape[1], n_ctx, 0)
    o_ref[0] = x_ref[0] + gate * _dot(a, w_ref[...])


def out_matmul(a_list, w, x, gate, n_ctx, tb, latent_only=False):
    B, Tt, D = x.shape
    K = w.shape[0]
    if latent_only:
        skip = n_ctx // tb
    else:
        tb, skip = _mm_rows(Tt), 0
    kern = functools.partial(_out_mm_kernel, n_a=len(a_list), n_ctx=n_ctx, skip=skip)
    return pl.pallas_call(
        kern,
        grid=(B, Tt // tb - skip),
        in_specs=[pl.BlockSpec((1, tb, K), lambda b, t: (b, t + skip, 0)) for _ in a_list] + [
            pl.BlockSpec((K, D), lambda b, t: (0, 0)),
            pl.BlockSpec((1, tb, D), lambda b, t: (b, t + skip, 0)),
            pl.BlockSpec((1, 2, 1, D), lambda b, t: (b, 0, 0, 0))],
        out_specs=pl.BlockSpec((1, tb, D), lambda b, t: (b, t, 0)),
        out_shape=jax.ShapeDtypeStruct((B, Tt - skip * tb, D), F32),
        compiler_params=_cparams(2),
        name="out_matmul",
    )(*a_list, _bf(w), x, gate)


_V_NORM_G, _V_MU, _V_KK, _V_KA, _V_B0, _V_RK = 0, 1, 7, 8, 9, 13
_V_ROWS = 16


def _head_sum(x, sel_ref, selt_ref):
    mm = lambda a, w: jnp.dot(a, w, preferred_element_type=F32)
    hi = _bf(x)
    lo = _bf(x - hi.astype(F32))
    s = mm(jnp.concatenate([hi, lo], axis=1), sel_ref[...])
    s_hi = _bf(s)
    s_lo = _bf(s - s_hi.astype(F32))
    return mm(jnp.concatenate([s_hi, s_lo], axis=1), selt_ref[...])


def _head_selectors(D):
    hid = jnp.arange(D) // HEAD_DIM
    sel = (hid[:, None] == jnp.arange(LANES)[None, :]).astype(BF16)
    return jnp.concatenate([sel, sel], axis=0), jnp.concatenate([sel.T, sel.T], axis=0)


def _rwkv_proj_kernel(x_ref, xp_ref, xn_ref, mod_ref, vec_ref, win_ref, dw_ref, da_ref, upw_ref,
                      upa_ref, sel_ref, selt_ref, r_ref, v_ref, g_ref, kk_ref, bonus_ref, lw0_ref,
                      a0_ref, k0_ref, lw1_ref, a1_ref, k1_ref, *, n_ctx_blocks):
    t = pl.program_id(1)
    nb = pl.num_programs(1)
    vec = lambda i: vec_ref[i:i + 1, :]
    scale1, shift = mod_ref[0, 0, 0:1], mod_ref[0, 0, 1:2]
    g = vec(_V_NORM_G)
    h = _norm_mod(x_ref[0], g, scale1, shift)
    first = jnp.logical_or(t == 0, t == n_ctx_blocks)
    last = jnp.logical_or(t == n_ctx_blocks - 1, t == nb - 1)
    hp = _norm_mod(xp_ref[0], g, scale1, shift)[SUBLANES - 1:SUBLANES]
    hn = _norm_mod(xn_ref[0], g, scale1, shift)[0:1]
    hp = jnp.where(first, 0.0, hp)
    hn = jnp.where(last, 0.0, hn)
    h_prev = _shift_down(h, 1, jnp.broadcast_to(hp, h.shape))
    h_next = _shift_up(h, 1, jnp.broadcast_to(hn, h.shape))
    xx = 0.5 * (h_prev + h_next) - h
    mix = lambda j: _bf(h + xx * vec(_V_MU + j))
    mm = lambda a, w: jnp.dot(a, w, preferred_element_type=F32)
    xr, xw, xk, xv, xa, xg = [mix(j) for j in range(6)]
    r = mm(xr, win_ref[0])
    k = mm(xk, win_ref[1])
    v = mm(xv, win_ref[2])
    r_ref[0] = r
    v_ref[0] = v
    g_ref[0] = _silu(mm(xg, win_ref[3]))
    kk_raw = k * vec(_V_KK)
    kk_ref[0] = kk_raw * jnp.minimum(lax.rsqrt(_head_sum(kk_raw * kk_raw, sel_ref, selt_ref)), 1e12)
    tw = _bf(jnp.tanh(mm(xw, dw_ref[...])))
    ta = _bf(mm(xa, da_ref[...]))
    kka = k * vec(_V_KA)
    k_rest = k - kka
    k_sum = None
    for d, (lw_ref, a_ref, kd_ref) in enumerate(((lw0_ref, a0_ref, k0_ref), (lw1_ref, a1_ref, k1_ref))):
        zw = vec(_V_B0 + 2 * d) + mm(tw, upw_ref[d])
        lw_ref[0] = -DECAY_SCALE * _sigmoid(zw)
        a_d = _sigmoid(vec(_V_B0 + 2 * d + 1) + mm(ta, upa_ref[d]))
        a_ref[0] = a_d
        k_d = k_rest + kka * a_d
        kd_ref[0] = k_d
        k_sum = k_d if k_sum is None else k_sum + k_d
    bonus_ref[0] = _head_sum(r * k_sum * vec(_V_RK), sel_ref, selt_ref) * v


def rwkv_proj(x, mod, p, n_ctx, tb):
    B, Tt, D = x.shape
    lora = p['lora_down'].shape[-1]
    ncb = n_ctx // tb
    nb8 = Tt // SUBLANES
    r8 = tb // SUBLANES
    vec = jnp.zeros((_V_ROWS, D), F32)
    vec = vec.at[_V_NORM_G].set(p['norm_g']).at[_V_MU:_V_MU + 6].set(p['mu'])
    vec = vec.at[_V_KK].set(p['k_ka'][0]).at[_V_KA].set(p['k_ka'][1])
    vec = vec.at[_V_B0:_V_B0 + 4].set(p['lora_b0'].reshape(4, D)).at[_V_RK].set(p['r_k'].reshape(D))
    sel, selt = _head_selectors(D)
    down, up = p['lora_down'], p['lora_up']
    dw = _bf(jnp.concatenate([down[0, 0], down[1, 0]], axis=1))
    da = _bf(jnp.concatenate([down[0, 1], down[1, 1]], axis=1))
    z = jnp.zeros((lora, D), F32)
    upw = _bf(jnp.stack([jnp.concatenate([up[0, 0], z], 0), jnp.concatenate([z, up[1, 0]], 0)]))
    upa = _bf(jnp.stack([jnp.concatenate([up[0, 1], z], 0), jnp.concatenate([z, up[1, 1]], 0)]))
    full = lambda shape: pl.BlockSpec(shape, lambda b, t: (0,) * len(shape))
    tok = pl.BlockSpec((1, tb, D), lambda b, t: (b, t, 0))
    kern = functools.partial(_rwkv_proj_kernel, n_ctx_blocks=ncb)
    outs = pl.pallas_call(
        kern,
        grid=(B, Tt // tb),
        in_specs=[tok,
                  pl.BlockSpec((1, SUBLANES, D), lambda b, t: (b, jnp.maximum(t * r8 - 1, 0), 0)),
                  pl.BlockSpec((1, SUBLANES, D), lambda b, t: (b, jnp.minimum((t + 1) * r8, nb8 - 1), 0)),
                  pl.BlockSpec((1, 1, 2, D), lambda b, t: (b, _seg_of_block(t, ncb), 0, 0)),
                  full((_V_ROWS, D)), full((4, D, D)), full((D, 2 * lora)), full((D, 2 * lora)),
                  full((2, 2 * lora, D)), full((2, 2 * lora, D)), full((2 * D, LANES)), full((2 * LANES, D))],
        out_specs=[tok] * 11,
        out_shape=[jax.ShapeDtypeStruct((B, Tt, D), F32)] * 11,
        compiler_params=_cparams(2),
        name="rwkv_proj",
    )(x, x, x, mod, vec, _bf(p['w_in']), dw, da, upw, upa, sel, selt)
    names = ('r', 'v', 'gate', 'kk', 'bonus', 'lw0', 'a0', 'k0', 'lw1', 'a1', 'k1')
    return dict(zip(names, outs))


def _level_masks(L, reverse):
    ri = lax.broadcasted_iota(jnp.int32, (L, 2 * L), 0)
    ci = lax.broadcasted_iota(jnp.int32, (L, 2 * L), 1) & (L - 1)
    if reverse:
        ri, ci = ci, ri
    masks = []
    for j in range(int(np.log2(L))):
        same = (ri >> (j + 1)) == (ci >> (j + 1))
        masks.append(same & (((ri >> j) & 1) == 1) & (((ci >> j) & 1) == 0))
    return masks


def _rwkv_chunk_pairs(rt, kap, bt, kt, v, wl, st, tri_strict, tri_incl, eye, masks, first):
    ps = range(len(rt))
    L = v[0].shape[0]
    mm = lambda a, b: jnp.dot(a, b, preferred_element_type=F32)
    mm_nt = lambda a, b: lax.dot_general(a, b, (((1,), (1,)), ((), ())), preferred_element_type=F32)
    mm_tn = lambda a, b: lax.dot_general(a, b, (((0,), (0,)), ((), ())), preferred_element_type=F32)

    def bd(x):
        z = jnp.zeros_like(x)
        return jnp.concatenate([jnp.where(first, x, z), jnp.where(first, z, x)], axis=0)

    def diag_blocks(x):
        return jnp.where(first, x[:L], x[L:])

    v_b = [_bf(v[p]) for p in ps]
    kap_b = [_bf(kap[p]) for p in ps]
    v_bd = [bd(v_b[p]) for p in ps]
    lhs = [jnp.concatenate([kap_b[p], _bf(rt[p])], axis=0) for p in ps]
    s = [mm_nt(lhs[p], jnp.concatenate([bd(_bf(bt[p])), bd(_bf(kt[p]))], axis=0)) for p in ps]
    n_b = [jnp.where(tri_strict, s[p][:L, :LANES], 0.0) for p in ps]
    a_rb = [_bf(jnp.where(tri_incl, s[p][L:, :LANES], 0.0)) for p in ps]
    n_k = [_bf(jnp.where(tri_strict, s[p][:L, LANES:], 0.0)) for p in ps]
    a_rk = [_bf(jnp.where(tri_incl, s[p][L:, LANES:], 0.0)) for p in ps]
    t = [jnp.where(eye, 1.0, 0.0) - jnp.where(masks[0], n_b[p], 0.0) for p in ps]
    for m in masks[1:]:
        t_b = [_bf(t[p]) for p in ps]
        ct = [_bf(mm(_bf(jnp.where(m, n_b[p], 0.0)), bd(t_b[p]))) for p in ps]
        t = [t[p] - mm(t_b[p], bd(ct[p])) for p in ps]
    t_b = [_bf(t[p]) for p in ps]
    nkv = [_bf(mm(n_k[p], v_bd[p])) for p in ps]
    eg = [_bf(-mm(t_b[p], jnp.concatenate([bd(nkv[p]), bd(kap_b[p])], axis=1))) for p in ps]
    e0 = [eg[p][:, :LANES] for p in ps]
    g = [eg[p][:, LANES:] for p in ps]
    y0 = [mm(jnp.concatenate([a_rb[p], a_rk[p]], axis=1), jnp.concatenate([bd(e0[p]), v_bd[p]], axis=0))
          for p in ps]
    qp = [rt[p] + mm(a_rb[p], bd(g[p])) for p in ps]
    btw = [_bf(bt[p] * wl[p]) for p in ps]
    ktw = [_bf(kt[p] * wl[p]) for p in ps]
    zeros = jnp.zeros_like(v_b[0])
    up = [mm_tn(jnp.concatenate([btw[p], ktw[p]], axis=0),
                jnp.concatenate([eg[p], jnp.concatenate([v_b[p], zeros], axis=1)], axis=0)) for p in ps]
    ut = [diag_blocks(up[p][:, :LANES]) for p in ps]
    pt = [jnp.where(eye, wl[p], 0.0) + diag_blocks(up[p][:, LANES:]) for p in ps]
    ys = [mm(_bf(jnp.concatenate([qp[p], pt[p]], axis=0)), bd(_bf(st[p]))) for p in ps]
    return [ys[p][:L] + y0[p] for p in ps], [ys[p][L:] + ut[p] for p in ps]


def _rwkv_scan_kernel(*refs, reverse, finish, n_heads, bb):
    if finish:
        (r_ref, lw_ref, k_ref, kk_ref, a_ref, v_ref, yf_ref, bonus_ref, gate_ref, vec_ref,
         o_ref, st_ref) = refs
    else:
        r_ref, lw_ref, k_ref, kk_ref, a_ref, v_ref, o_ref, st_ref = refs

    @pl.when(pl.program_id(1) == 0)
    def _():
        st_ref[...] = jnp.zeros_like(st_ref)

    L = r_ref.shape[1]
    hd = HEAD_DIM
    assert L == hd and 2 * hd == LANES
    n_pairs = n_heads // 2
    ri2 = lax.broadcasted_iota(jnp.int32, (L, LANES), 0)
    lane = lax.broadcasted_iota(jnp.int32, (L, LANES), 1)
    ci2 = lane & (L - 1)
    if reverse:
        tri_incl, tri_strict = ci2 >= ri2, ci2 > ri2
    else:
        tri_incl, tri_strict = ci2 <= ri2, ci2 < ri2
    eye = ri2 == ci2
    first = lane < hd
    masks = _level_masks(L, reverse)
    end = 0 if reverse else L - 1

    sls = [slice(p * LANES, (p + 1) * LANES) for p in range(n_pairs)]
    split = lambda x: [x[:, sl] for sl in sls]
    rt, kap, bt, kt, vs, wl, st = [], [], [], [], [], [], []
    for i in range(bb):
        lw = lw_ref[i]
        cum = _cumsum_rows(lw, reverse)
        w_t = jnp.exp(cum)
        w_i = jnp.exp(-cum)
        w_p = jnp.exp(cum - lw)
        kk = kk_ref[i]
        rt += split(r_ref[i] * w_t)
        kap += split(kk * w_p)
        bt += split(kk * a_ref[i] * w_i)
        kt += split(k_ref[i] * w_i)
        vs += split(v_ref[i])
        wl += split(w_t[end:end + 1])
        st += split(st_ref[i])
    outs, states = _rwkv_chunk_pairs(rt, kap, bt, kt, vs, wl, st, tri_strict, tri_incl, eye, masks, first)
    lrow = lax.broadcasted_iota(jnp.int32, (LANES, LANES), 0)
    lcol = lax.broadcasted_iota(jnp.int32, (LANES, LANES), 1)
    ones = jnp.where((lrow < hd) == (lcol < hd), 1.0, 0.0).astype(BF16)
    lane_sum = lambda x: jnp.dot(x, ones, preferred_element_type=F32)
    for i in range(bb):
        mine = slice(i * n_pairs, (i + 1) * n_pairs)
        st_ref[i] = jnp.concatenate(states[mine], axis=1)
        out = jnp.concatenate(outs[mine], axis=1)
        if finish:
            ys = jnp.concatenate(split(out + yf_ref[i]), axis=0)
            ys_hi = _bf(ys)
            mean = (lane_sum(ys_hi) + lane_sum(_bf(ys - ys_hi.astype(F32)))) * (1.0 / hd)
            cen = ys - mean
            var = lane_sum(_bf(cen * cen)) * (1.0 / hd)
            yn = cen * lax.rsqrt(var + RW_GN_EPS)
            out = jnp.concatenate([yn[p * L:(p + 1) * L] for p in range(n_pairs)], axis=1)
            out = (out * vec_ref[0:1] + vec_ref[1:2] + bonus_ref[i]) * gate_ref[i]
        o_ref[i] = out.astype(o_ref.dtype)


def rwkv_scan(pr, d, n_ctx, reverse, finish=None):
    r = pr['r']
    B, Tt, D = r.shape
    L = CHUNK
    nc, ncc = Tt // L, n_ctx // L
    bb = next(n for n in (SCAN_BATCH, 2, 1) if B % n == 0)
    tok = pl.BlockSpec((bb, L, D), lambda b, c: (b, _scan_order(c, ncc, nc, reverse), 0))
    ins = [r, pr['lw%d' % d], pr['k%d' % d], pr['kk'], pr['a%d' % d], pr['v']]
    specs = [tok] * 6
    if finish is not None:
        y_fwd, gn = finish
        vec = jnp.zeros((SUBLANES, D), F32).at[0].set(gn[0]).at[1].set(gn[1])
        ins += [y_fwd, pr['bonus'], pr['gate'], vec]
        specs += [tok, tok, tok, pl.BlockSpec((SUBLANES, D), lambda b, c: (0, 0))]
    kern = functools.partial(_rwkv_scan_kernel, reverse=reverse, finish=finish is not None,
                             n_heads=D // HEAD_DIM, bb=bb)
    return pl.pallas_call(
        kern,
        grid=(B // bb, nc),
        in_specs=specs,
        out_specs=tok,
        out_shape=jax.ShapeDtypeStruct((B, Tt, D), F32 if finish is None else BF16),
        scratch_shapes=[pltpu.VMEM((bb, HEAD_DIM, D), F32)],
        compiler_params=_cparams(2),
        name="rwkv_scan_rev" if reverse else "rwkv_scan_fwd",
    )(*ins)


def rwkv_layer(x, mod, gate, p, n_ctx, tb, latent_only=False):
    pr = rwkv_proj(x, mod, p, n_ctx, tb)
    y_fwd = rwkv_scan(pr, 0, n_ctx, reverse=False)
    z = rwkv_scan(pr, 1, n_ctx, reverse=True, finish=(y_fwd, p['gn']))
    return out_matmul([z], p['w_out'], x, gate, n_ctx, tb, latent_only)


GATE_COLS = 256
LRU_SCAN_UNROLL = 4


def _gate_tiles(W, bd):
    assert bd <= LANES and W % LANES == 0
    win = min(W, GATE_COLS + 2 * LANES)
    tiles = []
    for c0 in range(0, W, GATE_COLS):
        lo = min(max(c0 - LANES, 0), W - win)
        tiles.append((c0, min(GATE_COLS, W - c0), lo))
    return win, tiles


def _rglru_scan_kernel(*refs, reverse, finish, n_ctx_blocks, block_dim):
    if finish:
        (x_ref, xp_ref, xn_ref, cw_ref, vec_ref, wg_ref, hf_ref, gs_ref, o_ref,
         a_scr, b_scr, carry_ref) = refs
    else:
        x_ref, xp_ref, xn_ref, cw_ref, vec_ref, wg_ref, o_ref, a_scr, b_scr, carry_ref = refs
    c = pl.program_id(1)
    nb = pl.num_programs(1)
    t = _scan_order(c, n_ctx_blocks, nb, reverse)

    @pl.when(c == 0)
    def _():
        carry_ref[...] = jnp.zeros_like(carry_ref)

    x = x_ref[0]
    tb, W = x.shape
    first = jnp.logical_or(t == 0, t == n_ctx_blocks)
    last = jnp.logical_or(t == n_ctx_blocks - 1, t == nb - 1)
    xp = jnp.where(first, 0.0, xp_ref[0])
    xn = jnp.where(last, 0.0, xn_ref[0])
    row = lambda a, i: jnp.broadcast_to(a[i:i + 1], x.shape)
    rows = _rows(x.shape)
    x_m1 = _shift_down(x, 1, row(xp, SUBLANES - 1))
    x_m2 = _shift_down(x, 2, jnp.where(rows == 0, row(xp, SUBLANES - 2), row(xp, SUBLANES - 1)))
    x_p1 = _shift_up(x, 1, row(xn, 0))
    cw = lambda i: cw_ref[i:i + 1, :]
    xc = cw(4) + x_m2 * cw(0) + x_m1 * cw(1) + x * cw(2) + x_p1 * cw(3)
    win, tiles = _gate_tiles(W, block_dim)
    xc_b = _bf(xc)
    z = [jnp.dot(xc_b[:, lo:lo + win], wg_ref[j], preferred_element_type=F32)
         for j, (_, _, lo) in enumerate(tiles)]
    gate_r = jnp.concatenate([z[j][:, :cw_] for j, (_, cw_, _) in enumerate(tiles)], axis=1)
    gate_i = jnp.concatenate([z[j][:, GATE_COLS:GATE_COLS + cw_] for j, (_, cw_, _) in enumerate(tiles)],
                             axis=1)
    r = _sigmoid(gate_r + vec_ref[0:1])
    i = _sigmoid(gate_i + vec_ref[1:2])
    log_a = -LRU_C * r * _softplus(-vec_ref[2:3])
    a = jnp.exp(log_a)
    a_scr[...] = a
    b_scr[...] = jnp.sqrt(1.0 - a * a) * (i * xc)

    S = SUBLANES
    ng = tb // S
    sub = _rows((S, W))

    def group(gi, carry):
        g = (ng - 1 - gi) if reverse else gi
        a = a_scr[pl.ds(pl.multiple_of(g * S, S), S), :]
        b = b_scr[pl.ds(pl.multiple_of(g * S, S), S), :]
        for s in (1, 2, 4):
            if reverse:
                ok = sub < S - s
                a_s = jnp.where(ok, pltpu.roll(a, S - s, 0), 1.0)
                b_s = jnp.where(ok, pltpu.roll(b, S - s, 0), 0.0)
            else:
                ok = sub >= s
                a_s = jnp.where(ok, pltpu.roll(a, s, 0), 1.0)
                b_s = jnp.where(ok, pltpu.roll(b, s, 0), 0.0)
            b = a * b_s + b
            a = a * a_s
        h = b + a * carry
        b_scr[pl.ds(pl.multiple_of(g * S, S), S), :] = h
        e = 0 if reverse else S - 1
        return jnp.broadcast_to(h[e:e + 1], (S, W))

    carry_ref[...] = lax.fori_loop(0, ng, group, carry_ref[...], unroll=LRU_SCAN_UNROLL)
    h_all = b_scr[...]
    if finish:
        h_all = (h_all + hf_ref[0]) * gs_ref[0]
    o_ref[0] = h_all.astype(o_ref.dtype)


def rglru_scan(xr, p, d, n_ctx, tb, reverse, finish=None):
    B, Tt, W = xr.shape
    nblk = p['gate_w'].shape[2]
    bd = W // nblk
    ncb = n_ctx // tb
    nb = Tt // tb
    nb8 = Tt // SUBLANES
    r8 = tb // SUBLANES

    def dense(wb):
        eye = jnp.eye(nblk, dtype=wb.dtype)
        return jnp.einsum('ncd,nm->ncmd', wb, eye).reshape(W, W)

    win, tiles = _gate_tiles(W, bd)
    w_r, w_i = dense(_bf(p['gate_w'][d, 0])), dense(_bf(p['gate_w'][d, 1]))
    pad = lambda m: jnp.pad(m, ((0, 0), (0, GATE_COLS - m.shape[1])))
    wg = _bf(jnp.stack([jnp.concatenate([pad(w_r[lo:lo + win, c0:c0 + cw_]), pad(w_i[lo:lo + win, c0:c0 + cw_])],
                                        axis=1) for c0, cw_, lo in tiles]))
    cw = jnp.zeros((SUBLANES, W), F32).at[0:4].set(p['conv_w']).at[4].set(p['conv_b'])
    vec = jnp.zeros((SUBLANES, W), F32).at[0:2].set(p['gate_b'][d]).at[2].set(p['lam'][d])
    order = lambda c: _scan_order(c, ncb, nb, reverse)
    tok = pl.BlockSpec((1, tb, W), lambda b, c: (b, order(c), 0))
    full = lambda shape: pl.BlockSpec(shape, lambda b, c: (0,) * len(shape))
    ins = [xr, xr, xr, cw, vec, wg]
    specs = [tok,
             pl.BlockSpec((1, SUBLANES, W), lambda b, c: (b, jnp.maximum(order(c) * r8 - 1, 0), 0)),
             pl.BlockSpec((1, SUBLANES, W), lambda b, c: (b, jnp.minimum((order(c) + 1) * r8, nb8 - 1), 0)),
             full((SUBLANES, W)), full((SUBLANES, W)), full((len(tiles), win, 2 * GATE_COLS))]
    if finish is not None:
        ins += list(finish)
        specs += [tok, tok]
    kern = functools.partial(_rglru_scan_kernel, reverse=reverse, finish=finish is not None,
                             n_ctx_blocks=ncb, block_dim=bd)
    return pl.pallas_call(
        kern,
        grid=(B, nb),
        in_specs=specs,
        out_specs=tok,
        out_shape=jax.ShapeDtypeStruct((B, Tt, W), F32 if finish is None else BF16),
        scratch_shapes=[pltpu.VMEM((tb, W), F32), pltpu.VMEM((tb, W), F32), pltpu.VMEM((SUBLANES, W), F32)],
        compiler_params=_cparams(2),
        name="rglru_scan_rev" if reverse else "rglru_scan_fwd",
    )(*ins)


def rglru_layer(x, mod, gate, p, n_ctx, tb, latent_only=False):
    W = p['conv_w'].shape[1]
    xr, gs = in_matmul(x, mod, p['norm_g'], p['w_in'], (W, W), (None, "silu"), n_ctx)
    h_fwd = rglru_scan(xr, p, 0, n_ctx, tb, reverse=False)
    z = rglru_scan(xr, p, 1, n_ctx, tb, reverse=True, finish=(h_fwd, gs))
    return out_matmul([z], p['w_out'], x, gate, n_ctx, tb, latent_only)


def _natten_in_kernel(x_ref, mod_ref, ng_ref, w_ref, cs_ref, g_ref, sel_ref, selt_ref,
                      qn_ref, qr_ref, kr_ref, vb_ref, gs_ref, *, n_ctx):
    t, tb, D = pl.program_id(1), x_ref.shape[1], x_ref.shape[2]
    hb = _bf(_norm_mod(x_ref[0], ng_ref[...], _seg_rows(mod_ref, t, tb, n_ctx, 0),
                       _seg_rows(mod_ref, t, tb, n_ctx, 1)))
    proj = lambda j: jnp.dot(hb, w_ref[:, j * D:(j + 1) * D], preferred_element_type=F32)
    reps = D // cs_ref.shape[2]
    cos = jnp.concatenate([cs_ref[0]] * reps, axis=1)
    sin = jnp.concatenate([cs_ref[1]] * reps, axis=1)
    lane = lax.broadcasted_iota(jnp.int32, cos.shape, 1)
    quarter = HEAD_DIM // 4
    low = (lane % (2 * quarter)) < quarter

    def prep(x, g):
        ms = _head_sum(x * x, sel_ref, selt_ref) * (1.0 / HEAD_DIM)
        xn = x * lax.rsqrt(ms + RMS_EPS) * g
        partner = jnp.where(low, pltpu.roll(xn, D - quarter, 1), pltpu.roll(xn, quarter, 1))
        return xn, xn * cos + partner * sin

    qn, qr = prep(proj(0), g_ref[0:1])
    _, kr = prep(proj(1), g_ref[1:2])
    scale = HEAD_DIM ** -0.5
    qn_ref[0] = _bf(qn * scale)
    qr_ref[0] = _bf(qr * scale)
    kr_ref[0] = _bf(kr)
    vb_ref[0] = _bf(proj(2))
    gs_ref[0] = _silu(proj(3))


def natten_in(x, mod, norm_g, w_in, qk_g, n_ctx, tb):
    B, Tt, D = x.shape
    H = D // HEAD_DIM
    T = Tt - n_ctx
    nfreq = HEAD_DIM // 4
    pos = jnp.arange(T)
    inv = ROPE_THETA ** (-jnp.arange(nfreq, dtype=F32) / nfreq)
    ang_r = (pos // GRID_W).astype(F32)[:, None] * inv
    ang_c = (pos % GRID_W).astype(F32)[:, None] * inv
    cos = jnp.concatenate([jnp.cos(ang_r)] * 2 + [jnp.cos(ang_c)] * 2, axis=1)
    sin = jnp.concatenate([-jnp.sin(ang_r), jnp.sin(ang_r), -jnp.sin(ang_c), jnp.sin(ang_c)], axis=1)
    cs = jnp.stack([jnp.concatenate([jnp.ones((n_ctx, HEAD_DIM), F32), cos], 0),
                    jnp.concatenate([jnp.zeros((n_ctx, HEAD_DIM), F32), sin], 0)])
    cs = jnp.concatenate([cs, cs], axis=2)
    g = jnp.zeros((SUBLANES, D), F32).at[0].set(jnp.tile(qk_g[0], H)).at[1].set(jnp.tile(qk_g[1], H))
    sel, selt = _head_selectors(D)
    tok = pl.BlockSpec((1, tb, D), lambda b, t: (b, t, 0))
    full = lambda shape: pl.BlockSpec(shape, lambda b, t: (0,) * len(shape))
    return pl.pallas_call(
        functools.partial(_natten_in_kernel, n_ctx=n_ctx),
        grid=(B, Tt // tb),
        in_specs=[tok,
                  pl.BlockSpec((1, 2, 2, D), lambda b, t: (b, 0, 0, 0)),
                  full((1, D)), full((D, 4 * D)),
                  pl.BlockSpec((2, tb, 2 * HEAD_DIM), lambda b, t: (0, t, 0)),
                  full((SUBLANES, D)), full((2 * D, LANES)), full((2 * LANES, D))],
        out_specs=[tok] * 5,
        out_shape=[jax.ShapeDtypeStruct((B, Tt, D), BF16)] * 4 + [jax.ShapeDtypeStruct((B, Tt, D), F32)],
        compiler_params=_cparams(2),
        name="natten_in",
    )(x, mod, norm_g.reshape(1, D), _bf(w_in), cs, g, sel, selt)


def _natten_kernel(qr_ref, qn_ref, kr_ref, v_ref, bias_ref, o_ref, *, rows, kh, rb, n_ctx):
    gw = GRID_W
    lane = lax.broadcasted_iota(jnp.int32, (gw, 2 * HEAD_DIM), 1)
    head_lanes = [lane < HEAD_DIM, lane >= HEAD_DIM]
    kc = kr_ref[0, 0:n_ctx, :]
    vc = v_ref[0, 0:n_ctx, :]
    nt = lambda a, b: lax.dot_general(a, b, (((1,), (1,)), ((), ())), preferred_element_type=F32)
    mm = lambda a, b: jnp.dot(a, b, preferred_element_type=F32)

    q_c = qn_ref[0, 0:n_ctx, :]
    lane_c = lax.broadcasted_iota(jnp.int32, q_c.shape, 1)
    o_c = []
    for h in range(2):
        mine = (lane_c >= HEAD_DIM) if h else (lane_c < HEAD_DIM)
        s = nt(jnp.where(mine, q_c, jnp.zeros_like(q_c)), kc)
        p = jnp.exp(s - s.max(axis=-1, keepdims=True))
        o_c.append(mm(_bf(p), vc) / p.sum(axis=-1, keepdims=True))
    o_ref[0, 0:n_ctx, :] = jnp.where(lane_c < HEAD_DIM, o_c[0], o_c[1])

    def row_group(g, carry):
        q0s, qrs, qns, kbs, vbs, biases = [], [], [], [], [], []
        for j in range(rb):
            r = g * rb + j
            start = jnp.clip(r - kh // 2, 0, rows - kh)
            d0 = start - r + kh - 1
            q0 = pl.multiple_of(n_ctx + r * gw, gw)
            k0 = pl.multiple_of(n_ctx + start * gw, gw)
            qr = qr_ref[0, pl.ds(q0, gw), :]
            qn = qn_ref[0, pl.ds(q0, gw), :]
            zero = jnp.zeros_like(qr)
            by_head = lambda q: jnp.concatenate([jnp.where(m, q, zero) for m in head_lanes], axis=0)
            q0s.append(q0)
            qrs.append(by_head(qr))
            qns.append(by_head(qn))
            kbs.append(kr_ref[0, pl.ds(k0, kh * gw), :])
            vbs.append(v_ref[0, pl.ds(k0, kh * gw), :])
            biases.append(jnp.concatenate([bias_ref[0, d0], bias_ref[1, d0]], axis=0))
        n = range(rb)
        s_band = [nt(qrs[i], kbs[i]) + biases[i] for i in n]
        s_ctx = [nt(qns[i], kc) for i in n]
        m = [jnp.maximum(s_band[i].max(axis=-1, keepdims=True), s_ctx[i].max(axis=-1, keepdims=True))
             for i in n]
        p_band = [jnp.exp(s_band[i] - m[i]) for i in n]
        p_ctx = [jnp.exp(s_ctx[i] - m[i]) for i in n]
        den = [p_band[i].sum(axis=-1, keepdims=True) + p_ctx[i].sum(axis=-1, keepdims=True) for i in n]
        o = [(mm(_bf(p_band[i]), vbs[i]) + mm(_bf(p_ctx[i]), vc)) / den[i] for i in n]
        for i in n:
            o_ref[0, pl.ds(q0s[i], gw), :] = jnp.where(head_lanes[0], o[i][:gw], o[i][gw:])
        return carry

    lax.fori_loop(0, rows // rb, row_group, 0)


def _natten_bias(rpb, rows, kh):
    cols = np.arange(GRID_W)
    c_start = np.clip(cols - WIN_W // 2, 0, GRID_W - WIN_W)
    col_ok = (cols[None, :] >= c_start[:, None]) & (cols[None, :] < c_start[:, None] + WIN_W)
    dc_idx = np.clip(cols[None, :] - cols[:, None] + WIN_W - 1, 0, 2 * WIN_W - 2)
    onehot = jnp.asarray(dc_idx[None] == np.arange(2 * WIN_W - 1)[:, None, None], F32)
    picked = jnp.einsum('hrd,dqk->hrqk', rpb.astype(F32), onehot, precision=lax.Precision.HIGHEST)
    by_dr = jnp.where(jnp.asarray(col_ok), picked, NEG_BIG)
    base = WIN_H - kh
    per_v = [jnp.concatenate([by_dr[:, base + v + i] for i in range(kh)], axis=-1) for v in range(kh)]
    return jnp.stack(per_v, axis=1)


def natten_attention(qr, qn, kr, vb, rpb, n_ctx):
    B, Tt, D = qr.shape
    T = Tt - n_ctx
    rows = T // GRID_W
    kh = min(WIN_H, rows)
    HP = D // (2 * HEAD_DIM)
    assert n_ctx % GRID_W == 0
    bias = _natten_bias(rpb, rows, kh)
    seq = pl.BlockSpec((1, Tt, 2 * HEAD_DIM), lambda hp, b: (b, 0, hp))
    rb = next(n for n in (NATTEN_ROW_GROUP, 8, 4, 2, 1) if rows % n == 0)
    kern = functools.partial(_natten_kernel, rows=rows, kh=kh, rb=rb, n_ctx=n_ctx)
    return pl.pallas_call(
        kern,
        grid=(HP, B),
        in_specs=[seq, seq, seq, seq,
                  pl.BlockSpec((2, kh, GRID_W, kh * GRID_W), lambda hp, b: (hp, 0, 0, 0))],
        out_specs=seq,
        out_shape=jax.ShapeDtypeStruct((B, Tt, D), F32),
        compiler_params=_cparams(2),
        name="natten",
    )(qr, qn, kr, vb, bias)


def natten_layer(x, mod, gate, p, n_ctx, tb, latent_only=False):
    qn, qr, kr, vb, gs = natten_in(x, mod, p['norm_g'], p['w_in'], p['qk_g'], n_ctx, tb)
    o = natten_attention(qr, qn, kr, vb, p['rpb'], n_ctx)
    return out_matmul([o, gs], p['w_out'], x, gate, n_ctx, tb, latent_only)


_LAYER_KEYS = (
    ('norm_g', 'ada_w', 'ada_b', 'w_in', 'mu', 'lora_b0', 'lora_down', 'lora_up', 'k_ka', 'r_k', 'gn', 'w_out'),
    ('norm_g', 'ada_w', 'ada_b', 'w_in', 'conv_w', 'conv_b', 'gate_w', 'gate_b', 'lam', 'w_out'),
    ('norm_g', 'ada_w', 'ada_b', 'w_in', 'qk_g', 'rpb', 'w_out'),
)
_LAYERS = (rwkv_layer, rglru_layer, natten_layer)


def _forward(x, c, ctx, c_ctx, layer_params, tb):
    B, T, D = x.shape
    n_ctx = ctx.shape[1]
    xs = jnp.concatenate([ctx, x], axis=1)
    m_rows = -(-(B + 1) // SUBLANES) * SUBLANES
    c_all = jnp.zeros((m_rows, D), F32).at[:B].set(c).at[B].set(c_ctx)
    for i, p in enumerate(layer_params):
        m = ada_mod(c_all, p['ada_w'], p['ada_b'])
        m_l = m[:B].reshape(B, 3, D)
        m_c = jnp.broadcast_to(m[B].reshape(1, 3, D), (B, 3, D))
        both = jnp.stack([m_c, m_l], axis=1)
        mod = jnp.stack([1.0 + both[:, :, 1], both[:, :, 0]], axis=2)
        gate = both[:, :, 2:3]
        xs = _LAYERS[i % 3](xs, mod, gate, p, n_ctx, tb, latent_only=i == len(layer_params) - 1)
    return xs


def kernel(x, c, ctx, c_ctx, l0_norm_g, l0_ada_w, l0_ada_b, l0_w_in, l0_mu, l0_lora_b0, l0_lora_down, l0_lora_up, l0_k_ka, l0_r_k, l0_gn, l0_w_out, l1_norm_g, l1_ada_w, l1_ada_b, l1_w_in, l1_conv_w, l1_conv_b, l1_gate_w, l1_gate_b, l1_lam, l1_w_out, l2_norm_g, l2_ada_w, l2_ada_b, l2_w_in, l2_qk_g, l2_rpb, l2_w_out, l3_norm_g, l3_ada_w, l3_ada_b, l3_w_in, l3_mu, l3_lora_b0, l3_lora_down, l3_lora_up, l3_k_ka, l3_r_k, l3_gn, l3_w_out):
    args = (l0_norm_g, l0_ada_w, l0_ada_b, l0_w_in, l0_mu, l0_lora_b0, l0_lora_down, l0_lora_up, l0_k_ka, l0_r_k, l0_gn, l0_w_out, l1_norm_g, l1_ada_w, l1_ada_b, l1_w_in, l1_conv_w, l1_conv_b, l1_gate_w, l1_gate_b, l1_lam, l1_w_out, l2_norm_g, l2_ada_w, l2_ada_b, l2_w_in, l2_qk_g, l2_rpb, l2_w_out, l3_norm_g, l3_ada_w, l3_ada_b, l3_w_in, l3_mu, l3_lora_b0, l3_lora_down, l3_lora_up, l3_k_ka, l3_r_k, l3_gn, l3_w_out)
    layer_params, pos = [], 0
    for i in range(4):
        keys = _LAYER_KEYS[i % 3]
        layer_params.append(dict(zip(keys, args[pos:pos + len(keys)])))
        pos += len(keys)
    return _forward(x, c, ctx, c_ctx, layer_params, tb=256)
```

```python
import functools

import jax
import jax.numpy as jnp
import numpy as np
from jax import lax
from jax.experimental import pallas as pl
from jax.experimental.pallas import tpu as pltpu

F32 = jnp.float32
BF16 = jnp.bfloat16

HEAD_DIM = 64
CHUNK = 64
SCAN_BATCH = 4
NATTEN_ROW_GROUP = 32
GRID_W = 64
WIN_H = 8
WIN_W = 16
ROPE_THETA = 10000.0
RMS_EPS = 1e-6
RW_GN_EPS = 64e-5
LRU_C = 8.0
DECAY_SCALE = float(np.exp(-0.5))
SUBLANES = 8
LANES = 128
NEG_BIG = -1e30
VMEM_LIMIT = 56 * 1024 * 1024
MM_ROWS_MAX = 544


def _cparams(n_axes):
    return pltpu.CompilerParams(
        dimension_semantics=("arbitrary",) * n_axes, vmem_limit_bytes=VMEM_LIMIT)


def _bf(x):
    return x.astype(BF16)


def _dot(a, b):
    return jnp.dot(_bf(a), _bf(b), preferred_element_type=F32)


def _cumsum_rows(x, reverse):
    n = x.shape[0]
    rows = _rows(x.shape)
    s = 1
    while s < n:
        if reverse:
            x = x + jnp.where(rows < n - s, pltpu.roll(x, n - s, 0), 0.0)
        else:
            x = x + jnp.where(rows >= s, pltpu.roll(x, s, 0), 0.0)
        s *= 2
    return x


def _sigmoid(x):
    return 0.5 * jnp.tanh(0.5 * x) + 0.5


def _silu(x):
    return x * _sigmoid(x)


def _softplus(x):
    return jnp.maximum(x, 0.0) + jnp.log(1.0 + jnp.exp(-jnp.abs(x)))


def _rows(shape):
    return lax.broadcasted_iota(jnp.int32, shape, 0)


def _shift_down(x, s, fill):
    return jnp.where(_rows(x.shape) >= s, pltpu.roll(x, s, 0), fill)


def _shift_up(x, s, fill):
    n = x.shape[0]
    return jnp.where(_rows(x.shape) < n - s, pltpu.roll(x, n - s, 0), fill)


def _norm_mod(xb, g, scale1, shift):
    xf = xb.astype(F32)
    ms = jnp.mean(xf * xf, axis=-1, keepdims=True)
    return xf * lax.rsqrt(ms + RMS_EPS) * g * scale1 + shift


def _ada_kernel(c_ref, w_ref, b_ref, o_ref):
    o_ref[...] = _dot(_silu(c_ref[...]), w_ref[...]) + b_ref[...]


def ada_mod(c_all, ada_w, ada_b):
    m, d = c_all.shape
    n = ada_w.shape[1]
    tn = d
    return pl.pallas_call(
        _ada_kernel,
        grid=(n // tn,),
        in_specs=[pl.BlockSpec((m, d), lambda j: (0, 0)),
                  pl.BlockSpec((d, tn), lambda j: (0, j)),
                  pl.BlockSpec((1, tn), lambda j: (0, j))],
        out_specs=pl.BlockSpec((m, tn), lambda j: (0, j)),
        out_shape=jax.ShapeDtypeStruct((m, n), F32),
        compiler_params=_cparams(1),
        name="ada_mod",
    )(c_all, _bf(ada_w), ada_b.reshape(1, n))


def _seg_of_block(t, n_ctx_blocks):
    return (t >= n_ctx_blocks).astype(jnp.int32)


def _scan_order(c, n_ctx_blocks, n_blocks, reverse):
    if not reverse:
        return c
    return jnp.where(c < n_ctx_blocks, n_ctx_blocks - 1 - c, n_blocks - 1 + n_ctx_blocks - c)


def _mm_rows(Tt):
    return max(d for d in range(2 * SUBLANES, MM_ROWS_MAX + 1, 2 * SUBLANES) if Tt % d == 0)


def _seg_rows(mod_ref, t, tb, n_ctx, i):
    latent = (t * tb + lax.broadcasted_iota(jnp.int32, (tb, 1), 0)) >= n_ctx
    return jnp.where(latent, mod_ref[0, 1, i:i + 1], mod_ref[0, 0, i:i + 1])


def _in_mm_kernel(x_ref, mod_ref, g_ref, w_ref, *o_refs, splits, acts, n_ctx):
    t, tb = pl.program_id(1), x_ref.shape[1]
    h = _norm_mod(x_ref[0], g_ref[...], _seg_rows(mod_ref, t, tb, n_ctx, 0), _seg_rows(mod_ref, t, tb, n_ctx, 1))
    hb = _bf(h)
    off = 0
    for o_ref, n, act in zip(o_refs, splits, acts):
        z = jnp.dot(hb, w_ref[:, off:off + n], preferred_element_type=F32)
        if act == "silu":
            z = _silu(z)
        o_ref[0] = z.astype(o_ref.dtype)
        off += n


def in_matmul(x, mod, norm_g, w, splits, acts, n_ctx):
    B, Tt, D = x.shape
    tb = _mm_rows(Tt)
    n = w.shape[1]
    assert sum(splits) == n
    kern = functools.partial(_in_mm_kernel, splits=tuple(splits), acts=tuple(acts), n_ctx=n_ctx)
    return pl.pallas_call(
        kern,
        grid=(B, Tt // tb),
        in_specs=[pl.BlockSpec((1, tb, D), lambda b, t: (b, t, 0)),
                  pl.BlockSpec((1, 2, 2, D), lambda b, t: (b, 0, 0, 0)),
                  pl.BlockSpec((1, D), lambda b, t: (0, 0)),
                  pl.BlockSpec((D, n), lambda b, t: (0, 0))],
        out_specs=[pl.BlockSpec((1, tb, s), lambda b, t: (b, t, 0)) for s in splits],
        out_shape=[jax.ShapeDtypeStruct((B, Tt, s), F32) for s in splits],
        compiler_params=_cparams(2),
        name="in_matmul",
    )(x, mod, norm_g.reshape(1, D), _bf(w))


def _out_mm_kernel(*refs, n_a, n_ctx, skip):
    a_refs = refs[:n_a]
    w_ref, x_ref, gate_ref, o_ref = refs[n_a:]
    a = a_refs[0][0]
    for r in a_refs[1:]:
        a = a * r[0]
    gate = _seg_rows(gate_ref, pl.program_id(1) + skip, x_ref.shape[1], n_ctx, 0)
    o_ref[0] = x_ref[0] + gate * _dot(a, w_ref[...])


def out_matmul(a_list, w, x, gate, n_ctx, tb, latent_only=False):
    B, Tt, D = x.shape
    K = w.shape[0]
    if latent_only:
        skip = n_ctx // tb
    else:
        tb, skip = _mm_rows(Tt), 0
    kern = functools.partial(_out_mm_kernel, n_a=len(a_list), n_ctx=n_ctx, skip=skip)
    return pl.pallas_call(
        kern,
        grid=(B, Tt // tb - skip),
        in_specs=[pl.BlockSpec((1, tb, K), lambda b, t: (b, t + skip, 0)) for _ in a_list] + [
            pl.BlockSpec((K, D), lambda b, t: (0, 0)),
            pl.BlockSpec((1, tb, D), lambda b, t: (b, t + skip, 0)),
            pl.BlockSpec((1, 2, 1, D), lambda b, t: (b, 0, 0, 0))],
        out_specs=pl.BlockSpec((1, tb, D), lambda b, t: (b, t, 0)),
        out_shape=jax.ShapeDtypeStruct((B, Tt - skip * tb, D), F32),
        compiler_params=_cparams(2),
        name="out_matmul",
    )(*a_list, _bf(w), x, gate)


_V_NORM_G, _V_MU, _V_KK, _V_KA, _V_B0, _V_RK = 0, 1, 7, 8, 9, 13
_V_ROWS = 16


def _head_sum(x, sel_ref, selt_ref):
    mm = lambda a, w: jnp.dot(a, w, preferred_element_type=F32)
    hi = _bf(x)
    lo = _bf(x - hi.astype(F32))
    s = mm(jnp.concatenate([hi, lo], axis=1), sel_ref[...])
    s_hi = _bf(s)
    s_lo = _bf(s - s_hi.astype(F32))
    return mm(jnp.concatenate([s_hi, s_lo], axis=1), selt_ref[...])


def _head_selectors(D):
    hid = jnp.arange(D) // HEAD_DIM
    sel = (hid[:, None] == jnp.arange(LANES)[None, :]).astype(BF16)
    return jnp.concatenate([sel, sel], axis=0), jnp.concatenate([sel.T, sel.T], axis=0)


def _rwkv_proj_kernel(x_ref, xp_ref, xn_ref, mod_ref, vec_ref, win_ref, dw_ref, da_ref, upw_ref,
                      upa_ref, sel_ref, selt_ref, r_ref, v_ref, g_ref, kk_ref, bonus_ref, lw0_ref,
                      a0_ref, k0_ref, lw1_ref, a1_ref, k1_ref, *, n_ctx_blocks):
    t = pl.program_id(1)
    nb = pl.num_programs(1)
    vec = lambda i: vec_ref[i:i + 1, :]
    scale1, shift = mod_ref[0, 0, 0:1], mod_ref[0, 0, 1:2]
    g = vec(_V_NORM_G)
    h = _norm_mod(x_ref[0], g, scale1, shift)
    first = jnp.logical_or(t == 0, t == n_ctx_blocks)
    last = jnp.logical_or(t == n_ctx_blocks - 1, t == nb - 1)
    hp = _norm_mod(xp_ref[0], g, scale1, shift)[SUBLANES - 1:SUBLANES]
    hn = _norm_mod(xn_ref[0], g, scale1, shift)[0:1]
    hp = jnp.where(first, 0.0, hp)
    hn = jnp.where(last, 0.0, hn)
    h_prev = _shift_down(h, 1, jnp.broadcast_to(hp, h.shape))
    h_next = _shift_up(h, 1, jnp.broadcast_to(hn, h.shape))
    xx = 0.5 * (h_prev + h_next) - h
    mix = lambda j: _bf(h + xx * vec(_V_MU + j))
    mm = lambda a, w: jnp.dot(a, w, preferred_element_type=F32)
    xr, xw, xk, xv, xa, xg = [mix(j) for j in range(6)]
    r = mm(xr, win_ref[0])
    k = mm(xk, win_ref[1])
    v = mm(xv, win_ref[2])
    r_ref[0] = r
    v_ref[0] = v
    g_ref[0] = _silu(mm(xg, win_ref[3]))
    kk_raw = k * vec(_V_KK)
    kk_ref[0] = kk_raw * jnp.minimum(lax.rsqrt(_head_sum(kk_raw * kk_raw, sel_ref, selt_ref)), 1e12)
    tw = _bf(jnp.tanh(mm(xw, dw_ref[...])))
    ta = _bf(mm(xa, da_ref[...]))
    kka = k * vec(_V_KA)
    k_rest = k - kka
    k_sum = None
    for d, (lw_ref, a_ref, kd_ref) in enumerate(((lw0_ref, a0_ref, k0_ref), (lw1_ref, a1_ref, k1_ref))):
        zw = vec(_V_B0 + 2 * d) + mm(tw, upw_ref[d])
        lw_ref[0] = -DECAY_SCALE * _sigmoid(zw)
        a_d = _sigmoid(vec(_V_B0 + 2 * d + 1) + mm(ta, upa_ref[d]))
        a_ref[0] = a_d
        k_d = k_rest + kka * a_d
        kd_ref[0] = k_d
        k_sum = k_d if k_sum is None else k_sum + k_d
    bonus_ref[0] = _head_sum(r * k_sum * vec(_V_RK), sel_ref, selt_ref) * v


def rwkv_proj(x, mod, p, n_ctx, tb):
    B, Tt, D = x.shape
    lora = p['lora_down'].shape[-1]
    ncb = n_ctx // tb
    nb8 = Tt // SUBLANES
    r8 = tb // SUBLANES
    vec = jnp.zeros((_V_ROWS, D), F32)
    vec = vec.at[_V_NORM_G].set(p['norm_g']).at[_V_MU:_V_MU + 6].set(p['mu'])
    vec = vec.at[_V_KK].set(p['k_ka'][0]).at[_V_KA].set(p['k_ka'][1])
    vec = vec.at[_V_B0:_V_B0 + 4].set(p['lora_b0'].reshape(4, D)).at[_V_RK].set(p['r_k'].reshape(D))
    sel, selt = _head_selectors(D)
    down, up = p['lora_down'], p['lora_up']
    dw = _bf(jnp.concatenate([down[0, 0], down[1, 0]], axis=1))
    da = _bf(jnp.concatenate([down[0, 1], down[1, 1]], axis=1))
    z = jnp.zeros((lora, D), F32)
    upw = _bf(jnp.stack([jnp.concatenate([up[0, 0], z], 0), jnp.concatenate([z, up[1, 0]], 0)]))
    upa = _bf(jnp.stack([jnp.concatenate([up[0, 1], z], 0), jnp.concatenate([z, up[1, 1]], 0)]))
    full = lambda shape: pl.BlockSpec(shape, lambda b, t: (0,) * len(shape))
    tok = pl.BlockSpec((1, tb, D), lambda b, t: (b, t, 0))
    kern = functools.partial(_rwkv_proj_kernel, n_ctx_blocks=ncb)
    outs = pl.pallas_call(
        kern,
        grid=(B, Tt // tb),
        in_specs=[tok,
                  pl.BlockSpec((1, SUBLANES, D), lambda b, t: (b, jnp.maximum(t * r8 - 1, 0), 0)),
                  pl.BlockSpec((1, SUBLANES, D), lambda b, t: (b, jnp.minimum((t + 1) * r8, nb8 - 1), 0)),
                  pl.BlockSpec((1, 1, 2, D), lambda b, t: (b, _seg_of_block(t, ncb), 0, 0)),
                  full((_V_ROWS, D)), full((4, D, D)), full((D, 2 * lora)), full((D, 2 * lora)),
                  full((2, 2 * lora, D)), full((2, 2 * lora, D)), full((2 * D, LANES)), full((2 * LANES, D))],
        out_specs=[tok] * 11,
        out_shape=[jax.ShapeDtypeStruct((B, Tt, D), F32)] * 11,
        compiler_params=_cparams(2),
        name="rwkv_proj",
    )(x, x, x, mod, vec, _bf(p['w_in']), dw, da, upw, upa, sel, selt)
    names = ('r', 'v', 'gate', 'kk', 'bonus', 'lw0', 'a0', 'k0', 'lw1', 'a1', 'k1')
    return dict(zip(names, outs))


def _level_masks(L, reverse):
    ri = lax.broadcasted_iota(jnp.int32, (L, 2 * L), 0)
    ci = lax.broadcasted_iota(jnp.int32, (L, 2 * L), 1) & (L - 1)
    if reverse:
        ri, ci = ci, ri
    masks = []
    for j in range(int(np.log2(L))):
        same = (ri >> (j + 1)) == (ci >> (j + 1))
        masks.append(same & (((ri >> j) & 1) == 1) & (((ci >> j) & 1) == 0))
    return masks


def _rwkv_chunk_pairs(rt, kap, bt, kt, v, wl, st, tri_strict, tri_incl, eye, masks, first):
    ps = range(len(rt))
    L = v[0].shape[0]
    mm = lambda a, b: jnp.dot(a, b, preferred_element_type=F32)
    mm_nt = lambda a, b: lax.dot_general(a, b, (((1,), (1,)), ((), ())), preferred_element_type=F32)
    mm_tn = lambda a, b: lax.dot_general(a, b, (((0,), (0,)), ((), ())), preferred_element_type=F32)

    def bd(x):
        z = jnp.zeros_like(x)
        return jnp.concatenate([jnp.where(first, x, z), jnp.where(first, z, x)], axis=0)

    def diag_blocks(x):
        return jnp.where(first, x[:L], x[L:])

    v_b = [_bf(v[p]) for p in ps]
    kap_b = [_bf(kap[p]) for p in ps]
    v_bd = [bd(v_b[p]) for p in ps]
    lhs = [jnp.concatenate([kap_b[p], _bf(rt[p])], axis=0) for p in ps]
    s = [mm_nt(lhs[p], jnp.concatenate([bd(_bf(bt[p])), bd(_bf(kt[p]))], axis=0)) for p in ps]
    n_b = [jnp.where(tri_strict, s[p][:L, :LANES], 0.0) for p in ps]
    a_rb = [_bf(jnp.where(tri_incl, s[p][L:, :LANES], 0.0)) for p in ps]
    n_k = [_bf(jnp.where(tri_strict, s[p][:L, LANES:], 0.0)) for p in ps]
    a_rk = [_bf(jnp.where(tri_incl, s[p][L:, LANES:], 0.0)) for p in ps]
    t = [jnp.where(eye, 1.0, 0.0) - jnp.where(masks[0], n_b[p], 0.0) for p in ps]
    for m in masks[1:]:
        t_b = [_bf(t[p]) for p in ps]
        ct = [_bf(mm(_bf(jnp.where(m, n_b[p], 0.0)), bd(t_b[p]))) for p in ps]
        t = [t[p] - mm(t_b[p], bd(ct[p])) for p in ps]
    t_b = [_bf(t[p]) for p in ps]
    nkv = [_bf(mm(n_k[p], v_bd[p])) for p in ps]
    eg = [_bf(-mm(t_b[p], jnp.concatenate([bd(nkv[p]), bd(kap_b[p])], axis=1))) for p in ps]
    e0 = [eg[p][:, :LANES] for p in ps]
    g = [eg[p][:, LANES:] for p in ps]
    y0 = [mm(jnp.concatenate([a_rb[p], a_rk[p]], axis=1), jnp.concatenate([bd(e0[p]), v_bd[p]], axis=0))
          for p in ps]
    qp = [rt[p] + mm(a_rb[p], bd(g[p])) for p in ps]
    btw = [_bf(bt[p] * wl[p]) for p in ps]
    ktw = [_bf(kt[p] * wl[p]) for p in ps]
    zeros = jnp.zeros_like(v_b[0])
    up = [mm_tn(jnp.concatenate([btw[p], ktw[p]], axis=0),
                jnp.concatenate([eg[p], jnp.concatenate([v_b[p], zeros], axis=1)], axis=0)) for p in ps]
    ut = [diag_blocks(up[p][:, :LANES]) for p in ps]
    pt = [jnp.where(eye, wl[p], 0.0) + diag_blocks(up[p][:, LANES:]) for p in ps]
    ys = [mm(_bf(jnp.concatenate([qp[p], pt[p]], axis=0)), bd(_bf(st[p]))) for p in ps]
    return [ys[p][:L] + y0[p] for p in ps], [ys[p][L:] + ut[p] for p in ps]


def _rwkv_scan_kernel(*refs, reverse, finish, n_heads, bb):
    if finish:
        (r_ref, lw_ref, k_ref, kk_ref, a_ref, v_ref, yf_ref, bonus_ref, gate_ref, vec_ref,
         o_ref, st_ref) = refs
    else:
        r_ref, lw_ref, k_ref, kk_ref, a_ref, v_ref, o_ref, st_ref = refs

    @pl.when(pl.program_id(1) == 0)
    def _():
        st_ref[...] = jnp.zeros_like(st_ref)

    L = r_ref.shape[1]
    hd = HEAD_DIM
    assert L == hd and 2 * hd == LANES
    n_pairs = n_heads // 2
    ri2 = lax.broadcasted_iota(jnp.int32, (L, LANES), 0)
    lane = lax.broadcasted_iota(jnp.int32, (L, LANES), 1)
    ci2 = lane & (L - 1)
    if reverse:
        tri_incl, tri_strict = ci2 >= ri2, ci2 > ri2
    else:
        tri_incl, tri_strict = ci2 <= ri2, ci2 < ri2
    eye = ri2 == ci2
    first = lane < hd
    masks = _level_masks(L, reverse)
    end = 0 if reverse else L - 1

    sls = [slice(p * LANES, (p + 1) * LANES) for p in range(n_pairs)]
    split = lambda x: [x[:, sl] for sl in sls]
    rt, kap, bt, kt, vs, wl, st = [], [], [], [], [], [], []
    for i in range(bb):
        lw = lw_ref[i]
        cum = _cumsum_rows(lw, reverse)
        w_t = jnp.exp(cum)
        w_i = jnp.exp(-cum)
        w_p = jnp.exp(cum - lw)
        kk = kk_ref[i]
        rt += split(r_ref[i] * w_t)
        kap += split(kk * w_p)
        bt += split(kk * a_ref[i] * w_i)
        kt += split(k_ref[i] * w_i)
        vs += split(v_ref[i])
        wl += split(w_t[end:end + 1])
        st += split(st_ref[i])
    outs, states = _rwkv_chunk_pairs(rt, kap, bt, kt, vs, wl, st, tri_strict, tri_incl, eye, masks, first)
    lrow = lax.broadcasted_iota(jnp.int32, (LANES, LANES), 0)
    lcol = lax.broadcasted_iota(jnp.int32, (LANES, LANES), 1)
    ones = jnp.where((lrow < hd) == (lcol < hd), 1.0, 0.0).astype(BF16)
    lane_sum = lambda x: jnp.dot(x, ones, preferred_element_type=F32)
    for i in range(bb):
        mine = slice(i * n_pairs, (i + 1) * n_pairs)
        st_ref[i] = jnp.concatenate(states[mine], axis=1)
        out = jnp.concatenate(outs[mine], axis=1)
        if finish:
            ys = jnp.concatenate(split(out + yf_ref[i]), axis=0)
            ys_hi = _bf(ys)
            mean = (lane_sum(ys_hi) + lane_sum(_bf(ys - ys_hi.astype(F32)))) * (1.0 / hd)
            cen = ys - mean
            var = lane_sum(_bf(cen * cen)) * (1.0 / hd)
            yn = cen * lax.rsqrt(var + RW_GN_EPS)
            out = jnp.concatenate([yn[p * L:(p + 1) * L] for p in range(n_pairs)], axis=1)
            out = (out * vec_ref[0:1] + vec_ref[1:2] + bonus_ref[i]) * gate_ref[i]
        o_ref[i] = out.astype(o_ref.dtype)


def rwkv_scan(pr, d, n_ctx, reverse, finish=None):
    r = pr['r']
    B, Tt, D = r.shape
    L = CHUNK
    nc, ncc = Tt // L, n_ctx // L
    bb = next(n for n in ((1 if finish else 2) * SCAN_BATCH, SCAN_BATCH, 2, 1) if B % n == 0)
    tok = pl.BlockSpec((bb, L, D), lambda b, c: (b, _scan_order(c, ncc, nc, reverse), 0))
    ins = [r, pr['lw%d' % d], pr['k%d' % d], pr['kk'], pr['a%d' % d], pr['v']]
    specs = [tok] * 6
    if finish is not None:
        y_fwd, gn = finish
        vec = jnp.zeros((SUBLANES, D), F32).at[0].set(gn[0]).at[1].set(gn[1])
        ins += [y_fwd, pr['bonus'], pr['gate'], vec]
        specs += [tok, tok, tok, pl.BlockSpec((SUBLANES, D), lambda b, c: (0, 0))]
    kern = functools.partial(_rwkv_scan_kernel, reverse=reverse, finish=finish is not None,
                             n_heads=D // HEAD_DIM, bb=bb)
    return pl.pallas_call(
        kern,
        grid=(B // bb, nc),
        in_specs=specs,
        out_specs=tok,
        out_shape=jax.ShapeDtypeStruct((B, Tt, D), F32 if finish is None else BF16),
        scratch_shapes=[pltpu.VMEM((bb, HEAD_DIM, D), F32)],
        compiler_params=_cparams(2),
        name="rwkv_scan_rev" if reverse else "rwkv_scan_fwd",
    )(*ins)


def rwkv_layer(x, mod, gate, p, n_ctx, tb, latent_only=False):
    pr = rwkv_proj(x, mod, p, n_ctx, tb)
    y_fwd = rwkv_scan(pr, 0, n_ctx, reverse=False)
    z = rwkv_scan(pr, 1, n_ctx, reverse=True, finish=(y_fwd, p['gn']))
    return out_matmul([z], p['w_out'], x, gate, n_ctx, tb, latent_only)


GATE_COLS = 256
LRU_SCAN_UNROLL = 4


def _gate_tiles(W, bd):
    assert bd <= LANES and W % LANES == 0
    win = min(W, GATE_COLS + 2 * LANES)
    tiles = []
    for c0 in range(0, W, GATE_COLS):
        lo = min(max(c0 - LANES, 0), W - win)
        tiles.append((c0, min(GATE_COLS, W - c0), lo))
    return win, tiles


def _rglru_scan_kernel(*refs, reverse, finish, n_ctx_blocks, block_dim):
    if finish:
        (x_ref, xp_ref, xn_ref, cw_ref, vec_ref, wg_ref, hf_ref, gs_ref, o_ref,
         a_scr, b_scr, carry_ref) = refs
    else:
        x_ref, xp_ref, xn_ref, cw_ref, vec_ref, wg_ref, o_ref, a_scr, b_scr, carry_ref = refs
    c = pl.program_id(1)
    nb = pl.num_programs(1)
    t = _scan_order(c, n_ctx_blocks, nb, reverse)

    @pl.when(c == 0)
    def _():
        carry_ref[...] = jnp.zeros_like(carry_ref)

    x = x_ref[0]
    tb, W = x.shape
    first = jnp.logical_or(t == 0, t == n_ctx_blocks)
    last = jnp.logical_or(t == n_ctx_blocks - 1, t == nb - 1)
    xp = jnp.where(first, 0.0, xp_ref[0])
    xn = jnp.where(last, 0.0, xn_ref[0])
    row = lambda a, i: jnp.broadcast_to(a[i:i + 1], x.shape)
    rows = _rows(x.shape)
    x_m1 = _shift_down(x, 1, row(xp, SUBLANES - 1))
    x_m2 = _shift_down(x, 2, jnp.where(rows == 0, row(xp, SUBLANES - 2), row(xp, SUBLANES - 1)))
    x_p1 = _shift_up(x, 1, row(xn, 0))
    cw = lambda i: cw_ref[i:i + 1, :]
    xc = cw(4) + x_m2 * cw(0) + x_m1 * cw(1) + x * cw(2) + x_p1 * cw(3)
    win, tiles = _gate_tiles(W, block_dim)
    xc_b = _bf(xc)
    z = [jnp.dot(xc_b[:, lo:lo + win], wg_ref[j], preferred_element_type=F32)
         for j, (_, _, lo) in enumerate(tiles)]
    gate_r = jnp.concatenate([z[j][:, :cw_] for j, (_, cw_, _) in enumerate(tiles)], axis=1)
    gate_i = jnp.concatenate([z[j][:, GATE_COLS:GATE_COLS + cw_] for j, (_, cw_, _) in enumerate(tiles)],
                             axis=1)
    r = _sigmoid(gate_r + vec_ref[0:1])
    i = _sigmoid(gate_i + vec_ref[1:2])
    log_a = -LRU_C * r * _softplus(-vec_ref[2:3])
    a = jnp.exp(log_a)
    a_scr[...] = a
    b_scr[...] = jnp.sqrt(1.0 - a * a) * (i * xc)

    S = SUBLANES
    ng = tb // S
    sub = _rows((S, W))

    def group(gi, carry):
        g = (ng - 1 - gi) if reverse else gi
        a = a_scr[pl.ds(pl.multiple_of(g * S, S), S), :]
        b = b_scr[pl.ds(pl.multiple_of(g * S, S), S), :]
        for s in (1, 2, 4):
            if reverse:
                ok = sub < S - s
                a_s = jnp.where(ok, pltpu.roll(a, S - s, 0), 1.0)
                b_s = jnp.where(ok, pltpu.roll(b, S - s, 0), 0.0)
            else:
                ok = sub >= s
                a_s = jnp.where(ok, pltpu.roll(a, s, 0), 1.0)
                b_s = jnp.where(ok, pltpu.roll(b, s, 0), 0.0)
            b = a * b_s + b
            a = a * a_s
        h = b + a * carry
        b_scr[pl.ds(pl.multiple_of(g * S, S), S), :] = h
        e = 0 if reverse else S - 1
        return jnp.broadcast_to(h[e:e + 1], (S, W))

    carry_ref[...] = lax.fori_loop(0, ng, group, carry_ref[...], unroll=LRU_SCAN_UNROLL)
    h_all = b_scr[...]
    if finish:
        h_all = (h_all + hf_ref[0]) * gs_ref[0]
    o_ref[0] = h_all.astype(o_ref.dtype)


def rglru_scan(xr, p, d, n_ctx, tb, reverse, finish=None):
    B, Tt, W = xr.shape
    nblk = p['gate_w'].shape[2]
    bd = W // nblk
    ncb = n_ctx // tb
    nb = Tt // tb
    nb8 = Tt // SUBLANES
    r8 = tb // SUBLANES

    def dense(wb):
        eye = jnp.eye(nblk, dtype=wb.dtype)
        return jnp.einsum('ncd,nm->ncmd', wb, eye).reshape(W, W)

    win, tiles = _gate_tiles(W, bd)
    w_r, w_i = dense(_bf(p['gate_w'][d, 0])), dense(_bf(p['gate_w'][d, 1]))
    pad = lambda m: jnp.pad(m, ((0, 0), (0, GATE_COLS - m.shape[1])))
    wg = _bf(jnp.stack([jnp.concatenate([pad(w_r[lo:lo + win, c0:c0 + cw_]), pad(w_i[lo:lo + win, c0:c0 + cw_])],
                                        axis=1) for c0, cw_, lo in tiles]))
    cw = jnp.zeros((SUBLANES, W), F32).at[0:4].set(p['conv_w']).at[4].set(p['conv_b'])
    vec = jnp.zeros((SUBLANES, W), F32).at[0:2].set(p['gate_b'][d]).at[2].set(p['lam'][d])
    order = lambda c: _scan_order(c, ncb, nb, reverse)
    tok = pl.BlockSpec((1, tb, W), lambda b, c: (b, order(c), 0))
    full = lambda shape: pl.BlockSpec(shape, lambda b, c: (0,) * len(shape))
    ins = [xr, xr, xr, cw, vec, wg]
    specs = [tok,
             pl.BlockSpec((1, SUBLANES, W), lambda b, c: (b, jnp.maximum(order(c) * r8 - 1, 0), 0)),
             pl.BlockSpec((1, SUBLANES, W), lambda b, c: (b, jnp.minimum((order(c) + 1) * r8, nb8 - 1), 0)),
             full((SUBLANES, W)), full((SUBLANES, W)), full((len(tiles), win, 2 * GATE_COLS))]
    if finish is not None:
        ins += list(finish)
        specs += [tok, tok]
    kern = functools.partial(_rglru_scan_kernel, reverse=reverse, finish=finish is not None,
                             n_ctx_blocks=ncb, block_dim=bd)
    return pl.pallas_call(
        kern,
        grid=(B, nb),
        in_specs=specs,
        out_specs=tok,
        out_shape=jax.ShapeDtypeStruct((B, Tt, W), F32 if finish is None else BF16),
        scratch_shapes=[pltpu.VMEM((tb, W), F32), pltpu.VMEM((tb, W), F32), pltpu.VMEM((SUBLANES, W), F32)],
        compiler_params=_cparams(2),
        name="rglru_scan_rev" if reverse else "rglru_scan_fwd",
    )(*ins)


def rglru_layer(x, mod, gate, p, n_ctx, tb, latent_only=False):
    W = p['conv_w'].shape[1]
    xr, gs = in_matmul(x, mod, p['norm_g'], p['w_in'], (W, W), (None, "silu"), n_ctx)
    h_fwd = rglru_scan(xr, p, 0, n_ctx, tb, reverse=False)
    z = rglru_scan(xr, p, 1, n_ctx, tb, reverse=True, finish=(h_fwd, gs))
    return out_matmul([z], p['w_out'], x, gate, n_ctx, tb, latent_only)


def _natten_in_kernel(x_ref, mod_ref, ng_ref, w_ref, cs_ref, g_ref, sel_ref, selt_ref,
                      qn_ref, qr_ref, kr_ref, vb_ref, gs_ref, *, n_ctx):
    t, tb, D = pl.program_id(1), x_ref.shape[1], x_ref.shape[2]
    hb = _bf(_norm_mod(x_ref[0], ng_ref[...], _seg_rows(mod_ref, t, tb, n_ctx, 0),
                       _seg_rows(mod_ref, t, tb, n_ctx, 1)))
    proj = lambda j: jnp.dot(hb, w_ref[:, j * D:(j + 1) * D], preferred_element_type=F32)
    reps = D // cs_ref.shape[2]
    cos = jnp.concatenate([cs_ref[0]] * reps, axis=1)
    sin = jnp.concatenate([cs_ref[1]] * reps, axis=1)
    lane = lax.broadcasted_iota(jnp.int32, cos.shape, 1)
    quarter = HEAD_DIM // 4
    low = (lane % (2 * quarter)) < quarter

    def prep(x, g):
        ms = _head_sum(x * x, sel_ref, selt_ref) * (1.0 / HEAD_DIM)
        xn = x * lax.rsqrt(ms + RMS_EPS) * g
        partner = jnp.where(low, pltpu.roll(xn, D - quarter, 1), pltpu.roll(xn, quarter, 1))
        return xn, xn * cos + partner * sin

    qn, qr = prep(proj(0), g_ref[0:1])
    _, kr = prep(proj(1), g_ref[1:2])
    scale = HEAD_DIM ** -0.5
    qn_ref[0] = _bf(qn * scale)
    qr_ref[0] = _bf(qr * scale)
    kr_ref[0] = _bf(kr)
    vb_ref[0] = _bf(proj(2))
    gs_ref[0] = _silu(proj(3))


def natten_in(x, mod, norm_g, w_in, qk_g, n_ctx, tb):
    B, Tt, D = x.shape
    H = D // HEAD_DIM
    T = Tt - n_ctx
    nfreq = HEAD_DIM // 4
    pos = jnp.arange(T)
    inv = ROPE_THETA ** (-jnp.arange(nfreq, dtype=F32) / nfreq)
    ang_r = (pos // GRID_W).astype(F32)[:, None] * inv
    ang_c = (pos % GRID_W).astype(F32)[:, None] * inv
    cos = jnp.concatenate([jnp.cos(ang_r)] * 2 + [jnp.cos(ang_c)] * 2, axis=1)
    sin = jnp.concatenate([-jnp.sin(ang_r), jnp.sin(ang_r), -jnp.sin(ang_c), jnp.sin(ang_c)], axis=1)
    cs = jnp.stack([jnp.concatenate([jnp.ones((n_ctx, HEAD_DIM), F32), cos], 0),
                    jnp.concatenate([jnp.zeros((n_ctx, HEAD_DIM), F32), sin], 0)])
    cs = jnp.concatenate([cs, cs], axis=2)
    g = jnp.zeros((SUBLANES, D), F32).at[0].set(jnp.tile(qk_g[0], H)).at[1].set(jnp.tile(qk_g[1], H))
    sel, selt = _head_selectors(D)
    tok = pl.BlockSpec((1, tb, D), lambda b, t: (b, t, 0))
    full = lambda shape: pl.BlockSpec(shape, lambda b, t: (0,) * len(shape))
    return pl.pallas_call(
        functools.partial(_natten_in_kernel, n_ctx=n_ctx),
        grid=(B, Tt // tb),
        in_specs=[tok,
                  pl.BlockSpec((1, 2, 2, D), lambda b, t: (b, 0, 0, 0)),
                  full((1, D)), full((D, 4 * D)),
                  pl.BlockSpec((2, tb, 2 * HEAD_DIM), lambda b, t: (0, t, 0)),
                  full((SUBLANES, D)), full((2 * D, LANES)), full((2 * LANES, D))],
        out_specs=[tok] * 5,
        out_shape=[jax.ShapeDtypeStruct((B, Tt, D), BF16)] * 4 + [jax.ShapeDtypeStruct((B, Tt, D), F32)],
        compiler_params=_cparams(2),
        name="natten_in",
    )(x, mod, norm_g.reshape(1, D), _bf(w_in), cs, g, sel, selt)


def _natten_kernel(qr_ref, qn_ref, kr_ref, v_ref, bias_ref, o_ref, *, rows, kh, rb, n_ctx):
    gw = GRID_W
    lane = lax.broadcasted_iota(jnp.int32, (gw, 2 * HEAD_DIM), 1)
    head_lanes = [lane < HEAD_DIM, lane >= HEAD_DIM]
    kc = kr_ref[0, 0:n_ctx, :]
    vc = v_ref[0, 0:n_ctx, :]
    nt = lambda a, b: lax.dot_general(a, b, (((1,), (1,)), ((), ())), preferred_element_type=F32)
    mm = lambda a, b: jnp.dot(a, b, preferred_element_type=F32)

    q_c = qn_ref[0, 0:n_ctx, :]
    lane_c = lax.broadcasted_iota(jnp.int32, q_c.shape, 1)
    o_c = []
    for h in range(2):
        mine = (lane_c >= HEAD_DIM) if h else (lane_c < HEAD_DIM)
        s = nt(jnp.where(mine, q_c, jnp.zeros_like(q_c)), kc)
        p = jnp.exp(s - s.max(axis=-1, keepdims=True))
        o_c.append(mm(_bf(p), vc) / p.sum(axis=-1, keepdims=True))
    o_ref[0, 0:n_ctx, :] = jnp.where(lane_c < HEAD_DIM, o_c[0], o_c[1])

    def row_group(g, carry):
        q0s, qrs, qns, kbs, vbs, biases = [], [], [], [], [], []
        for j in range(rb):
            r = g * rb + j
            start = jnp.clip(r - kh // 2, 0, rows - kh)
            d0 = start - r + kh - 1
            q0 = pl.multiple_of(n_ctx + r * gw, gw)
            k0 = pl.multiple_of(n_ctx + start * gw, gw)
            qr = qr_ref[0, pl.ds(q0, gw), :]
            qn = qn_ref[0, pl.ds(q0, gw), :]
            zero = jnp.zeros_like(qr)
            by_head = lambda q: jnp.concatenate([jnp.where(m, q, zero) for m in head_lanes], axis=0)
            q0s.append(q0)
            qrs.append(by_head(qr))
            qns.append(by_head(qn))
            kbs.append(kr_ref[0, pl.ds(k0, kh * gw), :])
            vbs.append(v_ref[0, pl.ds(k0, kh * gw), :])
            biases.append(jnp.concatenate([bias_ref[0, d0], bias_ref[1, d0]], axis=0))
        n = range(rb)
        s_band = [nt(qrs[i], kbs[i]) + biases[i] for i in n]
        s_ctx = [nt(qns[i], kc) for i in n]
        m = [jnp.maximum(s_band[i].max(axis=-1, keepdims=True), s_ctx[i].max(axis=-1, keepdims=True))
             for i in n]
        p_band = [jnp.exp(s_band[i] - m[i]) for i in n]
        p_ctx = [jnp.exp(s_ctx[i] - m[i]) for i in n]
        den = [p_band[i].sum(axis=-1, keepdims=True) + p_ctx[i].sum(axis=-1, keepdims=True) for i in n]
        o = [(mm(_bf(p_band[i]), vbs[i]) + mm(_bf(p_ctx[i]), vc)) / den[i] for i in n]
        for i in n:
            o_ref[0, pl.ds(q0s[i], gw), :] = jnp.where(head_lanes[0], o[i][:gw], o[i][gw:])
        return carry

    lax.fori_loop(0, rows // rb, row_group, 0)


def _natten_bias(rpb, rows, kh):
    cols = np.arange(GRID_W)
    c_start = np.clip(cols - WIN_W // 2, 0, GRID_W - WIN_W)
    col_ok = (cols[None, :] >= c_start[:, None]) & (cols[None, :] < c_start[:, None] + WIN_W)
    dc_idx = np.clip(cols[None, :] - cols[:, None] + WIN_W - 1, 0, 2 * WIN_W - 2)
    onehot = jnp.asarray(dc_idx[None] == np.arange(2 * WIN_W - 1)[:, None, None], F32)
    picked = jnp.einsum('hrd,dqk->hrqk', rpb.astype(F32), onehot, precision=lax.Precision.HIGHEST)
    by_dr = jnp.where(jnp.asarray(col_ok), picked, NEG_BIG)
    base = WIN_H - kh
    per_v = [jnp.concatenate([by_dr[:, base + v + i] for i in range(kh)], axis=-1) for v in range(kh)]
    return jnp.stack(per_v, axis=1)


def natten_attention(qr, qn, kr, vb, rpb, n_ctx):
    B, Tt, D = qr.shape
    T = Tt - n_ctx
    rows = T // GRID_W
    kh = min(WIN_H, rows)
    HP = D // (2 * HEAD_DIM)
    assert n_ctx % GRID_W == 0
    bias = _natten_bias(rpb, rows, kh)
    seq = pl.BlockSpec((1, Tt, 2 * HEAD_DIM), lambda hp, b: (b, 0, hp))
    rb = next(n for n in (NATTEN_ROW_GROUP, 16, 8, 4, 2, 1) if rows % n == 0)
    kern = functools.partial(_natten_kernel, rows=rows, kh=kh, rb=rb, n_ctx=n_ctx)
    return pl.pallas_call(
        kern,
        grid=(HP, B),
        in_specs=[seq, seq, seq, seq,
                  pl.BlockSpec((2, kh, GRID_W, kh * GRID_W), lambda hp, b: (hp, 0, 0, 0))],
        out_specs=seq,
        out_shape=jax.ShapeDtypeStruct((B, Tt, D), F32),
        compiler_params=_cparams(2),
        name="natten",
    )(qr, qn, kr, vb, bias)


def natten_layer(x, mod, gate, p, n_ctx, tb, latent_only=False):
    qn, qr, kr, vb, gs = natten_in(x, mod, p['norm_g'], p['w_in'], p['qk_g'], n_ctx, tb)
    o = natten_attention(qr, qn, kr, vb, p['rpb'], n_ctx)
    return out_matmul([o, gs], p['w_out'], x, gate, n_ctx, tb, latent_only)


_LAYER_KEYS = (
    ('norm_g', 'ada_w', 'ada_b', 'w_in', 'mu', 'lora_b0', 'lora_down', 'lora_up', 'k_ka', 'r_k', 'gn', 'w_out'),
    ('norm_g', 'ada_w', 'ada_b', 'w_in', 'conv_w', 'conv_b', 'gate_w', 'gate_b', 'lam', 'w_out'),
    ('norm_g', 'ada_w', 'ada_b', 'w_in', 'qk_g', 'rpb', 'w_out'),
)
_LAYERS = (rwkv_layer, rglru_layer, natten_layer)


def _forward(x, c, ctx, c_ctx, layer_params, tb):
    B, T, D = x.shape
    n_ctx = ctx.shape[1]
    xs = jnp.concatenate([ctx, x], axis=1)
    m_rows = -(-(B + 1) // SUBLANES) * SUBLANES
    c_all = jnp.zeros((m_rows, D), F32).at[:B].set(c).at[B].set(c_ctx)
    for i, p in enumerate(layer_params):
        m = ada_mod(c_all, p['ada_w'], p['ada_b'])
        m_l = m[:B].reshape(B, 3, D)
        m_c = jnp.broadcast_to(m[B].reshape(1, 3, D), (B, 3, D))
        both = jnp.stack([m_c, m_l], axis=1)
        mod = jnp.stack([1.0 + both[:, :, 1], both[:, :, 0]], axis=2)
        gate = both[:, :, 2:3]
        xs = _LAYERS[i % 3](xs, mod, gate, p, n_ctx, tb, latent_only=i == len(layer_params) - 1)
    return xs


def kernel(x, c, ctx, c_ctx, l0_norm_g, l0_ada_w, l0_ada_b, l0_w_in, l0_mu, l0_lora_b0, l0_lora_down, l0_lora_up, l0_k_ka, l0_r_k, l0_gn, l0_w_out, l1_norm_g, l1_ada_w, l1_ada_b, l1_w_in, l1_conv_w, l1_conv_b, l1_gate_w, l1_gate_b, l1_lam, l1_w_out, l2_norm_g, l2_ada_w, l2_ada_b, l2_w_in, l2_qk_g, l2_rpb, l2_w_out, l3_norm_g, l3_ada_w, l3_ada_b, l3_w_in, l3_mu, l3_lora_b0, l3_lora_down, l3_lora_up, l3_k_ka, l3_r_k, l3_gn, l3_w_out):
    args = (l0_norm_g, l0_ada_w, l0_ada_b, l0_w_in, l0_mu, l0_lora_b0, l0_lora_down, l0_lora_up, l0_k_ka, l0_r_k, l0_gn, l0_w_out, l1_norm_g, l1_ada_w, l1_ada_b, l1_w_in, l1_conv_w, l1_conv_b, l1_gate_w, l1_gate_b, l1_lam, l1_w_out, l2_norm_g, l2_ada_w, l2_ada_b, l2_w_in, l2_qk_g, l2_rpb, l2_w_out, l3_norm_g, l3_ada_w, l3_ada_b, l3_w_in, l3_mu, l3_lora_b0, l3_lora_down, l3_lora_up, l3_k_ka, l3_r_k, l3_gn, l3_w_out)
    layer_params, pos = [], 0
    for i in range(4):
        keys = _LAYER_KEYS[i % 3]
        layer_params.append(dict(zip(keys, args[pos:pos + len(keys)])))
        pos += len(keys)
    return _forward(x, c, ctx, c_ctx, layer_params, tb=256)
```
